```python
import math
import jax
import jax.numpy as jnp
from jax import lax
import numpy as np

D_MODEL = 4096
BATCH = 4
SEQ = 2048
DEPTH = 2
DEC_BATCH = 8
DEC_SEQ = 1
PAST_LEN = 16384
PAGE_SIZE = 128

N_MIXERS = 2
N_RET_LAYERS = (DEPTH + 1) // 2
N_NSA_LAYERS = DEPTH // 2

DN_ALPHA = (2.0 * DEPTH) ** 0.25
DN_BETA = (8.0 * DEPTH) ** -0.25
LN_EPS = 1e-5
NEG_INF = -1e30
NEG_SCORE = -1e9

RET_HEADS = 16
RET_DK = D_MODEL // RET_HEADS
RET_DV = 2 * RET_DK
RET_CHUNK = 128
ROPE_BASE = 10000.0

NSA_HEADS = 32
HEAD_DIM = D_MODEL // NSA_HEADS
NSA_KV_HEADS = 4
NSA_GROUP = NSA_HEADS // NSA_KV_HEADS
NSA_KV = NSA_KV_HEADS * HEAD_DIM
CMP_LEN = 32
CMP_STRIDE = 16
CMP_HIDDEN = 2 * HEAD_DIM
SEL_BLK = 64
SEL_TOPK = 16
WINDOW = 512
QBLK = 128
FORCE_BONUS = 1e6

REL_BUCKETS = 32
REL_MAX_DIST = 128

N_EXPERTS = 32
TOP_K = 4
D_EXPERT = D_MODEL // 2
SWIGLU_ALPHA = 1.702
SWIGLU_LIMIT = 7.0
MOE_ROWS = 128
MOE_ROWS_SMALL = 8

RET_IN = 2 * RET_HEADS * RET_DK + 2 * RET_HEADS * RET_DV
NSA_IN = NSA_HEADS * HEAD_DIM + 6 * NSA_KV + 3 * NSA_HEADS

kernel_name = 'hybrid_retnet_nsa_moe_step'

F32 = jnp.float32


def layer_norm(x, g, b):
    xf = x.astype(F32)
    mu = jnp.mean(xf, -1, keepdims=True)
    var = jnp.mean(jnp.square(xf - mu), -1, keepdims=True)
    return ((xf - mu) * lax.rsqrt(var + LN_EPS) * g + b).astype(x.dtype)


def ada_mod(c, w, b):
    m = (jax.nn.silu(c) @ w + b)[:, None, :]
    return jnp.split(m, 3, axis=-1)


def rotary(x, pos):
    half = x.shape[-1] // 2
    inv = 1.0 / (ROPE_BASE ** (jnp.arange(half, dtype=F32) / half))
    ang = pos.astype(F32)[:, None] * inv[None, :]
    cos = jnp.cos(ang)[None, :, None, :]
    sin = jnp.sin(ang)[None, :, None, :]
    x1, x2 = x[..., :half], x[..., half:]
    return jnp.concatenate([x1 * cos - x2 * sin, x1 * sin + x2 * cos], -1)


def retention_chunkwise(q, k, v, s0):
    b, t = q.shape[:2]
    c = RET_CHUNK if t % RET_CHUNK == 0 else t
    nc = t // c
    lg = jnp.log1p(-jnp.exp2(-5.0 - jnp.arange(RET_HEADS, dtype=F32)))
    i = jnp.arange(c, dtype=F32)
    diff = i[:, None] - i[None, :]
    decay = jnp.where(diff >= 0, jnp.exp(jnp.maximum(diff, 0.0)[None] * lg[:, None, None]), 0.0)
    q_dec = jnp.exp((i + 1.0)[None, :] * lg[:, None]).T
    k_dec = jnp.exp((c - 1.0 - i)[None, :] * lg[:, None]).T
    c_dec = jnp.exp(c * lg)

    def to_chunks(a):
        return a.reshape((b, nc, c) + a.shape[2:]).transpose((1, 0, 2, 3, 4))

    def step(s, xs):
        qc, kc, vc = xs
        att = jnp.einsum('bihd,bjhd->bhij', qc, kc) * decay[None]
        o = jnp.einsum('bhij,bjhe->bihe', att, vc) + jnp.einsum('bihd,bhde->bihe', qc, s) * q_dec[None, :, :, None]
        s = s * c_dec[None, :, None, None] + jnp.einsum('bjhd,bjhe->bhde', kc * k_dec[None, :, :, None], vc)
        return s, o

    s, o = lax.scan(step, s0, (to_chunks(q), to_chunks(k), to_chunks(v)))
    o = o.transpose((1, 0, 2, 3, 4)).reshape(b, t, RET_HEADS, RET_DV)
    return o, s


def retention_mixer(h, pos, s0, w_in, gn_w, w_out):
    b, t, _ = h.shape
    z = h @ w_in
    nk = RET_HEADS * RET_DK
    nv = RET_HEADS * RET_DV
    q = z[..., :nk].reshape(b, t, RET_HEADS, RET_DK).astype(F32)
    k = z[..., nk:2 * nk].reshape(b, t, RET_HEADS, RET_DK).astype(F32)
    v = z[..., 2 * nk:2 * nk + nv].reshape(b, t, RET_HEADS, RET_DV).astype(F32)
    g = z[..., 2 * nk + nv:]
    q = rotary(q, pos)
    k = rotary(k, pos) * (RET_DK ** -0.5)
    o, s = retention_chunkwise(q, k, v, s0.astype(F32))
    mu = jnp.mean(o, -1, keepdims=True)
    var = jnp.mean(jnp.square(o - mu), -1, keepdims=True)
    o = ((o - mu) * lax.rsqrt(var + LN_EPS)).reshape(b, t, nv) * gn_w
    y = (jax.nn.silu(g) * o.astype(h.dtype)) @ w_out
    return y, s.astype(s0.dtype)


def t5_bucket(dist):
    n = jnp.maximum(dist, 0)
    exact = REL_BUCKETS // 2
    nf = jnp.maximum(n, 1).astype(F32)
    large = exact + (jnp.log(nf / exact) / math.log(REL_MAX_DIST / exact) * (REL_BUCKETS - exact)).astype(jnp.int32)
    large = jnp.minimum(large, REL_BUCKETS - 1)
    return jnp.where(n < exact, n, large)


def rel_bias_gr(dist, table):
    return table[t5_bucket(dist)].astype(F32).reshape(dist.shape + (NSA_KV_HEADS, NSA_GROUP))


def nsa_project(h, w_in):
    b, t, _ = h.shape
    z = h @ w_in
    nq = NSA_HEADS * HEAD_DIM
    q = z[..., :nq].reshape(b, t, NSA_HEADS, HEAD_DIM)
    kv = z[..., nq:nq + 6 * NSA_KV].reshape(b, t, 6, NSA_KV_HEADS, HEAD_DIM)
    gates = jax.nn.sigmoid(z[..., nq + 6 * NSA_KV:].astype(F32)).reshape(b, t, NSA_HEADS, 3)
    return q, kv, gates


def compress(k, pe, w1, b1, w2, b2):
    b, tk = k.shape[:2]
    ncb = (tk - CMP_LEN) // CMP_STRIDE + 1
    per = CMP_LEN // CMP_STRIDE
    nch = ncb + per - 1
    ch = k[:, :nch * CMP_STRIDE].reshape(b, nch, CMP_STRIDE, NSA_KV_HEADS, HEAD_DIM)
    ch = ch.transpose((0, 1, 3, 2, 4)).reshape(b, nch, NSA_KV_HEADS, CMP_STRIDE * HEAD_DIM)
    w1p = w1.reshape(per, CMP_STRIDE * HEAD_DIM, CMP_HIDDEN)
    hid = b1 + pe.reshape(-1) @ w1
    for m in range(per):
        hid = hid + ch[:, m:m + ncb] @ w1p[m]
    return jax.nn.gelu(hid) @ w2 + b2


def cmp_attention(q, kcmp, vcmp, q_pos, table):
    b, tq = q.shape[:2]
    ncb = kcmp.shape[1]
    qg = q.reshape(b, tq, NSA_KV_HEADS, NSA_GROUP, HEAD_DIM)
    dist = q_pos[:, None] - (jnp.arange(ncb) * CMP_STRIDE + CMP_LEN - 1)[None, :]
    visible = dist >= 0
    logits = jnp.einsum('btgrd,bngd->btgrn', qg, kcmp, preferred_element_type=F32) * (HEAD_DIM ** -0.5)
    logits = logits + rel_bias_gr(dist, table).transpose((0, 2, 3, 1))
    logits = jnp.where(visible[:, None, None, :], logits, NEG_INF)
    p = jax.nn.softmax(logits, axis=-1) * jnp.any(visible, -1).astype(F32)[:, None, None, None]
    o = jnp.einsum('btgrn,bngd->btgrd', p.astype(vcmp.dtype), vcmp)
    return o.reshape(b, tq, NSA_HEADS, HEAD_DIM), p


def select_blocks(p_cmp, q_pos, tk):
    ncb = p_cmp.shape[-1]
    nsb = -(-tk // SEL_BLK)
    i = jnp.arange(ncb)[:, None]
    j = jnp.arange(nsb)[None, :]
    cover = ((i * CMP_STRIDE < (j + 1) * SEL_BLK) & (i * CMP_STRIDE + CMP_LEN > j * SEL_BLK)).astype(F32)
    score = jnp.einsum('btgrn,nj->btgj', p_cmp, cover)
    cur = (q_pos // SEL_BLK)[:, None]
    forced = (j == 0) | (j == cur) | (j == cur - 1)
    score = jnp.where((j > cur)[:, None, :], NEG_SCORE, score + jnp.where(forced, FORCE_BONUS, 0.0)[:, None, :])
    _, idx = lax.top_k(score, min(SEL_TOPK, nsb))
    return idx


def sel_attention(q, sel_idx, q_pos, ks, vs, table):
    b, tq = q.shape[:2]
    tk = ks.shape[1]
    nsb = -(-tk // SEL_BLK)
    pad = ((0, 0), (0, nsb * SEL_BLK - tk), (0, 0), (0, 0))
    kb = jnp.pad(ks, pad).reshape(b, nsb, SEL_BLK, NSA_KV_HEADS, HEAD_DIM).transpose((0, 3, 1, 2, 4))
    vb = jnp.pad(vs, pad).reshape(b, nsb, SEL_BLK, NSA_KV_HEADS, HEAD_DIM).transpose((0, 3, 1, 2, 4))
    qb = QBLK if tq % QBLK == 0 else tq
    nqb = tq // qb
    nsel = sel_idx.shape[-1]
    q_items = q.reshape(b * nqb, qb, NSA_KV_HEADS, NSA_GROUP, HEAD_DIM)
    idx_items = sel_idx.reshape(b * nqb, qb, NSA_KV_HEADS, nsel)
    pos_items = jnp.broadcast_to(q_pos.reshape(1, nqb, qb), (b, nqb, qb)).reshape(b * nqb, qb)
    b_items = jnp.repeat(jnp.arange(b), nqb)
    table_gr = table.reshape(REL_BUCKETS, NSA_KV_HEADS, NSA_GROUP)
    g_ix = jnp.arange(NSA_KV_HEADS)[None, :, None]
    g_ix4 = jnp.arange(NSA_KV_HEADS)[None, :, None, None]

    def item(args):
        bi, qi, ii, pi = args
        kg = kb[bi, g_ix, ii]
        vg = vb[bi, g_ix, ii]
        kpos = ii[..., None] * SEL_BLK + jnp.arange(SEL_BLK)
        dist = pi[:, None, None, None] - kpos
        logits = jnp.einsum('qgrd,qgnsd->qgrns', qi, kg, preferred_element_type=F32) * (HEAD_DIM ** -0.5)
        logits = logits + table_gr[t5_bucket(dist), g_ix4].astype(F32).transpose((0, 1, 4, 2, 3))
        logits = jnp.where((dist >= 0)[:, :, None], logits, NEG_INF)
        sh = logits.shape
        p = jax.nn.softmax(logits.reshape(sh[:3] + (-1,)), axis=-1).reshape(sh)
        return jnp.einsum('qgrns,qgnsd->qgrd', p.astype(vg.dtype), vg)

    o = lax.map(item, (b_items, q_items, idx_items, pos_items))
    return o.reshape(b, tq, NSA_HEADS, HEAD_DIM)


def window_core(qg, q_pos, k, v, k_pos, table):
    dist = q_pos[:, None] - k_pos[None, :]
    valid = (dist >= 0) & (dist < WINDOW) & (k_pos >= 0)[None, :]
    logits = jnp.einsum('btgrd,bkgd->btgrk', qg, k, preferred_element_type=F32) * (HEAD_DIM ** -0.5)
    logits = logits + rel_bias_gr(dist, table).transpose((0, 2, 3, 1))
    logits = jnp.where(valid[:, None, None, :], logits, NEG_INF)
    p = jax.nn.softmax(logits, axis=-1)
    return jnp.einsum('btgrk,bkgd->btgrd', p.astype(v.dtype), v)


def window_attention_banded(q, k, v, table):
    b, t = q.shape[:2]
    qb = QBLK if t % QBLK == 0 else t
    nb = t // qb
    qg = q.reshape(b, t, NSA_KV_HEADS, NSA_GROUP, HEAD_DIM)
    pad = ((0, 0), (WINDOW, 0), (0, 0), (0, 0))
    kp = jnp.pad(k, pad)
    vp = jnp.pad(v, pad)

    def blk(n):
        s0 = n * qb
        qq = lax.dynamic_slice_in_dim(qg, s0, qb, axis=1)
        kk = lax.dynamic_slice_in_dim(kp, s0, WINDOW + qb, axis=1)
        vv = lax.dynamic_slice_in_dim(vp, s0, WINDOW + qb, axis=1)
        qpos = s0 + jnp.arange(qb)
        kpos = s0 - WINDOW + jnp.arange(WINDOW + qb)
        return window_core(qq, qpos, kk, vv, kpos, table)

    o = lax.map(blk, jnp.arange(nb))
    return o.transpose((1, 0, 2, 3, 4, 5)).reshape(b, t, NSA_HEADS, HEAD_DIM)


def nsa_combine(q, gates, kc, vc, ks, vs, o_win, q_pos, pe, w1, b1, w2, b2, table):
    kcmp = compress(kc, pe[0], w1[0], b1[0], w2[0], b2[0])
    vcmp = compress(vc, pe[1], w1[1], b1[1], w2[1], b2[1])
    o_cmp, p_cmp = cmp_attention(q, kcmp, vcmp, q_pos, table)
    idx = select_blocks(p_cmp, q_pos, ks.shape[1])
    o_sel = sel_attention(q, idx, q_pos, ks, vs, table)
    o = gates[..., 0:1] * o_cmp + gates[..., 1:2] * o_sel + gates[..., 2:3] * o_win
    return o.astype(q.dtype)


def nsa_prompt(h, w_in, w_out, pe, w1, b1, w2, b2, table):
    b, t, _ = h.shape
    q, kv, gates = nsa_project(h, w_in)
    pos = jnp.arange(t)
    o_win = window_attention_banded(q, kv[:, :, 4], kv[:, :, 5], table)
    o = nsa_combine(q, gates, kv[:, :, 0], kv[:, :, 1], kv[:, :, 2], kv[:, :, 3], o_win, pos, pe, w1, b1, w2, b2, table)
    y = o.reshape(b, t, -1) @ w_out
    keep = min(WINDOW, t)
    return y, kv[:, :, :4], kv[:, t - keep:, 4:]


def nsa_sample(h, past, win_buf, past_len, w_in, w_out, pe, w1, b1, w2, b2, table):
    b, t, _ = h.shape
    q, kv, gates = nsa_project(h, w_in)
    pos = past_len + jnp.arange(t)
    full = jnp.concatenate([past.astype(kv.dtype), kv[:, :, :4]], axis=1)
    wb = win_buf.shape[1]
    win_all = jnp.concatenate([win_buf.astype(kv.dtype), kv[:, :, 4:]], axis=1)
    kpos = past_len - wb + jnp.arange(wb + t)
    qg = q.reshape(b, t, NSA_KV_HEADS, NSA_GROUP, HEAD_DIM)
    o_win = window_core(qg, pos, win_all[:, :, 0], win_all[:, :, 1], kpos, table).reshape(b, t, NSA_HEADS, HEAD_DIM)
    o = nsa_combine(q, gates, full[:, :, 0], full[:, :, 1], full[:, :, 2], full[:, :, 3], o_win, pos, pe, w1, b1, w2, b2, table)
    y = o.reshape(b, t, -1) @ w_out
    keep = min(WINDOW, wb + t)
    return y, kv[:, :, :4], win_all[:, wb + t - keep:]


def clamped_swiglu(hu):
    glu = jnp.minimum(hu[..., ::2], SWIGLU_LIMIT)
    lin = jnp.clip(hu[..., 1::2], -SWIGLU_LIMIT, SWIGLU_LIMIT)
    return glu * jax.nn.sigmoid(SWIGLU_ALPHA * glu) * (lin + 1.0)


def moe(h, w_r, b_r, w_up, b_up, w_down, b_down):
    shp = h.shape
    x = h.reshape(-1, shp[-1])
    n = x.shape[0]
    logits = jnp.dot(x, w_r, preferred_element_type=F32) + b_r.astype(F32)
    top_v, top_e = lax.top_k(logits, TOP_K)
    gate = jax.nn.softmax(top_v, axis=-1)
    n_asg = n * TOP_K
    rb = MOE_ROWS if n_asg >= N_EXPERTS * MOE_ROWS else MOE_ROWS_SMALL
    n_blk = (n_asg + N_EXPERTS * (rb - 1) + rb - 1) // rb
    flat_e = top_e.reshape(n_asg)
    flat_t = jnp.arange(n_asg) // TOP_K
    flat_g = gate.reshape(n_asg)
    order = jnp.argsort(flat_e)
    se = flat_e[order]
    counts = jnp.bincount(flat_e, length=N_EXPERTS)
    pcounts = (counts + rb - 1) // rb * rb
    pend = jnp.cumsum(pcounts)
    pstart = pend - pcounts
    start = jnp.cumsum(counts) - counts
    dest = pstart[se] + jnp.arange(n_asg) - start[se]
    rows = n_blk * rb
    row_tok = jnp.full((rows,), n, jnp.int32).at[dest].set(flat_t[order].astype(jnp.int32))
    row_gate = jnp.zeros((rows,), F32).at[dest].set(flat_g[order])
    blk_e = jnp.minimum(jnp.searchsorted(pend, jnp.arange(n_blk) * rb, side='right'), N_EXPERTS - 1)
    xp = jnp.concatenate([x, jnp.zeros((1, x.shape[1]), x.dtype)], axis=0)
    xs = xp[row_tok].reshape(n_blk, rb, x.shape[1])

    def expert_block(args):
        xb, e = args
        hu = xb @ w_up[e] + b_up[e]
        return clamped_swiglu(hu) @ w_down[e] + b_down[e]

    yb = lax.map(expert_block, (xs, blk_e)).reshape(rows, x.shape[1])
    out = jnp.zeros((n + 1, x.shape[1]), F32).at[row_tok].add(yb.astype(F32) * row_gate[:, None])[:n]
    return out.astype(h.dtype).reshape(shp)


def setup_inputs(seed: int = 0) -> dict:
    key = jax.random.key(seed)
    ks = jax.random.split(key, 32)

    def nrm(k, shape, scale):
        return jax.random.normal(k, shape, F32) * scale

    n_pages = PAST_LEN // PAGE_SIZE
    n_used = DEC_BATCH * n_pages
    n_pool = n_used + max(1, n_used // 4)
    page_table = jax.random.permutation(ks[7], n_pool)[:n_used].reshape(DEC_BATCH, n_pages).astype(jnp.int32)
    win_buf = min(WINDOW, PAST_LEN)
    d = D_MODEL
    return {
        'x_prompt': nrm(ks[0], (BATCH, SEQ, d), 1.0),
        'x_sample': nrm(ks[1], (DEC_BATCH, DEC_SEQ, d), 1.0),
        'c_prompt': nrm(ks[2], (BATCH, d), 1.0),
        'c_sample': nrm(ks[3], (DEC_BATCH, d), 1.0),
        'state_ret': nrm(ks[4], (N_RET_LAYERS, DEC_BATCH, RET_HEADS, RET_DK, RET_DV), 0.5),
        'cache_nsa': nrm(ks[5], (n_pool, N_NSA_LAYERS, PAGE_SIZE, 4, NSA_KV_HEADS, HEAD_DIM), 1.0),
        'state_nsa_win': nrm(ks[6], (N_NSA_LAYERS, DEC_BATCH, win_buf, 2, NSA_KV_HEADS, HEAD_DIM), 1.0),
        'page_table': page_table,
        'rel_bias': nrm(ks[8], (REL_BUCKETS, NSA_HEADS), 0.5),
        'ada_w': nrm(ks[9], (DEPTH, 2, d, 3 * d), 0.5 * d ** -0.5),
        'ada_b': nrm(ks[10], (DEPTH, 2, 3 * d), 0.02),
        'ln_g': 1.0 + nrm(ks[11], (DEPTH, 2, d), 0.02),
        'ln_b': nrm(ks[12], (DEPTH, 2, d), 0.02),
        'ret_w_in': nrm(ks[13], (N_RET_LAYERS, d, RET_IN), d ** -0.5),
        'ret_gn_w': 1.0 + nrm(ks[14], (N_RET_LAYERS, RET_HEADS * RET_DV), 0.02),
        'ret_w_out': nrm(ks[15], (N_RET_LAYERS, RET_HEADS * RET_DV, d), DN_BETA * (RET_HEADS * RET_DV) ** -0.5),
        'nsa_w_in': nrm(ks[16], (N_NSA_LAYERS, d, NSA_IN), d ** -0.5),
        'nsa_cmp_pe': nrm(ks[17], (N_NSA_LAYERS, 2, CMP_LEN, HEAD_DIM), 0.1),
        'nsa_cmp_w1': nrm(ks[18], (N_NSA_LAYERS, 2, CMP_LEN * HEAD_DIM, CMP_HIDDEN), (CMP_LEN * HEAD_DIM) ** -0.5),
        'nsa_cmp_b1': nrm(ks[19], (N_NSA_LAYERS, 2, CMP_HIDDEN), 0.02),
        'nsa_cmp_w2': nrm(ks[20], (N_NSA_LAYERS, 2, CMP_HIDDEN, HEAD_DIM), CMP_HIDDEN ** -0.5),
        'nsa_cmp_b2': nrm(ks[21], (N_NSA_LAYERS, 2, HEAD_DIM), 0.02),
        'nsa_w_out': nrm(ks[22], (N_NSA_LAYERS, NSA_HEADS * HEAD_DIM, d), DN_BETA * (NSA_HEADS * HEAD_DIM) ** -0.5),
        'moe_w_router': nrm(ks[23], (DEPTH, d, N_EXPERTS), d ** -0.5),
        'moe_b_router': nrm(ks[24], (DEPTH, N_EXPERTS), 0.01),
        'moe_w_up': nrm(ks[25], (DEPTH, N_EXPERTS, d, 2 * D_EXPERT), d ** -0.5),
        'moe_b_up': nrm(ks[26], (DEPTH, N_EXPERTS, 2 * D_EXPERT), 0.02),
        'moe_w_down': nrm(ks[27], (DEPTH, N_EXPERTS, D_EXPERT, d), DN_BETA * D_EXPERT ** -0.5),
        'moe_b_down': nrm(ks[28], (DEPTH, N_EXPERTS, d), 0.02),
    }


def reference(x_prompt, x_sample, c_prompt, c_sample, state_ret, cache_nsa, state_nsa_win, page_table, rel_bias,
              ada_w, ada_b, ln_g, ln_b, ret_w_in, ret_gn_w, ret_w_out, nsa_w_in, nsa_cmp_pe, nsa_cmp_w1, nsa_cmp_b1,
              nsa_cmp_w2, nsa_cmp_b2, nsa_w_out, moe_w_router, moe_b_router, moe_w_up, moe_b_up, moe_w_down, moe_b_down):
    b, t, _ = x_prompt.shape
    db, ds, _ = x_sample.shape
    n_pages = page_table.shape[1]
    past_len = n_pages * PAGE_SIZE
    pos_p = jnp.arange(t)
    pos_s = past_len + jnp.arange(ds)
    xp, xs = x_prompt, x_sample
    ret_sp, ret_ss, rows_p, rows_s, win_p, win_s = [], [], [], [], [], []
    for li in range(DEPTH):
        shp, scp, gtp = ada_mod(c_prompt, ada_w[li, 0], ada_b[li, 0])
        shs, scs, gts = ada_mod(c_sample, ada_w[li, 0], ada_b[li, 0])
        hp = xp * (1.0 + scp) + shp
        hs = xs * (1.0 + scs) + shs
        if li % N_MIXERS == 0:
            ri = li // N_MIXERS
            s_zero = jnp.zeros((b, RET_HEADS, RET_DK, RET_DV), x_prompt.dtype)
            yp, sp = retention_mixer(hp, pos_p, s_zero, ret_w_in[ri], ret_gn_w[ri], ret_w_out[ri])
            ys, ss = retention_mixer(hs, pos_s, state_ret[ri], ret_w_in[ri], ret_gn_w[ri], ret_w_out[ri])
            ret_sp.append(sp)
            ret_ss.append(ss)
        else:
            ni = li // N_MIXERS
            cmp_args = (nsa_cmp_pe[ni], nsa_cmp_w1[ni], nsa_cmp_b1[ni], nsa_cmp_w2[ni], nsa_cmp_b2[ni], rel_bias)
            yp, rp, wp = nsa_prompt(hp, nsa_w_in[ni], nsa_w_out[ni], *cmp_args)
            past = cache_nsa[page_table, ni].reshape(db, past_len, 4, NSA_KV_HEADS, HEAD_DIM)
            ys, rs, wsb = nsa_sample(hs, past, state_nsa_win[ni], past_len, nsa_w_in[ni], nsa_w_out[ni], *cmp_args)
            rows_p.append(rp)
            rows_s.append(rs)
            win_p.append(wp)
            win_s.append(wsb)
        xp = layer_norm(DN_ALPHA * xp + gtp * yp, ln_g[li, 0], ln_b[li, 0])
        xs = layer_norm(DN_ALPHA * xs + gts * ys, ln_g[li, 0], ln_b[li, 0])
        shp, scp, gtp = ada_mod(c_prompt, ada_w[li, 1], ada_b[li, 1])
        shs, scs, gts = ada_mod(c_sample, ada_w[li, 1], ada_b[li, 1])
        hp = xp * (1.0 + scp) + shp
        hs = xs * (1.0 + scs) + shs
        moe_args = (moe_w_router[li], moe_b_router[li], moe_w_up[li], moe_b_up[li], moe_w_down[li], moe_b_down[li])
        yp = moe(hp, *moe_args)
        ys = moe(hs, *moe_args)
        xp = layer_norm(DN_ALPHA * xp + gtp * yp, ln_g[li, 1], ln_b[li, 1])
        xs = layer_norm(DN_ALPHA * xs + gts * ys, ln_g[li, 1], ln_b[li, 1])
    return (xp, xs, jnp.stack(ret_sp), jnp.stack(ret_ss), jnp.stack(rows_p, axis=1), jnp.stack(rows_s, axis=1), jnp.stack(win_p), jnp.stack(win_s))
```

```python
import functools
import math

import jax
import jax.numpy as jnp
from jax import lax
from jax.experimental import pallas as pl
from jax.experimental.pallas import tpu as pltpu

D_MODEL = 4096
BATCH = 4
SEQ = 2048
DEPTH = 2
DEC_BATCH = 8
DEC_SEQ = 1
PAST_LEN = 16384
PAGE_SIZE = 128

N_MIXERS = 2
N_RET_LAYERS = (DEPTH + 1) // 2
N_NSA_LAYERS = DEPTH // 2

DN_ALPHA = (2.0 * DEPTH) ** 0.25
LN_EPS = 1e-5
NEG_INF = -1e30
NEG_SCORE = -1e9

RET_HEADS = 16
RET_DK = D_MODEL // RET_HEADS
RET_DV = 2 * RET_DK
RET_CHUNK = 128
ROPE_BASE = 10000.0

NSA_HEADS = 32
HEAD_DIM = D_MODEL // NSA_HEADS
NSA_KV_HEADS = 4
NSA_GROUP = NSA_HEADS // NSA_KV_HEADS
NSA_KV = NSA_KV_HEADS * HEAD_DIM
CMP_LEN = 32
CMP_STRIDE = 16
CMP_HIDDEN = 2 * HEAD_DIM
SEL_BLK = 64
SEL_TOPK = 16
WINDOW = 512
QBLK = 128
FORCE_BONUS = 1e6

REL_BUCKETS = 32
REL_MAX_DIST = 128

N_EXPERTS = 32
TOP_K = 4
D_EXPERT = D_MODEL // 2
SWIGLU_ALPHA = 1.702
SWIGLU_LIMIT = 7.0

RET_IN = 2 * RET_HEADS * RET_DK + 2 * RET_HEADS * RET_DV
NSA_IN = NSA_HEADS * HEAD_DIM + 6 * NSA_KV + 3 * NSA_HEADS

F32 = jnp.float32
BF16 = jnp.bfloat16
HI = lax.Precision.HIGHEST

LANE = 128
SUBLANE = 8
VMEM_LIMIT_BYTES = 56 * 1024 * 1024

ROW_TILE = 256
MOE_TM = 256
MOE_GATHER_ROWS = 256
MOE_COMBINE_ROWS = 128
CMP_PAGES = 8


def _cp(*sem, vmem=VMEM_LIMIT_BYTES):
    return pltpu.CompilerParams(dimension_semantics=sem, vmem_limit_bytes=vmem)


def _dot(a, b):
    return jnp.dot(a, b, preferred_element_type=F32)


def _dot_nt(a, b):
    return lax.dot_general(a, b, (((1,), (1,)), ((), ())), preferred_element_type=F32)


def _dot_tn(a, b):
    return lax.dot_general(a, b, (((0,), (0,)), ((), ())), preferred_element_type=F32)


def _pick(n, *cands):
    for c in cands:
        if n % c == 0:
            return c
    return n


def _ada_kernel(c_ref, w_ref, b_ref, o_ref):
    c = c_ref[...]
    s = (c / (1.0 + jnp.exp(-c))).astype(BF16)
    o_ref[...] = _dot(s, w_ref[...].astype(BF16)) + b_ref[...]


def ada_all(c_all, ada_w, ada_b):
    rc, d = c_all.shape
    nl = ada_w.shape[0] * ada_w.shape[1]
    w = ada_w.reshape(nl, d, 3 * d)
    b = ada_b.reshape(nl, 1, 3 * d)
    tn = _pick(3 * d, 512, 256, 128)
    return pl.pallas_call(
        _ada_kernel,
        out_shape=jax.ShapeDtypeStruct((nl, rc, 3 * d), F32),
        grid=(nl, 3 * d // tn),
        in_specs=[pl.BlockSpec((rc, d), lambda l, j: (0, 0)),
                  pl.BlockSpec((None, d, tn), lambda l, j: (l, 0, j)),
                  pl.BlockSpec((None, 1, tn), lambda l, j: (l, 0, j))],
        out_specs=pl.BlockSpec((None, rc, tn), lambda l, j: (l, 0, j)),
        compiler_params=_cp("arbitrary", "arbitrary"),
        name="ada_mod",
    )(c_all, w, b)


def _slabs(mod, nb, ndb):
    d = mod.shape[-1]
    p = jnp.broadcast_to(mod[:nb, None, :], (nb, SUBLANE, d))
    s = jnp.zeros((1, SUBLANE, d), F32).at[0, :ndb].set(mod[nb:nb + ndb])
    return jnp.concatenate([p, s], axis=0)


def _mod_kernel(x_ref, sc_ref, sh_ref, o_ref):
    tm, d = x_ref.shape
    x = x_ref[...].reshape(tm // SUBLANE, SUBLANE, d)
    h = x * (1.0 + sc_ref[...][None]) + sh_ref[...][None]
    o_ref[...] = h.reshape(tm, d).astype(o_ref.dtype)


def _mod_router_kernel(x_ref, sc_ref, sh_ref, wr_ref, br_ref, o_ref, lg_ref):
    tm, d = x_ref.shape
    x = x_ref[...].reshape(tm // SUBLANE, SUBLANE, d)
    h = (x * (1.0 + sc_ref[...][None]) + sh_ref[...][None]).reshape(tm, d)
    o_ref[...] = h
    lg_ref[...] = jnp.dot(h, wr_ref[...], precision=HI, preferred_element_type=F32) + br_ref[...]


def _slab_spec(d, tiles_per_seq, nb):
    return pl.BlockSpec((None, SUBLANE, d), lambda i: (jnp.minimum(i // tiles_per_seq, nb), 0, 0))


def modulate(x_all, sc, sh, tiles_per_seq, nb):
    n, d = x_all.shape
    return pl.pallas_call(
        _mod_kernel,
        out_shape=jax.ShapeDtypeStruct((n, d), BF16),
        grid=(n // ROW_TILE,),
        in_specs=[pl.BlockSpec((ROW_TILE, d), lambda i: (i, 0)),
                  _slab_spec(d, tiles_per_seq, nb), _slab_spec(d, tiles_per_seq, nb)],
        out_specs=pl.BlockSpec((ROW_TILE, d), lambda i: (i, 0)),
        compiler_params=_cp("arbitrary"),
        name="modulate",
    )(x_all, sc, sh)


def modulate_router(x_all, sc, sh, w_r, b_r, li, tiles_per_seq, nb):
    n, d = x_all.shape
    ne = w_r.shape[-1]
    b_r3 = b_r.reshape(b_r.shape[0], 1, ne)
    return pl.pallas_call(
        _mod_router_kernel,
        out_shape=(jax.ShapeDtypeStruct((n, d), F32), jax.ShapeDtypeStruct((n, ne), F32)),
        grid=(n // ROW_TILE,),
        in_specs=[pl.BlockSpec((ROW_TILE, d), lambda i: (i, 0)),
                  _slab_spec(d, tiles_per_seq, nb), _slab_spec(d, tiles_per_seq, nb),
                  pl.BlockSpec((None, d, ne), lambda i: (li, 0, 0)),
                  pl.BlockSpec((None, 1, ne), lambda i: (li, 0, 0))],
        out_specs=(pl.BlockSpec((ROW_TILE, d), lambda i: (i, 0)),
                   pl.BlockSpec((ROW_TILE, ne), lambda i: (i, 0))),
        compiler_params=_cp("arbitrary"),
        name="modulate_router",
    )(x_all, sc, sh, w_r, b_r3)


def _resid_ln_kernel(x_ref, y_ref, gt_ref, g_ref, b_ref, o_ref):
    tm, d = x_ref.shape
    x = x_ref[...].reshape(tm // SUBLANE, SUBLANE, d)
    y = y_ref[...].reshape(tm // SUBLANE, SUBLANE, d)
    v = (DN_ALPHA * x + gt_ref[...][None] * y).reshape(tm, d)
    mu = jnp.mean(v, axis=-1, keepdims=True)
    c = v - mu
    var = jnp.mean(c * c, axis=-1, keepdims=True)
    o_ref[...] = c * lax.rsqrt(var + LN_EPS) * g_ref[...] + b_ref[...]


def resid_ln(x_all, y_all, gt, ln_g, ln_b, li, sub, tiles_per_seq, nb):
    n, d = x_all.shape
    g3 = ln_g.reshape(ln_g.shape[0] * ln_g.shape[1], 1, d)
    b3 = ln_b.reshape(ln_b.shape[0] * ln_b.shape[1], 1, d)
    idx = li * 2 + sub
    return pl.pallas_call(
        _resid_ln_kernel,
        out_shape=jax.ShapeDtypeStruct((n, d), F32),
        grid=(n // ROW_TILE,),
        in_specs=[pl.BlockSpec((ROW_TILE, d), lambda i: (i, 0)),
                  pl.BlockSpec((ROW_TILE, d), lambda i: (i, 0)),
                  _slab_spec(d, tiles_per_seq, nb),
                  pl.BlockSpec((None, 1, d), lambda i: (idx, 0, 0)),
                  pl.BlockSpec((None, 1, d), lambda i: (idx, 0, 0))],
        out_specs=pl.BlockSpec((ROW_TILE, d), lambda i: (i, 0)),
        compiler_params=_cp("arbitrary"),
        name="resid_ln",
    )(x_all, y_all, gt, g3, b3)


def _mm_kernel(x_ref, w_ref, o_ref):
    @pl.when(pl.program_id(2) == 0)
    def _():
        o_ref[...] = jnp.zeros_like(o_ref)

    o_ref[...] += _dot(x_ref[...], w_ref[...].astype(BF16))


def _mm_into_kernel(prev_ref, x_ref, w_ref, o_ref):
    del prev_ref
    _mm_kernel(x_ref, w_ref, o_ref)


def matmul(x, w, wsel, *, row0, rows, tm, tn, tk, ncols=None, out_rows=None, out_row0=0, into=None):
    kdim = x.shape[1]
    n = w.shape[-1] if ncols is None else ncols
    assert n % tn == 0
    nj = n // tn
    assert rows % tm == 0 and row0 % tm == 0 and out_row0 % tm == 0 and kdim % tk == 0
    rb0, ob0 = row0 // tm, out_row0 // tm
    out_rows = rows if out_rows is None else out_rows
    nlead = len(wsel)
    in_specs = [pl.BlockSpec((tm, tk), lambda i, j, k: (rb0 + i, k)),
                pl.BlockSpec((None,) * nlead + (tk, tn), lambda i, j, k: tuple(wsel) + (k, j))]
    args = [x, w]
    kern, aliases = _mm_kernel, {}
    if into is not None:
        assert into.shape == (out_rows, nj * tn)
        in_specs = [pl.BlockSpec(memory_space=pl.ANY)] + in_specs
        args = [into] + args
        kern, aliases = _mm_into_kernel, {0: 0}
    return pl.pallas_call(
        kern,
        out_shape=jax.ShapeDtypeStruct((out_rows, nj * tn), F32),
        grid=(rows // tm, nj, kdim // tk),
        in_specs=in_specs,
        out_specs=pl.BlockSpec((tm, tn), lambda i, j, k: (ob0 + i, j)),
        input_output_aliases=aliases,
        compiler_params=_cp("arbitrary", "arbitrary", "arbitrary"),
        name="matmul",
    )(*args)


def dense_prompt_and_sample(x_p, x_s, w, wsel, n_prompt, *, ncols=None, into_rows=None):
    kdim = x_p.shape[1]
    n = w.shape[-1] if ncols is None else ncols
    tm = _pick(n_prompt, 2048, 1024, 512, 256)
    tn = _pick(n, 1024, 512, 256, 128)
    tk = _pick(kdim, 512, 256)
    s_row0 = x_s.shape[0] - ROW_TILE
    if into_rows is None:
        z_p = matmul(x_p, w, wsel, row0=0, rows=n_prompt, tm=tm, tn=tn, tk=tk, ncols=ncols)
        z_s = matmul(x_s, w, wsel, row0=s_row0, rows=ROW_TILE, tm=ROW_TILE, tn=tn, tk=tk, ncols=ncols)
        return z_p, z_s
    y = jnp.zeros((into_rows, n), F32)
    y = matmul(x_p, w, wsel, row0=0, rows=n_prompt, tm=tm, tn=tn, tk=tk, ncols=ncols, out_rows=into_rows, into=y)
    return matmul(x_s, w, wsel, row0=s_row0, rows=ROW_TILE, tm=ROW_TILE, tn=tn, tk=tk, ncols=ncols,
                  out_rows=into_rows, out_row0=n_prompt, into=y)


def _ret_kernel(*refs, has_s0, nchunks):
    (q_ref, k_ref, v_ref, g_ref, cos_ref, sin_ref, dec_ref, qd_ref, kd_ref, cd_ref, gn_ref), rest = refs[:11], refs[11:]
    if has_s0:
        s0_ref, a_ref, so_ref, s_scr = rest
    else:
        a_ref, so_ref, s_scr = rest
    c = pl.program_id(2)

    @pl.when(c == 0)
    def _():
        s_scr[...] = s0_ref[...] if has_s0 else jnp.zeros_like(s_scr)

    half = RET_DK // 2
    cos = cos_ref[...]
    sin = sin_ref[...]

    def rot(x):
        x1, x2 = x[:, :half], x[:, half:]
        return jnp.concatenate([x1 * cos - x2 * sin, x1 * sin + x2 * cos], axis=-1)

    q = rot(q_ref[...])
    k = rot(k_ref[...]) * (RET_DK ** -0.5)
    qb = q.astype(BF16)
    vb = v_ref[...].astype(BF16)
    s = s_scr[...]
    att = _dot_nt(qb, k.astype(BF16)) * dec_ref[...]
    o = _dot(att.astype(BF16), vb) + _dot(qb, s.astype(BF16)) * qd_ref[...]
    s_new = s * cd_ref[...] + _dot_tn((k * kd_ref[...]).astype(BF16), vb)
    s_scr[...] = s_new
    mu = jnp.mean(o, axis=-1, keepdims=True)
    oc = o - mu
    var = jnp.mean(oc * oc, axis=-1, keepdims=True)
    on = oc * lax.rsqrt(var + LN_EPS) * gn_ref[...]
    g = g_ref[...]
    a_ref[...] = ((g / (1.0 + jnp.exp(-g))) * on).astype(BF16)

    @pl.when(c == nchunks - 1)
    def _():
        so_ref[...] = s_new


def retention(z3, pos, s0, gn_w, ri, true_chunk):
    b, tpad, _ = z3.shape
    cpad = min(RET_CHUNK, tpad)
    nchunks = tpad // cpad
    nk, nv = RET_HEADS * RET_DK, RET_HEADS * RET_DV
    half = RET_DK // 2
    inv = 1.0 / (ROPE_BASE ** (jnp.arange(half, dtype=F32) / half))
    ang = pos.astype(F32)[:, None] * inv[None, :]
    cos, sin = jnp.cos(ang), jnp.sin(ang)
    lg = jnp.log1p(-jnp.exp2(-5.0 - jnp.arange(RET_HEADS, dtype=F32)))
    i = jnp.arange(cpad, dtype=F32)
    diff = i[:, None] - i[None, :]
    decay = jnp.where(diff >= 0, jnp.exp(jnp.maximum(diff, 0.0)[None] * lg[:, None, None]), 0.0)
    q_dec = jnp.exp((i + 1.0)[None, :] * lg[:, None])[:, :, None]
    k_dec = jnp.exp((true_chunk - 1.0 - i)[None, :] * lg[:, None])[:, :, None]
    c_dec = jnp.exp(true_chunk * lg)[:, None, None]
    kq, kv = RET_DK, RET_DV
    in_specs = [
        pl.BlockSpec((None, cpad, kq), lambda bi, h, c: (bi, c, h)),
        pl.BlockSpec((None, cpad, kq), lambda bi, h, c: (bi, c, nk // kq + h)),
        pl.BlockSpec((None, cpad, kv), lambda bi, h, c: (bi, c, 2 * nk // kv + h)),
        pl.BlockSpec((None, cpad, kv), lambda bi, h, c: (bi, c, (2 * nk + nv) // kv + h)),
        pl.BlockSpec((cpad, half), lambda bi, h, c: (c, 0)),
        pl.BlockSpec((cpad, half), lambda bi, h, c: (c, 0)),
        pl.BlockSpec((None, cpad, cpad), lambda bi, h, c: (h, 0, 0)),
        pl.BlockSpec((None, cpad, 1), lambda bi, h, c: (h, 0, 0)),
        pl.BlockSpec((None, cpad, 1), lambda bi, h, c: (h, 0, 0)),
        pl.BlockSpec((None, 1, 1), lambda bi, h, c: (h, 0, 0)),
        pl.BlockSpec((None, 1, kv), lambda bi, h, c: (ri, 0, h)),
    ]
    args = [z3, z3, z3, z3, cos, sin, decay, q_dec, k_dec, c_dec, gn_w.reshape(gn_w.shape[0], 1, nv)]
    if s0 is not None:
        in_specs.append(pl.BlockSpec((None, None, None, kq, kv), lambda bi, h, c: (ri, bi, h, 0, 0)))
        args.append(s0)
    return pl.pallas_call(
        functools.partial(_ret_kernel, has_s0=s0 is not None, nchunks=nchunks),
        out_shape=(jax.ShapeDtypeStruct((b, tpad, nv), BF16),
                   jax.ShapeDtypeStruct((b, RET_HEADS, kq, kv), F32)),
        grid=(b, RET_HEADS, nchunks),
        in_specs=in_specs,
        out_specs=(pl.BlockSpec((None, cpad, kv), lambda bi, h, c: (bi, c, h)),
                   pl.BlockSpec((None, None, kq, kv), lambda bi, h, c: (bi, h, 0, 0))),
        scratch_shapes=[pltpu.VMEM((kq, kv), F32)],
        compiler_params=_cp("arbitrary", "arbitrary", "arbitrary"),
        name="retention",
    )(*args)


def _moe_gather_kernel(tok_ref, nu_ref, h_hbm, o_ref, buf, sem, *, rb):
    i = pl.program_id(0)

    def row_copy(r):
        return pltpu.make_async_copy(h_hbm.at[pl.ds(tok_ref[i * rb + r], 1)], buf.at[pl.ds(r, 1)], sem)

    @pl.when(i * rb < nu_ref[0] * MOE_TM)
    def _():
        def start(r, carry):
            row_copy(r).start()
            return carry

        def wait(r, carry):
            row_copy(r).wait()
            return carry

        lax.fori_loop(0, rb, start, 0)
        lax.fori_loop(0, rb, wait, 0)
        o_ref[...] = buf[...].astype(BF16)

    @pl.when(i * rb >= nu_ref[0] * MOE_TM)
    def _():
        o_ref[...] = jnp.zeros_like(o_ref)


def moe_gather(h_all, row_tok, n_used):
    rows = row_tok.shape[0]
    d = h_all.shape[1]
    rb = MOE_GATHER_ROWS
    return pl.pallas_call(
        functools.partial(_moe_gather_kernel, rb=rb),
        out_shape=jax.ShapeDtypeStruct((rows, d), BF16),
        grid_spec=pltpu.PrefetchScalarGridSpec(
            num_scalar_prefetch=2,
            grid=(rows // rb,),
            in_specs=[pl.BlockSpec(memory_space=pl.ANY)],
            out_specs=pl.BlockSpec((rb, d), lambda i, tok, nu: (i, 0)),
            scratch_shapes=[pltpu.VMEM((rb, d), F32), pltpu.SemaphoreType.DMA(())]),
        compiler_params=_cp("arbitrary"),
        name="moe_gather",
    )(row_tok, n_used, h_all)


def _moe_up_kernel(be_ref, first_ref, nu_ref, x_ref, w_ref, b_ref, o_ref, wbf, *, tn):
    blk = pl.program_id(1)

    @pl.when(blk < nu_ref[0])
    def _():
        @pl.when(first_ref[blk] == 1)
        def _():
            wbf[...] = w_ref[...].astype(BF16)

        hu = _dot(x_ref[...], wbf[...]) + b_ref[...]
        lin = pltpu.roll(hu, tn - 1, 1)
        glu = jnp.minimum(hu, SWIGLU_LIMIT)
        linc = jnp.clip(lin, -SWIGLU_LIMIT, SWIGLU_LIMIT)
        act = ((glu / (1.0 + jnp.exp(-SWIGLU_ALPHA * glu))) * (linc + 1.0)).astype(BF16)
        pick_even = (lax.broadcasted_iota(jnp.int32, (2 * LANE, LANE), 0)
                     == 2 * lax.broadcasted_iota(jnp.int32, (2 * LANE, LANE), 1)).astype(BF16)
        for c in range(tn // (2 * LANE)):
            o_ref[:, c * LANE:(c + 1) * LANE] = _dot(act[:, c * 2 * LANE:(c + 1) * 2 * LANE], pick_even).astype(BF16)

    @pl.when(blk >= nu_ref[0])
    def _():
        o_ref[...] = jnp.zeros_like(o_ref)


def _moe_down_kernel(be_ref, first_ref, nu_ref, x_ref, w_ref, b_ref, o_ref, wbf):
    blk = pl.program_id(1)

    @pl.when(blk < nu_ref[0])
    def _():
        @pl.when(first_ref[blk] == 1)
        def _():
            wbf[...] = w_ref[...].astype(BF16)

        o_ref[...] = _dot(x_ref[...], wbf[...]) + b_ref[...]

    @pl.when(blk >= nu_ref[0])
    def _():
        o_ref[...] = jnp.zeros_like(o_ref)


def _moe_grouped(kern, x, w, bias, li, blk_e, first, n_used, tn, out_cols, out_tn, out_dtype, name):
    rows, kdim = x.shape
    n = w.shape[-1]
    n_blk = rows // MOE_TM
    b4 = bias.reshape(bias.shape[0], bias.shape[1], 1, n)

    def bc(blk, nu):
        return jnp.minimum(blk, nu[0] - 1)

    return pl.pallas_call(
        kern,
        out_shape=jax.ShapeDtypeStruct((rows, out_cols), out_dtype),
        grid_spec=pltpu.PrefetchScalarGridSpec(
            num_scalar_prefetch=3,
            grid=(n // tn, n_blk),
            in_specs=[pl.BlockSpec((MOE_TM, kdim), lambda j, blk, be, fi, nu: (bc(blk, nu), 0)),
                      pl.BlockSpec((None, None, kdim, tn), lambda j, blk, be, fi, nu: (li, be[bc(blk, nu)], 0, j)),
                      pl.BlockSpec((None, None, 1, tn), lambda j, blk, be, fi, nu: (li, be[bc(blk, nu)], 0, j))],
            out_specs=pl.BlockSpec((MOE_TM, out_tn), lambda j, blk, be, fi, nu: (blk, j)),
            scratch_shapes=[pltpu.VMEM((kdim, tn), BF16)]),
        compiler_params=_cp("arbitrary", "arbitrary"),
        name=name,
    )(blk_e, first, n_used, x, w, b4)


def _moe_combine_kernel(pos_ref, tv_ref, yb_hbm, o_ref, buf, sem, *, tt):
    i = pl.program_id(0)

    def row_copy(r, k):
        return pltpu.make_async_copy(yb_hbm.at[pl.ds(pos_ref[(i * tt + r) * TOP_K + k], 1)],
                                     buf.at[k, pl.ds(r, 1)], sem)

    def start(r, carry):
        for k in range(TOP_K):
            row_copy(r, k).start()
        return carry

    def wait(r, carry):
        for k in range(TOP_K):
            row_copy(r, k).wait()
        return carry

    lax.fori_loop(0, tt, start, 0)
    lax.fori_loop(0, tt, wait, 0)
    tv = tv_ref[...]
    e = jnp.exp(tv - jnp.max(tv, axis=-1, keepdims=True))
    gate = e / jnp.sum(e, axis=-1, keepdims=True)
    acc = gate[:, 0:1] * buf[0]
    for k in range(1, TOP_K):
        acc = acc + gate[:, k:k + 1] * buf[k]
    o_ref[...] = acc


def moe_combine(yb, pos_flat, top_v):
    n = top_v.shape[0]
    d = yb.shape[1]
    tt = MOE_COMBINE_ROWS
    return pl.pallas_call(
        functools.partial(_moe_combine_kernel, tt=tt),
        out_shape=jax.ShapeDtypeStruct((n, d), F32),
        grid_spec=pltpu.PrefetchScalarGridSpec(
            num_scalar_prefetch=1,
            grid=(n // tt,),
            in_specs=[pl.BlockSpec((tt, TOP_K), lambda i, pos: (i, 0)),
                      pl.BlockSpec(memory_space=pl.ANY)],
            out_specs=pl.BlockSpec((tt, d), lambda i, pos: (i, 0)),
            scratch_shapes=[pltpu.VMEM((TOP_K, tt, d), F32), pltpu.SemaphoreType.DMA(())]),
        compiler_params=_cp("arbitrary"),
        name="moe_combine",
    )(pos_flat, top_v, yb)


def moe(h_all, logits, ntok, li, w_up, b_up, w_down, b_down):
    npad, d = h_all.shape
    tm = MOE_TM
    top_v, top_e = lax.top_k(logits[:ntok], TOP_K)
    n_asg = ntok * TOP_K
    n_blk = (n_asg + N_EXPERTS * (tm - 1) + tm - 1) // tm
    rows = n_blk * tm
    flat_e = top_e.reshape(n_asg)
    order = jnp.argsort(flat_e)
    se = flat_e[order]
    counts = jnp.bincount(flat_e, length=N_EXPERTS)
    pcounts = (counts + tm - 1) // tm * tm
    pend = jnp.cumsum(pcounts)
    pstart = pend - pcounts
    start = jnp.cumsum(counts) - counts
    dest = (pstart[se] + jnp.arange(n_asg) - start[se]).astype(jnp.int32)
    row_tok = jnp.zeros((rows,), jnp.int32).at[dest].set((order // TOP_K).astype(jnp.int32))
    pos = jnp.zeros((npad * TOP_K,), jnp.int32).at[order].set(dest)
    blk_e = jnp.minimum(jnp.searchsorted(pend, jnp.arange(n_blk) * tm, side='right'), N_EXPERTS - 1).astype(jnp.int32)
    first = jnp.concatenate([jnp.ones((1,), jnp.int32), (blk_e[1:] != blk_e[:-1]).astype(jnp.int32)])
    n_used = (pend[-1] // tm).astype(jnp.int32).reshape(1)
    tv_pad = jnp.zeros((npad, TOP_K), F32).at[:ntok].set(top_v)

    xs = moe_gather(h_all, row_tok, n_used)
    tn_up = _pick(2 * D_EXPERT, 512, 256)
    act = _moe_grouped(functools.partial(_moe_up_kernel, tn=tn_up), xs, w_up, b_up, li, blk_e, first, n_used,
                       tn_up, D_EXPERT, tn_up // 2, BF16, "moe_up")
    tn_dn = _pick(d, 1024, 512, 256)
    yb = _moe_grouped(_moe_down_kernel, act, w_down, b_down, li, blk_e, first, n_used,
                      tn_dn, d, tn_dn, F32, "moe_down")
    return moe_combine(yb, pos, tv_pad)


def _t5_bucket(dist):
    n = jnp.maximum(dist, 0)
    exact = REL_BUCKETS // 2
    nf = jnp.maximum(n, 1).astype(F32)
    large = exact + (jnp.log(nf / exact) / math.log(REL_MAX_DIST / exact) * (REL_BUCKETS - exact)).astype(jnp.int32)
    large = jnp.minimum(large, REL_BUCKETS - 1)
    return jnp.where(n < exact, n, large)


def _bias_of_dist(dist, table):
    return jnp.moveaxis(table[_t5_bucket(dist)].astype(F32), -1, 0)


def _cmp_geometry(tk):
    ncb = (tk - CMP_LEN) // CMP_STRIDE + 1
    nch = -(-(ncb + CMP_LEN // CMP_STRIDE - 1) // (CMP_PAGES * SUBLANE)) * (CMP_PAGES * SUBLANE)
    nsb = -(-tk // SEL_BLK)
    nsbp = -(-nsb // LANE) * LANE
    return ncb, nch, nsb, nsbp


def _cmpa_kernel(*refs, npage):
    page_refs, w_ref, o_ref, pg = refs[1:npage + 1], refs[npage + 1], refs[npage + 2], refs[npage + 3]
    cpp = PAGE_SIZE // CMP_STRIDE
    m_rows = NSA_KV_HEADS * npage * cpp
    blk = 2 * HEAD_DIM
    per = CMP_LEN // CMP_STRIDE
    for p in range(npage):
        for c in range(2 * NSA_KV_HEADS):
            pg[p, c] = page_refs[p][:, c * HEAD_DIM:(c + 1) * HEAD_DIM]
    for slot in range(2):
        accs = [jnp.zeros((m_rows, CMP_HIDDEN), F32) for _ in range(per)]
        for s2 in range(CMP_STRIDE // 2):
            parts = []
            for g in range(NSA_KV_HEADS):
                for p in range(npage):
                    c = slot * NSA_KV_HEADS + g
                    x0 = pg[p, c, pl.ds(2 * s2, cpp, stride=CMP_STRIDE), :]
                    x1 = pg[p, c, pl.ds(2 * s2 + 1, cpp, stride=CMP_STRIDE), :]
                    parts.append(jnp.concatenate([x0, x1], axis=1))
            xm = jnp.concatenate(parts, axis=0).astype(BF16)
            for m in range(per):
                w = w_ref[slot, pl.ds(m * CMP_STRIDE * HEAD_DIM + s2 * blk, blk), :].astype(BF16)
                accs[m] = accs[m] + _dot(xm, w)
        o_ref[slot] = jnp.concatenate(accs, axis=1).reshape(NSA_KV_HEADS, npage * cpp, per * CMP_HIDDEN)


def compress_chunks(src2d, page_rows, col_blk, w1, ni, nb, nch):
    npg = page_rows.shape[1]
    cpp = PAGE_SIZE // CMP_STRIDE
    steps = nch // (CMP_PAGES * cpp)
    per = CMP_LEN // CMP_STRIDE
    idx = jnp.minimum(jnp.arange(steps * CMP_PAGES), npg - 1)
    pr = page_rows[:, idx].reshape(-1).astype(jnp.int32)

    def page_spec(p):
        return pl.BlockSpec((PAGE_SIZE, 2 * NSA_KV),
                            lambda b, s, pr_ref: (pr_ref[(b * steps + s) * CMP_PAGES + p], col_blk))

    return pl.pallas_call(
        functools.partial(_cmpa_kernel, npage=CMP_PAGES),
        out_shape=jax.ShapeDtypeStruct((2, nb, NSA_KV_HEADS, nch, per * CMP_HIDDEN), F32),
        grid_spec=pltpu.PrefetchScalarGridSpec(
            num_scalar_prefetch=1,
            grid=(nb, steps),
            in_specs=[page_spec(p) for p in range(CMP_PAGES)]
            + [pl.BlockSpec((None, 2, CMP_LEN * HEAD_DIM, CMP_HIDDEN), lambda b, s, pr_ref: (ni, 0, 0, 0))],
            out_specs=pl.BlockSpec((2, None, NSA_KV_HEADS, CMP_PAGES * cpp, per * CMP_HIDDEN),
                                   lambda b, s, pr_ref: (0, b, 0, s, 0)),
            scratch_shapes=[pltpu.VMEM((CMP_PAGES, 2 * NSA_KV_HEADS, PAGE_SIZE, HEAD_DIM), F32)]),
        compiler_params=_cp("arbitrary", "arbitrary"),
        name="compress_chunks",
    )(pr, *([src2d] * CMP_PAGES), w1)


def _cmpb_kernel(a_ref, pe_ref, w1_ref, b1_ref, w2_ref, b2_ref, o_ref):
    nch = a_ref.shape[0]
    pew = _dot(pe_ref[...].astype(BF16), w1_ref[...].astype(BF16))[0:1]
    a = a_ref[...]
    hid = b1_ref[...] + pew
    hid = hid + a[:, :CMP_HIDDEN]
    hid = hid + pltpu.roll(a[:, CMP_HIDDEN:], nch - 1, 0)
    act = jax.nn.gelu(hid, approximate=True)
    o_ref[...] = _dot(act.astype(BF16), w2_ref[...].astype(BF16)) + b2_ref[...]


def compress_blocks(a, pe, w1, b1, w2, b2, ni):
    _, nb, g, nch, _ = a.shape
    assert CMP_LEN // CMP_STRIDE == 2
    pe8 = jnp.broadcast_to(pe.reshape(pe.shape[0], 2, 1, CMP_LEN * HEAD_DIM), (pe.shape[0], 2, SUBLANE, CMP_LEN * HEAD_DIM))
    return pl.pallas_call(
        _cmpb_kernel,
        out_shape=jax.ShapeDtypeStruct((2, nb, nch, g * HEAD_DIM), F32),
        grid=(2, nb, g),
        in_specs=[pl.BlockSpec((None, None, None, nch, 2 * CMP_HIDDEN), lambda s, b, gi: (s, b, gi, 0, 0)),
                  pl.BlockSpec((None, None, SUBLANE, CMP_LEN * HEAD_DIM), lambda s, b, gi: (ni, s, 0, 0)),
                  pl.BlockSpec((None, None, CMP_LEN * HEAD_DIM, CMP_HIDDEN), lambda s, b, gi: (ni, s, 0, 0)),
                  pl.BlockSpec((None, None, 1, CMP_HIDDEN), lambda s, b, gi: (ni, s, 0, 0)),
                  pl.BlockSpec((None, None, CMP_HIDDEN, HEAD_DIM), lambda s, b, gi: (ni, s, 0, 0)),
                  pl.BlockSpec((None, None, 1, HEAD_DIM), lambda s, b, gi: (ni, s, 0, 0))],
        out_specs=pl.BlockSpec((None, None, nch, HEAD_DIM), lambda s, b, gi: (s, b, 0, gi)),
        compiler_params=_cp("arbitrary", "arbitrary", "arbitrary"),
        name="compress_blocks",
    )(a, pe8, w1, b1.reshape(b1.shape[0], 2, 1, CMP_HIDDEN), w2, b2.reshape(b2.shape[0], 2, 1, HEAD_DIM))


def _cmp_attn_kernel(q_ref, kc_ref, vc_ref, bias_ref, cov_ref, o_ref, sc_ref):
    kc = kc_ref[...].astype(BF16)
    vc = vc_ref[...].astype(BF16)
    cov = cov_ref[...]
    scale = HEAD_DIM ** -0.5
    score = jnp.zeros(sc_ref.shape, F32)
    for r in range(NSA_GROUP):
        q = q_ref[:, r * HEAD_DIM:(r + 1) * HEAD_DIM].astype(BF16)
        lg = _dot_nt(q, kc) * scale + bias_ref[r]
        mx = jnp.max(lg, axis=-1, keepdims=True)
        e = jnp.exp(lg - mx)
        p = e / jnp.sum(e, axis=-1, keepdims=True) * (mx > 0.1 * NEG_INF).astype(F32)
        o_ref[:, r * HEAD_DIM:(r + 1) * HEAD_DIM] = _dot(p.astype(BF16), vc)
        score = score + jnp.dot(p, cov, precision=HI, preferred_element_type=F32)
    sc_ref[...] = score


def cmp_attention(z3, kvc, bias, cover, tq):
    b, t, _ = z3.shape
    nch = kvc.shape[2]
    nsbp = cover.shape[1]
    gw = NSA_GROUP * HEAD_DIM
    return pl.pallas_call(
        _cmp_attn_kernel,
        out_shape=(jax.ShapeDtypeStruct((b, t, NSA_HEADS * HEAD_DIM), F32),
                   jax.ShapeDtypeStruct((b, NSA_KV_HEADS, t, nsbp), F32)),
        grid=(b, NSA_KV_HEADS, t // tq),
        in_specs=[pl.BlockSpec((None, tq, gw), lambda bi, g, i: (bi, i, g)),
                  pl.BlockSpec((None, None, nch, HEAD_DIM), lambda bi, g, i: (0, bi, 0, g)),
                  pl.BlockSpec((None, None, nch, HEAD_DIM), lambda bi, g, i: (1, bi, 0, g)),
                  pl.BlockSpec((NSA_GROUP, tq, nch), lambda bi, g, i: (g, i, 0)),
                  pl.BlockSpec((nch, nsbp), lambda bi, g, i: (0, 0))],
        out_specs=(pl.BlockSpec((None, tq, gw), lambda bi, g, i: (bi, i, g)),
                   pl.BlockSpec((None, None, tq, nsbp), lambda bi, g, i: (bi, g, i, 0))),
        compiler_params=_cp("arbitrary", "arbitrary", "arbitrary"),
        name="cmp_attention",
    )(z3, kvc, kvc, bias, cover)


def cmp_tables(q_pos, tk, nch, nsbp, table):
    ncb = (tk - CMP_LEN) // CMP_STRIDE + 1
    nsb = -(-tk // SEL_BLK)
    n = jnp.arange(nch)
    dist = q_pos[:, None] - (n * CMP_STRIDE + CMP_LEN - 1)[None, :]
    vis = (dist >= 0) & (n < ncb)[None, :]
    bias = jnp.where(vis[None], _bias_of_dist(dist, table), NEG_INF)
    j = jnp.arange(nsbp)[None, :]
    i = n[:, None]
    cover = ((i * CMP_STRIDE < (j + 1) * SEL_BLK) & (i * CMP_STRIDE + CMP_LEN > j * SEL_BLK)
             & (i < ncb) & (j < nsb)).astype(F32)
    return bias, cover


def select_blocks(score, q_pos, nsb):
    j = jnp.arange(nsb)[None, :]
    cur = (q_pos // SEL_BLK)[:, None]
    forced = (j == 0) | (j == cur) | (j == cur - 1)
    s = score[..., :nsb]
    s = jnp.where((j > cur)[None, None], NEG_SCORE, s + jnp.where(forced, FORCE_BONUS, 0.0)[None, None])
    _, idx = lax.top_k(s, min(SEL_TOPK, nsb))
    return idx


def _nsa_attn_kernel(q_ref, ks_ref, vs_ref, kw_ref, vw_ref, msk_ref, tb_ref, oc_ref, gt_ref, o_ref):
    i = pl.program_id(2)
    tq = QBLK
    rows = NSA_GROUP * tq
    scale = HEAD_DIM ** -0.5
    qall = jnp.concatenate([q_ref[:, r * HEAD_DIM:(r + 1) * HEAD_DIM] for r in range(NSA_GROUP)], axis=0).astype(BF16)
    qpos = i * tq + lax.broadcasted_iota(jnp.int32, (tq, tq), 0)
    kcol = lax.broadcasted_iota(jnp.int32, (tq, tq), 1)
    selm = msk_ref[...]
    nsbp = selm.shape[1]
    per_blk = tq // SEL_BLK
    jrow = lax.broadcasted_iota(jnp.int32, (nsbp, tq), 0)
    jcol = lax.broadcasted_iota(jnp.int32, (nsbp, tq), 1) // SEL_BLK

    def sel_valid(m):
        expand = (jrow == per_blk * m + jcol).astype(F32)
        picked = _dot(selm, expand)
        return (picked > 0.5) & (m * tq + kcol <= qpos)

    def win_valid(m):
        dist = qpos - (m * tq + kcol)
        return (dist >= 0) & (dist < WINDOW)

    def branch(k_ref, v_ref, lo, valid_fn):
        def body(m, carry):
            mx, l, acc = carry
            r0 = pl.multiple_of(m * tq, tq)
            k = k_ref[pl.ds(r0, tq), :].astype(BF16)
            v = v_ref[pl.ds(r0, tq), :].astype(BF16)
            s = _dot_nt(qall, k) * scale
            s = s.reshape(NSA_GROUP, tq, tq) + tb_ref[jnp.minimum(i - m, 2)]
            s = jnp.where(valid_fn(m)[None], s, NEG_INF).reshape(rows, tq)
            mx_new = jnp.maximum(mx, jnp.max(s, axis=-1, keepdims=True))
            alpha = jnp.exp(mx - mx_new)
            p = jnp.exp(s - mx_new)
            l = alpha * l + jnp.sum(p, axis=-1, keepdims=True)
            acc = alpha * acc + _dot(p.astype(BF16), v)
            return mx_new, l, acc

        init = (jnp.full((rows, 1), NEG_INF, F32), jnp.zeros((rows, 1), F32), jnp.zeros((rows, HEAD_DIM), F32))
        _, l, acc = lax.fori_loop(lo, i + 1, body, init)
        return acc / l

    o_sel = branch(ks_ref, vs_ref, 0, sel_valid)
    o_win = branch(kw_ref, vw_ref, jnp.maximum(i - WINDOW // tq, 0), win_valid)
    gt = gt_ref[...]
    gates = 1.0 / (1.0 + jnp.exp(-gt))
    for r in range(NSA_GROUP):
        sl = slice(r * HEAD_DIM, (r + 1) * HEAD_DIM)
        o = (gates[:, 3 * r:3 * r + 1] * oc_ref[:, sl]
             + gates[:, 3 * r + 1:3 * r + 2] * o_sel[r * tq:(r + 1) * tq]
             + gates[:, 3 * r + 2:3 * r + 3] * o_win[r * tq:(r + 1) * tq])
        o_ref[:, sl] = o.astype(BF16)


def nsa_attention_prompt(z3, sel_mask, tb, o_cmp, gate_lin):
    b, t, _ = z3.shape
    gw = NSA_GROUP * HEAD_DIM
    nsbp = sel_mask.shape[-1]
    kv0 = NSA_HEADS * HEAD_DIM // HEAD_DIM
    g4 = NSA_KV_HEADS

    def kv_spec(slot):
        return pl.BlockSpec((None, t, HEAD_DIM), lambda bi, g, i: (bi, 0, kv0 + slot * g4 + g))

    return pl.pallas_call(
        _nsa_attn_kernel,
        out_shape=jax.ShapeDtypeStruct((b, t, NSA_HEADS * HEAD_DIM), BF16),
        grid=(b, NSA_KV_HEADS, t // QBLK),
        in_specs=[pl.BlockSpec((None, QBLK, gw), lambda bi, g, i: (bi, i, g)),
                  kv_spec(2), kv_spec(3), kv_spec(4), kv_spec(5),
                  pl.BlockSpec((None, None, QBLK, nsbp), lambda bi, g, i: (bi, g, i, 0)),
                  pl.BlockSpec((3, NSA_GROUP, QBLK, QBLK), lambda bi, g, i: (0, g, 0, 0)),
                  pl.BlockSpec((None, QBLK, gw), lambda bi, g, i: (bi, i, g)),
                  pl.BlockSpec((None, None, QBLK, 3 * NSA_GROUP), lambda bi, g, i: (bi, g, i, 0))],
        out_specs=pl.BlockSpec((None, QBLK, gw), lambda bi, g, i: (bi, i, g)),
        compiler_params=_cp("arbitrary", "arbitrary", "arbitrary"),
        name="nsa_attention_prompt",
    )(z3, z3, z3, z3, z3, sel_mask, tb, o_cmp, gate_lin)


def _sel_sample_kernel(rb_ref, js_ref, q_ref, k_ref, v_ref, kn_ref, vn_ref, b_ref, o_ref, m_s, l_s, a_s, *, jlast, ksel):
    bi, g, kk = pl.program_id(0), pl.program_id(1), pl.program_id(2)
    j = js_ref[(bi * NSA_KV_HEADS + g) * ksel + kk]
    scale = HEAD_DIM ** -0.5

    @pl.when(kk == 0)
    def _():
        m_s[...] = jnp.full_like(m_s, NEG_INF)
        l_s[...] = jnp.zeros_like(l_s)
        a_s[...] = jnp.zeros_like(a_s)

    is_new = j == jlast
    first_row = lax.broadcasted_iota(jnp.int32, (SEL_BLK, HEAD_DIM), 0) == 0
    k = jnp.where(is_new, jnp.where(first_row, kn_ref[...], 0.0), k_ref[...])
    v = jnp.where(is_new, jnp.where(first_row, vn_ref[...], 0.0), v_ref[...])
    s = _dot_nt(q_ref[...].astype(BF16), k.astype(BF16)) * scale + b_ref[...]
    valid = jnp.logical_or(jnp.logical_not(is_new), lax.broadcasted_iota(jnp.int32, s.shape, 1) == 0)
    s = jnp.where(valid, s, NEG_INF)
    mx = m_s[...]
    mx_new = jnp.maximum(mx, jnp.max(s, axis=-1, keepdims=True))
    alpha = jnp.exp(mx - mx_new)
    p = jnp.exp(s - mx_new)
    l_new = alpha * l_s[...] + jnp.sum(p, axis=-1, keepdims=True)
    a_new = alpha * a_s[...] + _dot(p.astype(BF16), v.astype(BF16))
    m_s[...] = mx_new
    l_s[...] = l_new
    a_s[...] = a_new

    @pl.when(kk == ksel - 1)
    def _():
        o_ref[...] = a_new / l_new


def sel_attention_sample(q4, cache2d, row_blk, jsel, k_new, v_new, bias, ni, n_layers, jlast):
    db, g, r, hd = q4.shape
    ksel = jsel.shape[-1]
    cb = ni * 4 * NSA_KV_HEADS

    def blk_spec(slot):
        return pl.BlockSpec((SEL_BLK, HEAD_DIM),
                            lambda bi, gi, kk, rb, js: (rb[(bi * g + gi) * ksel + kk], cb + slot * NSA_KV_HEADS + gi))

    return pl.pallas_call(
        functools.partial(_sel_sample_kernel, jlast=jlast, ksel=ksel),
        out_shape=jax.ShapeDtypeStruct((db, g, r, hd), F32),
        grid_spec=pltpu.PrefetchScalarGridSpec(
            num_scalar_prefetch=2,
            grid=(db, g, ksel),
            in_specs=[pl.BlockSpec((None, None, r, hd), lambda bi, gi, kk, rb, js: (bi, gi, 0, 0)),
                      blk_spec(2), blk_spec(3),
                      pl.BlockSpec((None, None, 1, hd), lambda bi, gi, kk, rb, js: (bi, gi, 0, 0)),
                      pl.BlockSpec((None, None, 1, hd), lambda bi, gi, kk, rb, js: (bi, gi, 0, 0)),
                      pl.BlockSpec((None, None, r, SEL_BLK),
                                   lambda bi, gi, kk, rb, js: (gi, js[(bi * g + gi) * ksel + kk], 0, 0))],
            out_specs=pl.BlockSpec((None, None, r, hd), lambda bi, gi, kk, rb, js: (bi, gi, 0, 0)),
            scratch_shapes=[pltpu.VMEM((r, 1), F32), pltpu.VMEM((r, 1), F32), pltpu.VMEM((r, hd), F32)]),
        compiler_params=_cp("arbitrary", "arbitrary", "arbitrary"),
        name="sel_attention_sample",
    )(row_blk.reshape(-1), jsel.reshape(-1), q4, cache2d, cache2d, k_new, v_new, bias)


def _win_sample_kernel(q_ref, kb_ref, vb_ref, kn_ref, vn_ref, bw_ref, b0_ref, oc_ref, os_ref, gt_ref, o_ref):
    scale = HEAD_DIM ** -0.5
    q = q_ref[...]
    wb = kb_ref.shape[0]
    s = _dot_nt(q.astype(BF16), kb_ref[...].astype(BF16)) * scale + bw_ref[...]
    dist = wb - lax.broadcasted_iota(jnp.int32, s.shape, 1)
    s = jnp.where(dist < WINDOW, s, NEG_INF)
    s_new = jnp.sum(q * kn_ref[...], axis=-1, keepdims=True) * scale + b0_ref[...][:, 0:1]
    mx = jnp.maximum(jnp.max(s, axis=-1, keepdims=True), s_new)
    e = jnp.exp(s - mx)
    e_new = jnp.exp(s_new - mx)
    l = jnp.sum(e, axis=-1, keepdims=True) + e_new
    o_win = (_dot(e.astype(BF16), vb_ref[...].astype(BF16)) + e_new * vn_ref[...]) / l
    gates = 1.0 / (1.0 + jnp.exp(-gt_ref[...]))
    o_ref[...] = gates[:, 0:1] * oc_ref[...] + gates[:, 1:2] * os_ref[...] + gates[:, 2:3] * o_win


def win_attention_sample(q4, win3, k_new, v_new, bias_w, bias0, o_cmp4, o_sel4, gate4, ni):
    db, g, r, hd = q4.shape
    wb = win3.shape[1]

    def small(shape_last):
        return pl.BlockSpec((None, None, r, shape_last), lambda bi, gi: (bi, gi, 0, 0))

    return pl.pallas_call(
        _win_sample_kernel,
        out_shape=jax.ShapeDtypeStruct((db, g, r, hd), F32),
        grid=(db, g),
        in_specs=[small(hd),
                  pl.BlockSpec((None, wb, hd), lambda bi, gi: (ni * db + bi, 0, gi)),
                  pl.BlockSpec((None, wb, hd), lambda bi, gi: (ni * db + bi, 0, g + gi)),
                  pl.BlockSpec((None, None, 1, hd), lambda bi, gi: (bi, gi, 0, 0)),
                  pl.BlockSpec((None, None, 1, hd), lambda bi, gi: (bi, gi, 0, 0)),
                  pl.BlockSpec((None, r, wb), lambda bi, gi: (gi, 0, 0)),
                  pl.BlockSpec((None, r, LANE), lambda bi, gi: (gi, 0, 0)),
                  small(hd), small(hd), small(3)],
        out_specs=small(hd),
        compiler_params=_cp("arbitrary", "arbitrary"),
        name="win_attention_sample",
    )(q4, win3, win3, k_new, v_new, bias_w, bias0, o_cmp4, o_sel4, gate4)


def nsa_layer(h_all, bt, b, t, db, ni, cache_nsa, state_nsa_win, page_table, rel_bias,
              nsa_w_in, nsa_cmp_pe, nsa_cmp_w1, nsa_cmp_b1, nsa_cmp_w2, nsa_cmp_b2):
    nq = NSA_HEADS * HEAD_DIM
    nqkv = nq + 6 * NSA_KV
    z_p, z_s = dense_prompt_and_sample(h_all, h_all, nsa_w_in, (ni,), bt, ncols=nqkv)
    w_gate = jnp.zeros((h_all.shape[1], LANE), F32).at[:, :3 * NSA_HEADS].set(nsa_w_in[ni, :, nqkv:])
    zg_p, zg_s = dense_prompt_and_sample(h_all, h_all, w_gate, (), bt)
    wz = z_p.shape[1]
    z3 = z_p.reshape(b, t, wz)
    n_layers = cache_nsa.shape[1]
    past_len = page_table.shape[1] * PAGE_SIZE
    pos_p = jnp.arange(t)

    ncb, nch, nsb, nsbp = _cmp_geometry(t)
    page_rows = (jnp.arange(b)[:, None] * (t // PAGE_SIZE) + jnp.arange(t // PAGE_SIZE)[None, :])
    a_p = compress_chunks(z_p, page_rows, nq // (2 * NSA_KV), nsa_cmp_w1, ni, b, nch)
    kvc_p = compress_blocks(a_p, nsa_cmp_pe, nsa_cmp_w1, nsa_cmp_b1, nsa_cmp_w2, nsa_cmp_b2, ni)
    bias_p, cover_p = cmp_tables(pos_p, t, nch, nsbp, rel_bias)
    o_cmp_p, score_p = cmp_attention(z3, kvc_p, bias_p, cover_p, QBLK)
    idx_p = select_blocks(score_p, pos_p, nsb)
    sel_mask = (idx_p[..., None] == jnp.arange(nsbp)).any(axis=-2).astype(F32)
    ii = jnp.arange(QBLK)
    tb = _bias_of_dist(jnp.arange(3)[:, None, None] * QBLK + ii[None, :, None] - ii[None, None, :], rel_bias)
    tb = jnp.moveaxis(tb, 0, 1)
    gate_lin = zg_p[:, :3 * NSA_HEADS].reshape(b, t, NSA_KV_HEADS, 3 * NSA_GROUP).transpose((0, 2, 1, 3))
    o_p = nsa_attention_prompt(z3, sel_mask, tb, o_cmp_p, gate_lin)
    rows_p = z3[:, :, nq:nq + 4 * NSA_KV].reshape(b, t, 4, NSA_KV_HEADS, HEAD_DIM)
    keep = min(WINDOW, t)
    win_p = z3[:, t - keep:, nq + 4 * NSA_KV:nq + 6 * NSA_KV].reshape(b, keep, 2, NSA_KV_HEADS, HEAD_DIM)

    zs = z_s[:db]
    tk = past_len + DEC_SEQ
    ncb_s, nch_s, nsb_s, nsbp_s = _cmp_geometry(tk)
    assert (ncb_s + 1) * CMP_STRIDE <= past_len
    cache2d = cache_nsa.reshape(cache_nsa.shape[0] * PAGE_SIZE, n_layers * 4 * NSA_KV)
    a_s = compress_chunks(cache2d, page_table, ni * 2, nsa_cmp_w1, ni, db, nch_s)
    kvc_s = compress_blocks(a_s, nsa_cmp_pe, nsa_cmp_w1, nsa_cmp_b1, nsa_cmp_w2, nsa_cmp_b2, ni)
    pos_s = jnp.full((SUBLANE,), past_len, jnp.int32)
    bias_s, cover_s = cmp_tables(pos_s, tk, nch_s, nsbp_s, rel_bias)
    zs3 = jnp.zeros((db, SUBLANE, wz), F32).at[:, 0].set(zs)
    o_cmp_s, score_s = cmp_attention(zs3, kvc_s, bias_s, cover_s, SUBLANE)
    idx_s = select_blocks(score_s[:, :, 0:1], pos_s[0:1], nsb_s)[:, :, 0]
    jlast = past_len // SEL_BLK
    per_page = PAGE_SIZE // SEL_BLK
    page_of = jnp.take_along_axis(page_table, jnp.minimum(idx_s, jlast - 1).reshape(db, -1) // per_page, axis=1)
    row_blk = (page_of.reshape(idx_s.shape) * per_page + jnp.minimum(idx_s, jlast - 1) % per_page).astype(jnp.int32)
    q4 = zs[:, :nq].reshape(db, NSA_KV_HEADS, NSA_GROUP, HEAD_DIM)
    kv_new = zs[:, nq:nq + 6 * NSA_KV].reshape(db, 6, NSA_KV_HEADS, 1, HEAD_DIM)
    kpos = jnp.arange(nsb_s * SEL_BLK).reshape(nsb_s, SEL_BLK)
    bias_sel = _bias_of_dist(past_len - kpos, rel_bias).reshape(NSA_KV_HEADS, NSA_GROUP, nsb_s, SEL_BLK).transpose((0, 2, 1, 3))
    o_sel_s = sel_attention_sample(q4, cache2d, row_blk, idx_s.astype(jnp.int32), kv_new[:, 2], kv_new[:, 3],
                                   bias_sel, ni, n_layers, jlast)
    wb = state_nsa_win.shape[2]
    win3 = state_nsa_win.reshape(state_nsa_win.shape[0] * db, wb, 2 * NSA_KV)
    bias_w = _bias_of_dist(wb - jnp.arange(wb), rel_bias).reshape(NSA_KV_HEADS, NSA_GROUP, wb)
    bias0 = jnp.broadcast_to(_bias_of_dist(jnp.zeros((1,), jnp.int32), rel_bias).reshape(NSA_KV_HEADS, NSA_GROUP, 1),
                             (NSA_KV_HEADS, NSA_GROUP, LANE))
    gate4 = zg_s[:db, :3 * NSA_HEADS].reshape(db, NSA_KV_HEADS, NSA_GROUP, 3)
    o_cmp4 = o_cmp_s[:, 0].reshape(db, NSA_KV_HEADS, NSA_GROUP, HEAD_DIM)
    o_s = win_attention_sample(q4, win3, kv_new[:, 4], kv_new[:, 5], bias_w, bias0, o_cmp4, o_sel_s, gate4, ni)
    rows_s = zs[:, nq:nq + 4 * NSA_KV].reshape(db, DEC_SEQ, 4, NSA_KV_HEADS, HEAD_DIM)
    new_win = zs[:, nq + 4 * NSA_KV:nq + 6 * NSA_KV].reshape(db, DEC_SEQ, 2, NSA_KV_HEADS, HEAD_DIM)
    win_all = jnp.concatenate([state_nsa_win[ni], new_win], axis=1)
    keep_s = min(WINDOW, wb + DEC_SEQ)
    win_s = win_all[:, wb + DEC_SEQ - keep_s:]
    return o_p.reshape(bt, nq), o_s.reshape(db, nq), rows_p, rows_s, win_p, win_s


def kernel(x_prompt, x_sample, c_prompt, c_sample, state_ret, cache_nsa, state_nsa_win, page_table, rel_bias,
           ada_w, ada_b, ln_g, ln_b, ret_w_in, ret_gn_w, ret_w_out, nsa_w_in, nsa_cmp_pe, nsa_cmp_w1, nsa_cmp_b1,
           nsa_cmp_w2, nsa_cmp_b2, nsa_w_out, moe_w_router, moe_b_router, moe_w_up, moe_b_up, moe_w_down, moe_b_down):
    b, t, d = x_prompt.shape
    db, ds, _ = x_sample.shape
    assert ds == DEC_SEQ == 1 and db <= SUBLANE and t % ROW_TILE == 0 and t % QBLK == 0
    bt = b * t
    npad = bt + ROW_TILE
    ntok = bt + db
    tiles_per_seq = t // ROW_TILE
    past_len = page_table.shape[1] * PAGE_SIZE

    x_all = jnp.zeros((npad, d), F32).at[:bt].set(x_prompt.reshape(bt, d)).at[bt:ntok].set(x_sample.reshape(db, d))
    rc = -(-(b + db) // SUBLANE) * SUBLANE
    c_all = jnp.zeros((rc, d), F32).at[:b].set(c_prompt).at[b:b + db].set(c_sample)
    mods = ada_all(c_all, ada_w, ada_b)

    def mod_slabs(li, sub):
        m = mods[li * 2 + sub]
        return tuple(_slabs(m[:, k * d:(k + 1) * d], b, db) for k in range(3))

    def tail_rows(a_s, width, dtype):
        return jnp.zeros((ROW_TILE, width), dtype).at[:db].set(a_s.astype(dtype))

    ret_sp, ret_ss, rows_p, rows_s, win_p, win_s = [], [], [], [], [], []
    for li in range(DEPTH):
        sh, sc, gt = mod_slabs(li, 0)
        h_all = modulate(x_all, sc, sh, tiles_per_seq, b)
        if li % N_MIXERS == 0:
            ri = li // N_MIXERS
            z_p, z_s = dense_prompt_and_sample(h_all, h_all, ret_w_in, (ri,), bt)
            a_p, sp = retention(z_p.reshape(b, t, RET_IN), jnp.arange(t), None, ret_gn_w, ri, min(RET_CHUNK, t) if t % RET_CHUNK == 0 else t)
            zs3 = jnp.zeros((db, SUBLANE, RET_IN), F32).at[:, 0].set(z_s[:db])
            a_s, ss = retention(zs3, jnp.full((SUBLANE,), past_len), state_ret, ret_gn_w, ri, DEC_SEQ)
            ret_sp.append(sp)
            ret_ss.append(ss)
            nv = RET_HEADS * RET_DV
            y_all = dense_prompt_and_sample(a_p.reshape(bt, nv), tail_rows(a_s[:, 0], nv, BF16), ret_w_out, (ri,), bt,
                                            into_rows=npad)
        else:
            ni = li // N_MIXERS
            o_p, o_s, rp, rs, wp, wsb = nsa_layer(h_all, bt, b, t, db, ni, cache_nsa, state_nsa_win, page_table, rel_bias,
                                                  nsa_w_in, nsa_cmp_pe, nsa_cmp_w1, nsa_cmp_b1, nsa_cmp_w2, nsa_cmp_b2)
            rows_p.append(rp)
            rows_s.append(rs)
            win_p.append(wp)
            win_s.append(wsb)
            nq = NSA_HEADS * HEAD_DIM
            y_all = dense_prompt_and_sample(o_p, tail_rows(o_s, nq, BF16), nsa_w_out, (ni,), bt, into_rows=npad)
        x_all = resid_ln(x_all, y_all, gt, ln_g, ln_b, li, 0, tiles_per_seq, b)

        sh, sc, gt = mod_slabs(li, 1)
        h_f32, logits = modulate_router(x_all, sc, sh, moe_w_router, moe_b_router, li, tiles_per_seq, b)
        y_all = moe(h_f32, logits, ntok, li, moe_w_up, moe_b_up, moe_w_down, moe_b_down)
        x_all = resid_ln(x_all, y_all, gt, ln_g, ln_b, li, 1, tiles_per_seq, b)

    y_prompt = x_all[:bt].reshape(b, t, d)
    y_sample = x_all[bt:ntok].reshape(db, ds, d)
    return (y_prompt, y_sample, jnp.stack(ret_sp), jnp.stack(ret_ss), jnp.stack(rows_p, axis=1),
            jnp.stack(rows_s, axis=1), jnp.stack(win_p), jnp.stack(win_s))
```

```python
import functools
import math

import jax
import jax.numpy as jnp
from jax import lax
from jax.experimental import pallas as pl
from jax.experimental.pallas import tpu as pltpu

D_MODEL = 4096
BATCH = 4
SEQ = 2048
DEPTH = 2
DEC_BATCH = 8
DEC_SEQ = 1
PAST_LEN = 16384
PAGE_SIZE = 128

N_MIXERS = 2
N_RET_LAYERS = (DEPTH + 1) // 2
N_NSA_LAYERS = DEPTH // 2

DN_ALPHA = (2.0 * DEPTH) ** 0.25
LN_EPS = 1e-5
NEG_INF = -1e30
NEG_SCORE = -1e9

RET_HEADS = 16
RET_DK = D_MODEL // RET_HEADS
RET_DV = 2 * RET_DK
RET_CHUNK = 128
ROPE_BASE = 10000.0

NSA_HEADS = 32
HEAD_DIM = D_MODEL // NSA_HEADS
NSA_KV_HEADS = 4
NSA_GROUP = NSA_HEADS // NSA_KV_HEADS
NSA_KV = NSA_KV_HEADS * HEAD_DIM
CMP_LEN = 32
CMP_STRIDE = 16
CMP_HIDDEN = 2 * HEAD_DIM
SEL_BLK = 64
SEL_TOPK = 16
WINDOW = 512
QBLK = 128
FORCE_BONUS = 1e6

REL_BUCKETS = 32
REL_MAX_DIST = 128

N_EXPERTS = 32
TOP_K = 4
D_EXPERT = D_MODEL // 2
SWIGLU_ALPHA = 1.702
SWIGLU_LIMIT = 7.0

RET_IN = 2 * RET_HEADS * RET_DK + 2 * RET_HEADS * RET_DV
NSA_IN = NSA_HEADS * HEAD_DIM + 6 * NSA_KV + 3 * NSA_HEADS

F32 = jnp.float32
BF16 = jnp.bfloat16
HI = lax.Precision.HIGHEST

LANE = 128
SUBLANE = 8
VMEM_LIMIT_BYTES = 56 * 1024 * 1024

ROW_TILE = 256
MOE_TM = 256
MOE_GATHER_ROWS = 256
MOE_COMBINE_ROWS = 128
CMP_PAGES = 8


def _cp(*sem, vmem=VMEM_LIMIT_BYTES):
    return pltpu.CompilerParams(dimension_semantics=sem, vmem_limit_bytes=vmem)


def _dot(a, b):
    return jnp.dot(a, b, preferred_element_type=F32)


def _dot_nt(a, b):
    return lax.dot_general(a, b, (((1,), (1,)), ((), ())), preferred_element_type=F32)


def _dot_tn(a, b):
    return lax.dot_general(a, b, (((0,), (0,)), ((), ())), preferred_element_type=F32)


def _pick(n, *cands):
    for c in cands:
        if n % c == 0:
            return c
    return n


def _ada_kernel(c_ref, w_ref, b_ref, o_ref):
    c = c_ref[...]
    s = (c / (1.0 + jnp.exp(-c))).astype(BF16)
    o_ref[...] = _dot(s, w_ref[...].astype(BF16)) + b_ref[...]


def ada_all(c_all, ada_w, ada_b):
    rc, d = c_all.shape
    nl = ada_w.shape[0] * ada_w.shape[1]
    w = ada_w.reshape(nl, d, 3 * d)
    b = ada_b.reshape(nl, 1, 3 * d)
    tn = _pick(3 * d, 512, 256, 128)
    return pl.pallas_call(
        _ada_kernel,
        out_shape=jax.ShapeDtypeStruct((nl, rc, 3 * d), F32),
        grid=(nl, 3 * d // tn),
        in_specs=[pl.BlockSpec((rc, d), lambda l, j: (0, 0)),
                  pl.BlockSpec((None, d, tn), lambda l, j: (l, 0, j)),
                  pl.BlockSpec((None, 1, tn), lambda l, j: (l, 0, j))],
        out_specs=pl.BlockSpec((None, rc, tn), lambda l, j: (l, 0, j)),
        compiler_params=_cp("arbitrary", "arbitrary"),
        name="ada_mod",
    )(c_all, w, b)


def _slabs(mod, nb, ndb):
    d = mod.shape[-1]
    p = jnp.broadcast_to(mod[:nb, None, :], (nb, SUBLANE, d))
    s = jnp.zeros((1, SUBLANE, d), F32).at[0, :ndb].set(mod[nb:nb + ndb])
    return jnp.concatenate([p, s], axis=0)


def _mod_kernel(x_ref, sc_ref, sh_ref, o_ref):
    tm, d = x_ref.shape
    x = x_ref[...].reshape(tm // SUBLANE, SUBLANE, d)
    h = x * (1.0 + sc_ref[...][None]) + sh_ref[...][None]
    o_ref[...] = h.reshape(tm, d).astype(o_ref.dtype)


def _mod_router_kernel(x_ref, sc_ref, sh_ref, wr_ref, br_ref, o_ref, lg_ref):
    tm, d = x_ref.shape
    x = x_ref[...].reshape(tm // SUBLANE, SUBLANE, d)
    h = (x * (1.0 + sc_ref[...][None]) + sh_ref[...][None]).reshape(tm, d)
    o_ref[...] = h
    lg_ref[...] = jnp.dot(h, wr_ref[...], precision=HI, preferred_element_type=F32) + br_ref[...]


def _slab_spec(d, tiles_per_seq, nb):
    return pl.BlockSpec((None, SUBLANE, d), lambda i: (jnp.minimum(i // tiles_per_seq, nb), 0, 0))


def modulate(x_all, sc, sh, tiles_per_seq, nb):
    n, d = x_all.shape
    return pl.pallas_call(
        _mod_kernel,
        out_shape=jax.ShapeDtypeStruct((n, d), BF16),
        grid=(n // ROW_TILE,),
        in_specs=[pl.BlockSpec((ROW_TILE, d), lambda i: (i, 0)),
                  _slab_spec(d, tiles_per_seq, nb), _slab_spec(d, tiles_per_seq, nb)],
        out_specs=pl.BlockSpec((ROW_TILE, d), lambda i: (i, 0)),
        compiler_params=_cp("arbitrary"),
        name="modulate",
    )(x_all, sc, sh)


def modulate_router(x_all, sc, sh, w_r, b_r, li, tiles_per_seq, nb):
    n, d = x_all.shape
    ne = w_r.shape[-1]
    b_r3 = b_r.reshape(b_r.shape[0], 1, ne)
    return pl.pallas_call(
        _mod_router_kernel,
        out_shape=(jax.ShapeDtypeStruct((n, d), F32), jax.ShapeDtypeStruct((n, ne), F32)),
        grid=(n // ROW_TILE,),
        in_specs=[pl.BlockSpec((ROW_TILE, d), lambda i: (i, 0)),
                  _slab_spec(d, tiles_per_seq, nb), _slab_spec(d, tiles_per_seq, nb),
                  pl.BlockSpec((None, d, ne), lambda i: (li, 0, 0)),
                  pl.BlockSpec((None, 1, ne), lambda i: (li, 0, 0))],
        out_specs=(pl.BlockSpec((ROW_TILE, d), lambda i: (i, 0)),
                   pl.BlockSpec((ROW_TILE, ne), lambda i: (i, 0))),
        compiler_params=_cp("arbitrary"),
        name="modulate_router",
    )(x_all, sc, sh, w_r, b_r3)


def _resid_ln_kernel(x_ref, y_ref, gt_ref, g_ref, b_ref, o_ref):
    tm, d = x_ref.shape
    x = x_ref[...].reshape(tm // SUBLANE, SUBLANE, d)
    y = y_ref[...].reshape(tm // SUBLANE, SUBLANE, d)
    v = (DN_ALPHA * x + gt_ref[...][None] * y).reshape(tm, d)
    mu = jnp.mean(v, axis=-1, keepdims=True)
    c = v - mu
    var = jnp.mean(c * c, axis=-1, keepdims=True)
    o_ref[...] = c * lax.rsqrt(var + LN_EPS) * g_ref[...] + b_ref[...]


def resid_ln(x_all, y_all, gt, ln_g, ln_b, li, sub, tiles_per_seq, nb):
    n, d = x_all.shape
    g3 = ln_g.reshape(ln_g.shape[0] * ln_g.shape[1], 1, d)
    b3 = ln_b.reshape(ln_b.shape[0] * ln_b.shape[1], 1, d)
    idx = li * 2 + sub
    return pl.pallas_call(
        _resid_ln_kernel,
        out_shape=jax.ShapeDtypeStruct((n, d), F32),
        grid=(n // ROW_TILE,),
        in_specs=[pl.BlockSpec((ROW_TILE, d), lambda i: (i, 0)),
                  pl.BlockSpec((ROW_TILE, d), lambda i: (i, 0)),
                  _slab_spec(d, tiles_per_seq, nb),
                  pl.BlockSpec((None, 1, d), lambda i: (idx, 0, 0)),
                  pl.BlockSpec((None, 1, d), lambda i: (idx, 0, 0))],
        out_specs=pl.BlockSpec((ROW_TILE, d), lambda i: (i, 0)),
        compiler_params=_cp("arbitrary"),
        name="resid_ln",
    )(x_all, y_all, gt, g3, b3)


def _mm_kernel(x_ref, w_ref, o_ref):
    @pl.when(pl.program_id(2) == 0)
    def _():
        o_ref[...] = jnp.zeros_like(o_ref)

    o_ref[...] += _dot(x_ref[...], w_ref[...].astype(BF16))


def _mm_into_kernel(prev_ref, x_ref, w_ref, o_ref):
    del prev_ref
    _mm_kernel(x_ref, w_ref, o_ref)


def matmul(x, w, wsel, *, row0, rows, tm, tn, tk, ncols=None, out_rows=None, out_row0=0, into=None):
    kdim = x.shape[1]
    n = w.shape[-1] if ncols is None else ncols
    assert n % tn == 0
    nj = n // tn
    assert rows % tm == 0 and row0 % tm == 0 and out_row0 % tm == 0 and kdim % tk == 0
    rb0, ob0 = row0 // tm, out_row0 // tm
    out_rows = rows if out_rows is None else out_rows
    nlead = len(wsel)
    in_specs = [pl.BlockSpec((tm, tk), lambda i, j, k: (rb0 + i, k)),
                pl.BlockSpec((None,) * nlead + (tk, tn), lambda i, j, k: tuple(wsel) + (k, j))]
    args = [x, w]
    kern, aliases = _mm_kernel, {}
    if into is not None:
        assert into.shape == (out_rows, nj * tn)
        in_specs = [pl.BlockSpec(memory_space=pl.ANY)] + in_specs
        args = [into] + args
        kern, aliases = _mm_into_kernel, {0: 0}
    return pl.pallas_call(
        kern,
        out_shape=jax.ShapeDtypeStruct((out_rows, nj * tn), F32),
        grid=(rows // tm, nj, kdim // tk),
        in_specs=in_specs,
        out_specs=pl.BlockSpec((tm, tn), lambda i, j, k: (ob0 + i, j)),
        input_output_aliases=aliases,
        compiler_params=_cp("arbitrary", "arbitrary", "arbitrary"),
        name="matmul",
    )(*args)


def dense_prompt_and_sample(x_p, x_s, w, wsel, n_prompt, *, ncols=None, into_rows=None):
    kdim = x_p.shape[1]
    n = w.shape[-1] if ncols is None else ncols
    tm = _pick(n_prompt, 2048, 1024, 512, 256)
    tn = _pick(n, 1024, 512, 256, 128)
    tk = _pick(kdim, 512, 256)
    s_row0 = x_s.shape[0] - ROW_TILE
    if into_rows is None:
        z_p = matmul(x_p, w, wsel, row0=0, rows=n_prompt, tm=tm, tn=tn, tk=tk, ncols=ncols)
        z_s = matmul(x_s, w, wsel, row0=s_row0, rows=ROW_TILE, tm=ROW_TILE, tn=tn, tk=tk, ncols=ncols)
        return z_p, z_s
    y = jnp.zeros((into_rows, n), F32)
    y = matmul(x_p, w, wsel, row0=0, rows=n_prompt, tm=tm, tn=tn, tk=tk, ncols=ncols, out_rows=into_rows, into=y)
    return matmul(x_s, w, wsel, row0=s_row0, rows=ROW_TILE, tm=ROW_TILE, tn=tn, tk=tk, ncols=ncols,
                  out_rows=into_rows, out_row0=n_prompt, into=y)


def _ret_kernel(*refs, has_s0, nchunks):
    (q_ref, k_ref, v_ref, g_ref, cos_ref, sin_ref, dec_ref, qd_ref, kd_ref, cd_ref, gn_ref), rest = refs[:11], refs[11:]
    if has_s0:
        s0_ref, a_ref, so_ref, s_scr = rest
    else:
        a_ref, so_ref, s_scr = rest
    c = pl.program_id(2)

    @pl.when(c == 0)
    def _():
        s_scr[...] = s0_ref[...] if has_s0 else jnp.zeros_like(s_scr)

    half = RET_DK // 2
    cos = cos_ref[...]
    sin = sin_ref[...]

    def rot(x):
        x1, x2 = x[:, :half], x[:, half:]
        return jnp.concatenate([x1 * cos - x2 * sin, x1 * sin + x2 * cos], axis=-1)

    q = rot(q_ref[...])
    k = rot(k_ref[...]) * (RET_DK ** -0.5)
    qb = q.astype(BF16)
    vb = v_ref[...].astype(BF16)
    s = s_scr[...]
    att = _dot_nt(qb, k.astype(BF16)) * dec_ref[...]
    o = _dot(att.astype(BF16), vb) + _dot(qb, s.astype(BF16)) * qd_ref[...]
    s_new = s * cd_ref[...] + _dot_tn((k * kd_ref[...]).astype(BF16), vb)
    s_scr[...] = s_new
    mu = jnp.mean(o, axis=-1, keepdims=True)
    oc = o - mu
    var = jnp.mean(oc * oc, axis=-1, keepdims=True)
    on = oc * lax.rsqrt(var + LN_EPS) * gn_ref[...]
    g = g_ref[...]
    a_ref[...] = ((g / (1.0 + jnp.exp(-g))) * on).astype(BF16)

    @pl.when(c == nchunks - 1)
    def _():
        so_ref[...] = s_new


def retention(z3, pos, s0, gn_w, ri, true_chunk):
    b, tpad, _ = z3.shape
    cpad = min(RET_CHUNK, tpad)
    nchunks = tpad // cpad
    nk, nv = RET_HEADS * RET_DK, RET_HEADS * RET_DV
    half = RET_DK // 2
    inv = 1.0 / (ROPE_BASE ** (jnp.arange(half, dtype=F32) / half))
    ang = pos.astype(F32)[:, None] * inv[None, :]
    cos, sin = jnp.cos(ang), jnp.sin(ang)
    lg = jnp.log1p(-jnp.exp2(-5.0 - jnp.arange(RET_HEADS, dtype=F32)))
    i = jnp.arange(cpad, dtype=F32)
    diff = i[:, None] - i[None, :]
    decay = jnp.where(diff >= 0, jnp.exp(jnp.maximum(diff, 0.0)[None] * lg[:, None, None]), 0.0)
    q_dec = jnp.exp((i + 1.0)[None, :] * lg[:, None])[:, :, None]
    k_dec = jnp.exp((true_chunk - 1.0 - i)[None, :] * lg[:, None])[:, :, None]
    c_dec = jnp.exp(true_chunk * lg)[:, None, None]
    kq, kv = RET_DK, RET_DV
    in_specs = [
        pl.BlockSpec((None, cpad, kq), lambda bi, h, c: (bi, c, h)),
        pl.BlockSpec((None, cpad, kq), lambda bi, h, c: (bi, c, nk // kq + h)),
        pl.BlockSpec((None, cpad, kv), lambda bi, h, c: (bi, c, 2 * nk // kv + h)),
        pl.BlockSpec((None, cpad, kv), lambda bi, h, c: (bi, c, (2 * nk + nv) // kv + h)),
        pl.BlockSpec((cpad, half), lambda bi, h, c: (c, 0)),
        pl.BlockSpec((cpad, half), lambda bi, h, c: (c, 0)),
        pl.BlockSpec((None, cpad, cpad), lambda bi, h, c: (h, 0, 0)),
        pl.BlockSpec((None, cpad, 1), lambda bi, h, c: (h, 0, 0)),
        pl.BlockSpec((None, cpad, 1), lambda bi, h, c: (h, 0, 0)),
        pl.BlockSpec((None, 1, 1), lambda bi, h, c: (h, 0, 0)),
        pl.BlockSpec((None, 1, kv), lambda bi, h, c: (ri, 0, h)),
    ]
    args = [z3, z3, z3, z3, cos, sin, decay, q_dec, k_dec, c_dec, gn_w.reshape(gn_w.shape[0], 1, nv)]
    if s0 is not None:
        in_specs.append(pl.BlockSpec((None, None, None, kq, kv), lambda bi, h, c: (ri, bi, h, 0, 0)))
        args.append(s0)
    return pl.pallas_call(
        functools.partial(_ret_kernel, has_s0=s0 is not None, nchunks=nchunks),
        out_shape=(jax.ShapeDtypeStruct((b, tpad, nv), BF16),
                   jax.ShapeDtypeStruct((b, RET_HEADS, kq, kv), F32)),
        grid=(b, RET_HEADS, nchunks),
        in_specs=in_specs,
        out_specs=(pl.BlockSpec((None, cpad, kv), lambda bi, h, c: (bi, c, h)),
                   pl.BlockSpec((None, None, kq, kv), lambda bi, h, c: (bi, h, 0, 0))),
        scratch_shapes=[pltpu.VMEM((kq, kv), F32)],
        compiler_params=_cp("arbitrary", "arbitrary", "arbitrary"),
        name="retention",
    )(*args)


DMA_UNROLL = 8


def _moe_gather_kernel(tok_ref, nu_ref, h_hbm, o_ref, buf, sem, *, rb):
    i = pl.program_id(0)
    n_steps = nu_ref[0] * MOE_TM // rb

    def row_copy(step, r):
        slot = step % 2
        return pltpu.make_async_copy(h_hbm.at[pl.ds(tok_ref[step * rb + r], 1)], buf.at[slot, pl.ds(r, 1)], sem.at[slot])

    def start_block(step):
        def body(r8, carry):
            for u in range(DMA_UNROLL):
                row_copy(step, r8 * DMA_UNROLL + u).start(priority=u % 2)
            return carry

        lax.fori_loop(0, rb // DMA_UNROLL, body, 0)

    def wait_block(step):
        def body(r8, carry):
            for u in range(DMA_UNROLL):
                row_copy(step, r8 * DMA_UNROLL + u).wait()
            return carry

        lax.fori_loop(0, rb // DMA_UNROLL, body, 0)

    @pl.when(jnp.logical_and(i == 0, n_steps > 0))
    def _():
        start_block(0)

    @pl.when(i + 1 < n_steps)
    def _():
        start_block(i + 1)

    @pl.when(i < n_steps)
    def _():
        wait_block(i)
        o_ref[...] = buf[i % 2].astype(BF16)

    @pl.when(i >= n_steps)
    def _():
        o_ref[...] = jnp.zeros_like(o_ref)


def moe_gather(h_all, row_tok, n_used):
    rows = row_tok.shape[0]
    d = h_all.shape[1]
    rb = MOE_GATHER_ROWS
    return pl.pallas_call(
        functools.partial(_moe_gather_kernel, rb=rb),
        out_shape=jax.ShapeDtypeStruct((rows, d), BF16),
        grid_spec=pltpu.PrefetchScalarGridSpec(
            num_scalar_prefetch=2,
            grid=(rows // rb,),
            in_specs=[pl.BlockSpec(memory_space=pl.ANY)],
            out_specs=pl.BlockSpec((rb, d), lambda i, tok, nu: (i, 0)),
            scratch_shapes=[pltpu.VMEM((2, rb, d), F32), pltpu.SemaphoreType.DMA((2,))]),
        compiler_params=_cp("arbitrary"),
        name="moe_gather",
    )(row_tok, n_used, h_all)


def _moe_up_kernel(be_ref, first_ref, nu_ref, x_ref, w_ref, b_ref, o_ref, wbf, *, tn):
    blk = pl.program_id(1)

    @pl.when(blk < nu_ref[0])
    def _():
        @pl.when(first_ref[blk] == 1)
        def _():
            wbf[...] = w_ref[...].astype(BF16)

        x = x_ref[...]
        pick_even = (lax.broadcasted_iota(jnp.int32, (2 * LANE, LANE), 0)
                     == 2 * lax.broadcasted_iota(jnp.int32, (2 * LANE, LANE), 1)).astype(BF16)
        for c in range(tn // (2 * LANE)):
            cols = slice(c * 2 * LANE, (c + 1) * 2 * LANE)
            hu = _dot(x, wbf[:, cols]) + b_ref[:, cols]
            lin = pltpu.roll(hu, 2 * LANE - 1, 1)
            glu = jnp.minimum(hu, SWIGLU_LIMIT)
            linc = jnp.clip(lin, -SWIGLU_LIMIT, SWIGLU_LIMIT)
            act = ((glu / (1.0 + jnp.exp(-SWIGLU_ALPHA * glu))) * (linc + 1.0)).astype(BF16)
            o_ref[:, c * LANE:(c + 1) * LANE] = _dot(act, pick_even).astype(BF16)

    @pl.when(blk >= nu_ref[0])
    def _():
        o_ref[...] = jnp.zeros_like(o_ref)


def _moe_down_kernel(be_ref, first_ref, nu_ref, x_ref, w_ref, b_ref, o_ref, wbf):
    blk = pl.program_id(1)

    @pl.when(blk < nu_ref[0])
    def _():
        @pl.when(first_ref[blk] == 1)
        def _():
            wbf[...] = w_ref[...].astype(BF16)

        o_ref[...] = _dot(x_ref[...], wbf[...]) + b_ref[...]

    @pl.when(blk >= nu_ref[0])
    def _():
        o_ref[...] = jnp.zeros_like(o_ref)


def _moe_grouped(kern, x, w, bias, li, blk_e, first, n_used, tn, out_cols, out_tn, out_dtype, name):
    rows, kdim = x.shape
    n = w.shape[-1]
    n_blk = rows // MOE_TM
    b4 = bias.reshape(bias.shape[0], bias.shape[1], 1, n)

    def bc(blk, nu):
        return jnp.minimum(blk, nu[0] - 1)

    return pl.pallas_call(
        kern,
        out_shape=jax.ShapeDtypeStruct((rows, out_cols), out_dtype),
        grid_spec=pltpu.PrefetchScalarGridSpec(
            num_scalar_prefetch=3,
            grid=(n // tn, n_blk),
            in_specs=[pl.BlockSpec((MOE_TM, kdim), lambda j, blk, be, fi, nu: (bc(blk, nu), 0)),
                      pl.BlockSpec((None, None, kdim, tn), lambda j, blk, be, fi, nu: (li, be[bc(blk, nu)], 0, j)),
                      pl.BlockSpec((None, None, 1, tn), lambda j, blk, be, fi, nu: (li, be[bc(blk, nu)], 0, j))],
            out_specs=pl.BlockSpec((MOE_TM, out_tn), lambda j, blk, be, fi, nu: (blk, j)),
            scratch_shapes=[pltpu.VMEM((kdim, tn), BF16)]),
        compiler_params=_cp("arbitrary", "arbitrary"),
        name=name,
    )(blk_e, first, n_used, x, w, b4)


def _moe_combine_kernel(pos_ref, tv_ref, yb_hbm, o_ref, buf, sem, *, tt):
    i = pl.program_id(0)
    n_steps = pl.num_programs(0)
    rows_per_iter = DMA_UNROLL // TOP_K if DMA_UNROLL >= TOP_K else 1

    def row_copy(step, r, k):
        slot = step % 2
        return pltpu.make_async_copy(yb_hbm.at[pl.ds(pos_ref[(step * tt + r) * TOP_K + k], 1)],
                                     buf.at[slot, k, pl.ds(r, 1)], sem.at[slot])

    def start_tile(step):
        def body(rr, carry):
            for u in range(rows_per_iter):
                for k in range(TOP_K):
                    row_copy(step, rr * rows_per_iter + u, k).start(priority=k % 2)
            return carry

        lax.fori_loop(0, tt // rows_per_iter, body, 0)

    def wait_tile(step):
        def body(rr, carry):
            for u in range(rows_per_iter):
                for k in range(TOP_K):
                    row_copy(step, rr * rows_per_iter + u, k).wait()
            return carry

        lax.fori_loop(0, tt // rows_per_iter, body, 0)

    @pl.when(i == 0)
    def _():
        start_tile(0)

    @pl.when(i + 1 < n_steps)
    def _():
        start_tile(i + 1)

    wait_tile(i)
    slot = i % 2
    tv = tv_ref[...]
    e = jnp.exp(tv - jnp.max(tv, axis=-1, keepdims=True))
    gate = e / jnp.sum(e, axis=-1, keepdims=True)
    acc = gate[:, 0:1] * buf[slot, 0]
    for k in range(1, TOP_K):
        acc = acc + gate[:, k:k + 1] * buf[slot, k]
    o_ref[...] = acc


def moe_combine(yb, pos_flat, top_v):
    n = top_v.shape[0]
    d = yb.shape[1]
    tt = MOE_COMBINE_ROWS
    return pl.pallas_call(
        functools.partial(_moe_combine_kernel, tt=tt),
        out_shape=jax.ShapeDtypeStruct((n, d), F32),
        grid_spec=pltpu.PrefetchScalarGridSpec(
            num_scalar_prefetch=1,
            grid=(n // tt,),
            in_specs=[pl.BlockSpec((tt, TOP_K), lambda i, pos: (i, 0)),
                      pl.BlockSpec(memory_space=pl.ANY)],
            out_specs=pl.BlockSpec((tt, d), lambda i, pos: (i, 0)),
            scratch_shapes=[pltpu.VMEM((2, TOP_K, tt, d), F32), pltpu.SemaphoreType.DMA((2,))]),
        compiler_params=_cp("arbitrary"),
        name="moe_combine",
    )(pos_flat, top_v, yb)


def moe(h_all, logits, ntok, li, w_up, b_up, w_down, b_down):
    npad, d = h_all.shape
    tm = MOE_TM
    top_v, top_e = lax.top_k(logits[:ntok], TOP_K)
    n_asg = ntok * TOP_K
    n_blk = (n_asg + N_EXPERTS * (tm - 1) + tm - 1) // tm
    rows = n_blk * tm
    flat_e = top_e.reshape(n_asg)
    onehot = (jnp.arange(N_EXPERTS, dtype=jnp.int32)[:, None] == flat_e[None, :]).astype(jnp.int32)
    prefix = jnp.cumsum(onehot, axis=1)
    counts = prefix[:, -1]
    pcounts = (counts + tm - 1) // tm * tm
    pend = jnp.cumsum(pcounts)
    pstart = pend - pcounts
    dest = jnp.sum(onehot * (prefix - 1 + pstart[:, None]), axis=0).astype(jnp.int32)
    row_tok = jnp.zeros((rows,), jnp.int32).at[dest].set(jnp.arange(n_asg, dtype=jnp.int32) // TOP_K)
    pos = jnp.zeros((npad * TOP_K,), jnp.int32).at[:n_asg].set(dest)
    blk_e = jnp.minimum(jnp.sum((pend[None, :] <= (jnp.arange(n_blk) * tm)[:, None]).astype(jnp.int32), axis=1),
                        N_EXPERTS - 1).astype(jnp.int32)
    first = jnp.concatenate([jnp.ones((1,), jnp.int32), (blk_e[1:] != blk_e[:-1]).astype(jnp.int32)])
    n_used = (pend[-1] // tm).astype(jnp.int32).reshape(1)
    tv_pad = jnp.zeros((npad, TOP_K), F32).at[:ntok].set(top_v)

    xs = moe_gather(h_all, row_tok, n_used)
    tn_up = _pick(2 * D_EXPERT, 1024, 512, 256)
    act = _moe_grouped(functools.partial(_moe_up_kernel, tn=tn_up), xs, w_up, b_up, li, blk_e, first, n_used,
                       tn_up, D_EXPERT, tn_up // 2, BF16, "moe_up")
    tn_dn = _pick(d, 2048, 1024, 512, 256)
    yb = _moe_grouped(_moe_down_kernel, act, w_down, b_down, li, blk_e, first, n_used,
                      tn_dn, d, tn_dn, F32, "moe_down")
    return moe_combine(yb, pos, tv_pad)


def _t5_bucket(dist):
    n = jnp.maximum(dist, 0)
    exact = REL_BUCKETS // 2
    nf = jnp.maximum(n, 1).astype(F32)
    large = exact + (jnp.log(nf / exact) / math.log(REL_MAX_DIST / exact) * (REL_BUCKETS - exact)).astype(jnp.int32)
    large = jnp.minimum(large, REL_BUCKETS - 1)
    return jnp.where(n < exact, n, large)


def _bias_of_dist(dist, table):
    onehot = (_t5_bucket(dist)[None] == jnp.arange(REL_BUCKETS).reshape((REL_BUCKETS,) + (1,) * dist.ndim)).astype(F32)
    return jnp.tensordot(table.astype(F32).T, onehot, axes=((1,), (0,)), precision=HI)


def _cmp_geometry(tk):
    ncb = (tk - CMP_LEN) // CMP_STRIDE + 1
    nch = -(-(ncb + CMP_LEN // CMP_STRIDE - 1) // (CMP_PAGES * SUBLANE)) * (CMP_PAGES * SUBLANE)
    nsb = -(-tk // SEL_BLK)
    nsbp = -(-nsb // LANE) * LANE
    return ncb, nch, nsb, nsbp


def _cmpa_kernel(*refs, npage):
    page_refs, w_ref, o_ref, pg = refs[1:npage + 1], refs[npage + 1], refs[npage + 2], refs[npage + 3]
    cpp = PAGE_SIZE // CMP_STRIDE
    m_rows = NSA_KV_HEADS * npage * cpp
    blk = 2 * HEAD_DIM
    per = CMP_LEN // CMP_STRIDE
    for p in range(npage):
        for slot in range(2):
            for g in range(NSA_KV_HEADS):
                pg[p, slot * NSA_KV_HEADS + g] = page_refs[p][:, slot, g, :]
    for slot in range(2):
        accs = [jnp.zeros((m_rows, CMP_HIDDEN), F32) for _ in range(per)]
        for s2 in range(CMP_STRIDE // 2):
            parts = []
            for g in range(NSA_KV_HEADS):
                for p in range(npage):
                    c = slot * NSA_KV_HEADS + g
                    x0 = pg[p, c, pl.ds(2 * s2, cpp, stride=CMP_STRIDE), :]
                    x1 = pg[p, c, pl.ds(2 * s2 + 1, cpp, stride=CMP_STRIDE), :]
                    parts.append(jnp.concatenate([x0, x1], axis=1))
            xm = jnp.concatenate(parts, axis=0).astype(BF16)
            for m in range(per):
                w = w_ref[slot, pl.ds(m * CMP_STRIDE * HEAD_DIM + s2 * blk, blk), :].astype(BF16)
                accs[m] = accs[m] + _dot(xm, w)
        o_ref[slot] = jnp.concatenate(accs, axis=1).reshape(NSA_KV_HEADS, npage * cpp, per * CMP_HIDDEN)


def compress_chunks(src6, page_rows, li, w1, ni, nb, nch):
    npg = page_rows.shape[1]
    cpp = PAGE_SIZE // CMP_STRIDE
    steps = nch // (CMP_PAGES * cpp)
    per = CMP_LEN // CMP_STRIDE
    idx = jnp.minimum(jnp.arange(steps * CMP_PAGES), npg - 1)
    pr = page_rows[:, idx].reshape(-1).astype(jnp.int32)

    def page_spec(p):
        return pl.BlockSpec((None, None, PAGE_SIZE, 2, NSA_KV_HEADS, HEAD_DIM),
                            lambda b, s, pr_ref: (pr_ref[(b * steps + s) * CMP_PAGES + p], li, 0, 0, 0, 0))

    return pl.pallas_call(
        functools.partial(_cmpa_kernel, npage=CMP_PAGES),
        out_shape=jax.ShapeDtypeStruct((2, nb, NSA_KV_HEADS, nch, per * CMP_HIDDEN), F32),
        grid_spec=pltpu.PrefetchScalarGridSpec(
            num_scalar_prefetch=1,
            grid=(nb, steps),
            in_specs=[page_spec(p) for p in range(CMP_PAGES)]
            + [pl.BlockSpec((None, 2, CMP_LEN * HEAD_DIM, CMP_HIDDEN), lambda b, s, pr_ref: (ni, 0, 0, 0))],
            out_specs=pl.BlockSpec((2, None, NSA_KV_HEADS, CMP_PAGES * cpp, per * CMP_HIDDEN),
                                   lambda b, s, pr_ref: (0, b, 0, s, 0)),
            scratch_shapes=[pltpu.VMEM((CMP_PAGES, 2 * NSA_KV_HEADS, PAGE_SIZE, HEAD_DIM), F32)]),
        compiler_params=_cp("arbitrary", "arbitrary"),
        name="compress_chunks",
    )(pr, *([src6] * CMP_PAGES), w1)


def _cmpb_kernel(a_ref, pe_ref, w1_ref, b1_ref, w2_ref, b2_ref, o_ref):
    nch = a_ref.shape[0]
    pew = _dot(pe_ref[...].astype(BF16), w1_ref[...].astype(BF16))[0:1]
    a = a_ref[...]
    hid = b1_ref[...] + pew
    hid = hid + a[:, :CMP_HIDDEN]
    hid = hid + pltpu.roll(a[:, CMP_HIDDEN:], nch - 1, 0)
    act = jax.nn.gelu(hid, approximate=True)
    o_ref[...] = _dot(act.astype(BF16), w2_ref[...].astype(BF16)) + b2_ref[...]


def compress_blocks(a, pe, w1, b1, w2, b2, ni):
    _, nb, g, nch, _ = a.shape
    assert CMP_LEN // CMP_STRIDE == 2
    pe8 = jnp.broadcast_to(pe.reshape(pe.shape[0], 2, 1, CMP_LEN * HEAD_DIM), (pe.shape[0], 2, SUBLANE, CMP_LEN * HEAD_DIM))
    return pl.pallas_call(
        _cmpb_kernel,
        out_shape=jax.ShapeDtypeStruct((2, nb, nch, g * HEAD_DIM), F32),
        grid=(2, nb, g),
        in_specs=[pl.BlockSpec((None, None, None, nch, 2 * CMP_HIDDEN), lambda s, b, gi: (s, b, gi, 0, 0)),
                  pl.BlockSpec((None, None, SUBLANE, CMP_LEN * HEAD_DIM), lambda s, b, gi: (ni, s, 0, 0)),
                  pl.BlockSpec((None, None, CMP_LEN * HEAD_DIM, CMP_HIDDEN), lambda s, b, gi: (ni, s, 0, 0)),
                  pl.BlockSpec((None, None, 1, CMP_HIDDEN), lambda s, b, gi: (ni, s, 0, 0)),
                  pl.BlockSpec((None, None, CMP_HIDDEN, HEAD_DIM), lambda s, b, gi: (ni, s, 0, 0)),
                  pl.BlockSpec((None, None, 1, HEAD_DIM), lambda s, b, gi: (ni, s, 0, 0))],
        out_specs=pl.BlockSpec((None, None, nch, HEAD_DIM), lambda s, b, gi: (s, b, 0, gi)),
        compiler_params=_cp("arbitrary", "arbitrary", "arbitrary"),
        name="compress_blocks",
    )(a, pe8, w1, b1.reshape(b1.shape[0], 2, 1, CMP_HIDDEN), w2, b2.reshape(b2.shape[0], 2, 1, HEAD_DIM))


def _cmp_attn_kernel(q_ref, kc_ref, vc_ref, bias_ref, cov_ref, o_ref, sc_ref):
    kc = kc_ref[...].astype(BF16)
    vc = vc_ref[...].astype(BF16)
    scale = HEAD_DIM ** -0.5
    tq = q_ref.shape[0]
    nch = kc.shape[0]
    qall = jnp.concatenate([q_ref[:, r * HEAD_DIM:(r + 1) * HEAD_DIM] for r in range(NSA_GROUP)], axis=0).astype(BF16)
    lg = _dot_nt(qall, kc) * scale + bias_ref[...].reshape(NSA_GROUP * tq, nch)
    mx = jnp.max(lg, axis=-1, keepdims=True)
    e = jnp.exp(lg - mx)
    p = e / jnp.sum(e, axis=-1, keepdims=True) * (mx > 0.1 * NEG_INF).astype(F32)
    o = _dot(p.astype(BF16), vc)
    for r in range(NSA_GROUP):
        o_ref[:, r * HEAD_DIM:(r + 1) * HEAD_DIM] = o[r * tq:(r + 1) * tq]
    p_heads = jnp.sum(p.reshape(NSA_GROUP, tq, nch), axis=0)
    sc_ref[...] = jnp.dot(p_heads, cov_ref[...], precision=HI, preferred_element_type=F32)


def cmp_attention(z3, kvc, bias, cover, tq):
    b, t, _ = z3.shape
    nch = kvc.shape[2]
    nsbp = cover.shape[1]
    gw = NSA_GROUP * HEAD_DIM
    return pl.pallas_call(
        _cmp_attn_kernel,
        out_shape=(jax.ShapeDtypeStruct((b, t, NSA_HEADS * HEAD_DIM), F32),
                   jax.ShapeDtypeStruct((b, NSA_KV_HEADS, t, nsbp), F32)),
        grid=(b, NSA_KV_HEADS, t // tq),
        in_specs=[pl.BlockSpec((None, tq, gw), lambda bi, g, i: (bi, i, g)),
                  pl.BlockSpec((None, None, nch, HEAD_DIM), lambda bi, g, i: (0, bi, 0, g)),
                  pl.BlockSpec((None, None, nch, HEAD_DIM), lambda bi, g, i: (1, bi, 0, g)),
                  pl.BlockSpec((NSA_GROUP, tq, nch), lambda bi, g, i: (g, i, 0)),
                  pl.BlockSpec((nch, nsbp), lambda bi, g, i: (0, 0))],
        out_specs=(pl.BlockSpec((None, tq, gw), lambda bi, g, i: (bi, i, g)),
                   pl.BlockSpec((None, None, tq, nsbp), lambda bi, g, i: (bi, g, i, 0))),
        compiler_params=_cp("arbitrary", "arbitrary", "arbitrary"),
        name="cmp_attention",
    )(z3, kvc, kvc, bias, cover)


def cmp_tables(q_pos, tk, nch, nsbp, table):
    ncb = (tk - CMP_LEN) // CMP_STRIDE + 1
    nsb = -(-tk // SEL_BLK)
    n = jnp.arange(nch)
    dist = q_pos[:, None] - (n * CMP_STRIDE + CMP_LEN - 1)[None, :]
    vis = (dist >= 0) & (n < ncb)[None, :]
    bias = jnp.where(vis[None], _bias_of_dist(dist, table), NEG_INF)
    j = jnp.arange(nsbp)[None, :]
    i = n[:, None]
    cover = ((i * CMP_STRIDE < (j + 1) * SEL_BLK) & (i * CMP_STRIDE + CMP_LEN > j * SEL_BLK)
             & (i < ncb) & (j < nsb)).astype(F32)
    return bias, cover


def select_blocks(score, q_pos, nsb):
    j = jnp.arange(nsb)[None, :]
    cur = (q_pos // SEL_BLK)[:, None]
    forced = (j == 0) | (j == cur) | (j == cur - 1)
    s = score[..., :nsb]
    s = jnp.where((j > cur)[None, None], NEG_SCORE, s + jnp.where(forced, FORCE_BONUS, 0.0)[None, None])
    _, idx = lax.top_k(s, min(SEL_TOPK, nsb))
    return idx


def _nsa_attn_kernel(q_ref, ks_ref, vs_ref, kw_ref, vw_ref, msk_ref, tb_ref, oc_ref, gt_ref, o_ref):
    i = pl.program_id(2)
    tq = QBLK
    rows = NSA_GROUP * tq
    scale = HEAD_DIM ** -0.5
    qall = jnp.concatenate([q_ref[:, r * HEAD_DIM:(r + 1) * HEAD_DIM] for r in range(NSA_GROUP)], axis=0).astype(BF16)
    qpos = i * tq + lax.broadcasted_iota(jnp.int32, (tq, tq), 0)
    kcol = lax.broadcasted_iota(jnp.int32, (tq, tq), 1)
    selm = msk_ref[...]
    nsbp = selm.shape[1]
    per_blk = tq // SEL_BLK
    jrow = lax.broadcasted_iota(jnp.int32, (nsbp, tq), 0)
    jcol = lax.broadcasted_iota(jnp.int32, (nsbp, tq), 1) // SEL_BLK

    def sel_valid(m):
        expand = (jrow == per_blk * m + jcol).astype(F32)
        picked = _dot(selm, expand)
        return (picked > 0.5) & (m * tq + kcol <= qpos)

    def win_valid(m):
        dist = qpos - (m * tq + kcol)
        return (dist >= 0) & (dist < WINDOW)

    heads_per_pass = 2 if NSA_GROUP % 2 == 0 else 1
    pass_rows = heads_per_pass * tq

    def branch(k_ref, v_ref, lo, valid_fn):
        def body(m, carry):
            mx, l, acc = carry
            r0 = pl.multiple_of(m * tq, tq)
            k = k_ref[pl.ds(r0, tq), :].astype(BF16)
            v = v_ref[pl.ds(r0, tq), :].astype(BF16)
            valid = valid_fn(m)[None]
            d = jnp.minimum(i - m, 2)
            mx_out, l_out, acc_out = [], [], []
            for hp in range(NSA_GROUP // heads_per_pass):
                rs = slice(hp * pass_rows, (hp + 1) * pass_rows)
                s = _dot_nt(qall[rs], k) * scale
                s = s.reshape(heads_per_pass, tq, tq) + tb_ref[d, hp * heads_per_pass:(hp + 1) * heads_per_pass]
                s = jnp.where(valid, s, NEG_INF).reshape(pass_rows, tq)
                mx_new = jnp.maximum(mx[rs], jnp.max(s, axis=-1, keepdims=True))
                alpha = jnp.exp(mx[rs] - mx_new)
                p = jnp.exp(s - mx_new)
                mx_out.append(mx_new)
                l_out.append(alpha * l[rs] + jnp.sum(p, axis=-1, keepdims=True))
                acc_out.append(alpha * acc[rs] + _dot(p.astype(BF16), v))
            return jnp.concatenate(mx_out, axis=0), jnp.concatenate(l_out, axis=0), jnp.concatenate(acc_out, axis=0)

        init = (jnp.full((rows, 1), NEG_INF, F32), jnp.zeros((rows, 1), F32), jnp.zeros((rows, HEAD_DIM), F32))
        _, l, acc = lax.fori_loop(lo, i + 1, body, init)
        return acc / l

    o_sel = branch(ks_ref, vs_ref, 0, sel_valid)
    o_win = branch(kw_ref, vw_ref, jnp.maximum(i - WINDOW // tq, 0), win_valid)
    gt = gt_ref[...]
    gates = 1.0 / (1.0 + jnp.exp(-gt))
    for r in range(NSA_GROUP):
        sl = slice(r * HEAD_DIM, (r + 1) * HEAD_DIM)
        o = (gates[:, 3 * r:3 * r + 1] * oc_ref[:, sl]
             + gates[:, 3 * r + 1:3 * r + 2] * o_sel[r * tq:(r + 1) * tq]
             + gates[:, 3 * r + 2:3 * r + 3] * o_win[r * tq:(r + 1) * tq])
        o_ref[:, sl] = o.astype(BF16)


def nsa_attention_prompt(z3, sel_mask, tb, o_cmp, gate_lin):
    b, t, _ = z3.shape
    gw = NSA_GROUP * HEAD_DIM
    nsbp = sel_mask.shape[-1]
    kv0 = NSA_HEADS * HEAD_DIM // HEAD_DIM
    g4 = NSA_KV_HEADS

    def kv_spec(slot):
        return pl.BlockSpec((None, t, HEAD_DIM), lambda bi, g, i: (bi, 0, kv0 + slot * g4 + g))

    return pl.pallas_call(
        _nsa_attn_kernel,
        out_shape=jax.ShapeDtypeStruct((b, t, NSA_HEADS * HEAD_DIM), BF16),
        grid=(b, NSA_KV_HEADS, t // QBLK),
        in_specs=[pl.BlockSpec((None, QBLK, gw), lambda bi, g, i: (bi, i, g)),
                  kv_spec(2), kv_spec(3), kv_spec(4), kv_spec(5),
                  pl.BlockSpec((None, None, QBLK, nsbp), lambda bi, g, i: (bi, g, i, 0)),
                  pl.BlockSpec((3, NSA_GROUP, QBLK, QBLK), lambda bi, g, i: (0, g, 0, 0)),
                  pl.BlockSpec((None, QBLK, gw), lambda bi, g, i: (bi, i, g)),
                  pl.BlockSpec((None, None, QBLK, 3 * NSA_GROUP), lambda bi, g, i: (bi, g, i, 0))],
        out_specs=pl.BlockSpec((None, QBLK, gw), lambda bi, g, i: (bi, i, g)),
        compiler_params=_cp("arbitrary", "arbitrary", "arbitrary"),
        name="nsa_attention_prompt",
    )(z3, z3, z3, z3, z3, sel_mask, tb, o_cmp, gate_lin)


def _sel_sample_kernel(rb_ref, js_ref, q_ref, k_ref, v_ref, kn_ref, vn_ref, b_ref, o_ref, m_s, l_s, a_s, kv_s, *, jlast, ksel):
    bi, g, kk = pl.program_id(0), pl.program_id(1), pl.program_id(2)
    j = js_ref[(bi * NSA_KV_HEADS + g) * ksel + kk]
    scale = HEAD_DIM ** -0.5

    @pl.when(kk == 0)
    def _():
        m_s[...] = jnp.full_like(m_s, NEG_INF)
        l_s[...] = jnp.zeros_like(l_s)
        a_s[...] = jnp.zeros_like(a_s)

    for gg in range(NSA_KV_HEADS):
        @pl.when(g == gg)
        def _():
            kv_s[0] = k_ref[:, gg, :]
            kv_s[1] = v_ref[:, gg, :]

    is_new = j == jlast
    first_row = lax.broadcasted_iota(jnp.int32, (SEL_BLK, HEAD_DIM), 0) == 0
    k = jnp.where(is_new, jnp.where(first_row, kn_ref[...], 0.0), kv_s[0])
    v = jnp.where(is_new, jnp.where(first_row, vn_ref[...], 0.0), kv_s[1])
    s = _dot_nt(q_ref[...].astype(BF16), k.astype(BF16)) * scale + b_ref[...]
    valid = jnp.logical_or(jnp.logical_not(is_new), lax.broadcasted_iota(jnp.int32, s.shape, 1) == 0)
    s = jnp.where(valid, s, NEG_INF)
    mx = m_s[...]
    mx_new = jnp.maximum(mx, jnp.max(s, axis=-1, keepdims=True))
    alpha = jnp.exp(mx - mx_new)
    p = jnp.exp(s - mx_new)
    l_new = alpha * l_s[...] + jnp.sum(p, axis=-1, keepdims=True)
    a_new = alpha * a_s[...] + _dot(p.astype(BF16), v.astype(BF16))
    m_s[...] = mx_new
    l_s[...] = l_new
    a_s[...] = a_new

    @pl.when(kk == ksel - 1)
    def _():
        o_ref[...] = a_new / l_new


def sel_attention_sample(q4, cache, row_blk, jsel, k_new, v_new, bias, ni, jlast):
    db, g, r, hd = q4.shape
    ksel = jsel.shape[-1]
    per_page = PAGE_SIZE // SEL_BLK

    def blk_spec(slot):
        def index(bi, gi, kk, rb, js):
            blk = rb[(bi * g + gi) * ksel + kk]
            return (blk // per_page, ni, blk % per_page, slot, 0, 0)

        return pl.BlockSpec((None, None, SEL_BLK, None, g, HEAD_DIM), index)

    return pl.pallas_call(
        functools.partial(_sel_sample_kernel, jlast=jlast, ksel=ksel),
        out_shape=jax.ShapeDtypeStruct((db, g, r, hd), F32),
        grid_spec=pltpu.PrefetchScalarGridSpec(
            num_scalar_prefetch=2,
            grid=(db, g, ksel),
            in_specs=[pl.BlockSpec((None, None, r, hd), lambda bi, gi, kk, rb, js: (bi, gi, 0, 0)),
                      blk_spec(2), blk_spec(3),
                      pl.BlockSpec((None, None, 1, hd), lambda bi, gi, kk, rb, js: (bi, gi, 0, 0)),
                      pl.BlockSpec((None, None, 1, hd), lambda bi, gi, kk, rb, js: (bi, gi, 0, 0)),
                      pl.BlockSpec((None, None, r, SEL_BLK),
                                   lambda bi, gi, kk, rb, js: (gi, js[(bi * g + gi) * ksel + kk], 0, 0))],
            out_specs=pl.BlockSpec((None, None, r, hd), lambda bi, gi, kk, rb, js: (bi, gi, 0, 0)),
            scratch_shapes=[pltpu.VMEM((r, 1), F32), pltpu.VMEM((r, 1), F32), pltpu.VMEM((r, hd), F32),
                            pltpu.VMEM((2, SEL_BLK, hd), F32)]),
        compiler_params=_cp("arbitrary", "arbitrary", "arbitrary"),
        name="sel_attention_sample",
    )(row_blk.reshape(-1), jsel.reshape(-1), q4, cache, cache, k_new, v_new, bias)


def _win_sample_kernel(q_ref, kb_ref, vb_ref, kn_ref, vn_ref, bw_ref, b0_ref, oc_ref, os_ref, gt_ref, o_ref):
    scale = HEAD_DIM ** -0.5
    q = q_ref[...]
    wb = kb_ref.shape[0]
    s = _dot_nt(q.astype(BF16), kb_ref[...].astype(BF16)) * scale + bw_ref[...]
    dist = wb - lax.broadcasted_iota(jnp.int32, s.shape, 1)
    s = jnp.where(dist < WINDOW, s, NEG_INF)
    s_new = jnp.sum(q * kn_ref[...], axis=-1, keepdims=True) * scale + b0_ref[...][:, 0:1]
    mx = jnp.maximum(jnp.max(s, axis=-1, keepdims=True), s_new)
    e = jnp.exp(s - mx)
    e_new = jnp.exp(s_new - mx)
    l = jnp.sum(e, axis=-1, keepdims=True) + e_new
    o_win = (_dot(e.astype(BF16), vb_ref[...].astype(BF16)) + e_new * vn_ref[...]) / l
    gates = 1.0 / (1.0 + jnp.exp(-gt_ref[...]))
    o_ref[...] = gates[:, 0:1] * oc_ref[...] + gates[:, 1:2] * os_ref[...] + gates[:, 2:3] * o_win


def win_attention_sample(q4, win3, k_new, v_new, bias_w, bias0, o_cmp4, o_sel4, gate4, ni):
    db, g, r, hd = q4.shape
    wb = win3.shape[1]

    def small(shape_last):
        return pl.BlockSpec((None, None, r, shape_last), lambda bi, gi: (bi, gi, 0, 0))

    return pl.pallas_call(
        _win_sample_kernel,
        out_shape=jax.ShapeDtypeStruct((db, g, r, hd), F32),
        grid=(db, g),
        in_specs=[small(hd),
                  pl.BlockSpec((None, wb, hd), lambda bi, gi: (ni * db + bi, 0, gi)),
                  pl.BlockSpec((None, wb, hd), lambda bi, gi: (ni * db + bi, 0, g + gi)),
                  pl.BlockSpec((None, None, 1, hd), lambda bi, gi: (bi, gi, 0, 0)),
                  pl.BlockSpec((None, None, 1, hd), lambda bi, gi: (bi, gi, 0, 0)),
                  pl.BlockSpec((None, r, wb), lambda bi, gi: (gi, 0, 0)),
                  pl.BlockSpec((None, r, LANE), lambda bi, gi: (gi, 0, 0)),
                  small(hd), small(hd), small(3)],
        out_specs=small(hd),
        compiler_params=_cp("arbitrary", "arbitrary"),
        name="win_attention_sample",
    )(q4, win3, win3, k_new, v_new, bias_w, bias0, o_cmp4, o_sel4, gate4)


def nsa_layer(h_all, bt, b, t, db, ni, cache_nsa, state_nsa_win, page_table, rel_bias,
              nsa_w_in, nsa_cmp_pe, nsa_cmp_w1, nsa_cmp_b1, nsa_cmp_w2, nsa_cmp_b2):
    nq = NSA_HEADS * HEAD_DIM
    nqkv = nq + 6 * NSA_KV
    z_p, z_s = dense_prompt_and_sample(h_all, h_all, nsa_w_in, (ni,), bt, ncols=nqkv)
    w_gate = jnp.zeros((h_all.shape[1], LANE), F32).at[:, :3 * NSA_HEADS].set(nsa_w_in[ni, :, nqkv:])
    zg_p, zg_s = dense_prompt_and_sample(h_all, h_all, w_gate, (), bt)
    wz = z_p.shape[1]
    z3 = z_p.reshape(b, t, wz)
    n_layers = cache_nsa.shape[1]
    past_len = page_table.shape[1] * PAGE_SIZE
    pos_p = jnp.arange(t)

    ncb, nch, nsb, nsbp = _cmp_geometry(t)
    page_rows = (jnp.arange(b)[:, None] * (t // PAGE_SIZE) + jnp.arange(t // PAGE_SIZE)[None, :])
    rows_p = z3[:, :, nq:nq + 4 * NSA_KV].reshape(b, t, 4, NSA_KV_HEADS, HEAD_DIM)
    src_p = rows_p.reshape(bt // PAGE_SIZE, 1, PAGE_SIZE, 4, NSA_KV_HEADS, HEAD_DIM)
    a_p = compress_chunks(src_p, page_rows, 0, nsa_cmp_w1, ni, b, nch)
    kvc_p = compress_blocks(a_p, nsa_cmp_pe, nsa_cmp_w1, nsa_cmp_b1, nsa_cmp_w2, nsa_cmp_b2, ni)
    bias_p, cover_p = cmp_tables(pos_p, t, nch, nsbp, rel_bias)
    o_cmp_p, score_p = cmp_attention(z3, kvc_p, bias_p, cover_p, QBLK)
    idx_p = select_blocks(score_p, pos_p, nsb)
    sel_mask = (idx_p[..., None] == jnp.arange(nsbp)).any(axis=-2).astype(F32)
    ii = jnp.arange(QBLK)
    tb = _bias_of_dist(jnp.arange(3)[:, None, None] * QBLK + ii[None, :, None] - ii[None, None, :], rel_bias)
    tb = jnp.moveaxis(tb, 0, 1)
    gate_lin = zg_p[:, :3 * NSA_HEADS].reshape(b, t, NSA_KV_HEADS, 3 * NSA_GROUP).transpose((0, 2, 1, 3))
    o_p = nsa_attention_prompt(z3, sel_mask, tb, o_cmp_p, gate_lin)
    keep = min(WINDOW, t)
    win_p = z3[:, t - keep:, nq + 4 * NSA_KV:nq + 6 * NSA_KV].reshape(b, keep, 2, NSA_KV_HEADS, HEAD_DIM)

    zs = z_s[:db]
    tk = past_len + DEC_SEQ
    ncb_s, nch_s, nsb_s, nsbp_s = _cmp_geometry(tk)
    assert (ncb_s + 1) * CMP_STRIDE <= past_len
    a_s = compress_chunks(cache_nsa, page_table, ni, nsa_cmp_w1, ni, db, nch_s)
    kvc_s = compress_blocks(a_s, nsa_cmp_pe, nsa_cmp_w1, nsa_cmp_b1, nsa_cmp_w2, nsa_cmp_b2, ni)
    pos_s = jnp.full((SUBLANE,), past_len, jnp.int32)
    bias_s, cover_s = cmp_tables(pos_s, tk, nch_s, nsbp_s, rel_bias)
    zs3 = jnp.zeros((db, SUBLANE, wz), F32).at[:, 0].set(zs)
    o_cmp_s, score_s = cmp_attention(zs3, kvc_s, bias_s, cover_s, SUBLANE)
    idx_s = select_blocks(score_s[:, :, 0:1], pos_s[0:1], nsb_s)[:, :, 0]
    jlast = past_len // SEL_BLK
    per_page = PAGE_SIZE // SEL_BLK
    page_of = jnp.take_along_axis(page_table, jnp.minimum(idx_s, jlast - 1).reshape(db, -1) // per_page, axis=1)
    row_blk = (page_of.reshape(idx_s.shape) * per_page + jnp.minimum(idx_s, jlast - 1) % per_page).astype(jnp.int32)
    q4 = zs[:, :nq].reshape(db, NSA_KV_HEADS, NSA_GROUP, HEAD_DIM)
    kv_new = zs[:, nq:nq + 6 * NSA_KV].reshape(db, 6, NSA_KV_HEADS, 1, HEAD_DIM)
    kpos = jnp.arange(nsb_s * SEL_BLK).reshape(nsb_s, SEL_BLK)
    bias_sel = _bias_of_dist(past_len - kpos, rel_bias).reshape(NSA_KV_HEADS, NSA_GROUP, nsb_s, SEL_BLK).transpose((0, 2, 1, 3))
    o_sel_s = sel_attention_sample(q4, cache_nsa, row_blk, idx_s.astype(jnp.int32), kv_new[:, 2], kv_new[:, 3],
                                   bias_sel, ni, jlast)
    wb = state_nsa_win.shape[2]
    win3 = state_nsa_win.reshape(state_nsa_win.shape[0] * db, wb, 2 * NSA_KV)
    bias_w = _bias_of_dist(wb - jnp.arange(wb), rel_bias).reshape(NSA_KV_HEADS, NSA_GROUP, wb)
    bias0 = jnp.broadcast_to(_bias_of_dist(jnp.zeros((1,), jnp.int32), rel_bias).reshape(NSA_KV_HEADS, NSA_GROUP, 1),
                             (NSA_KV_HEADS, NSA_GROUP, LANE))
    gate4 = zg_s[:db, :3 * NSA_HEADS].reshape(db, NSA_KV_HEADS, NSA_GROUP, 3)
    o_cmp4 = o_cmp_s[:, 0].reshape(db, NSA_KV_HEADS, NSA_GROUP, HEAD_DIM)
    o_s = win_attention_sample(q4, win3, kv_new[:, 4], kv_new[:, 5], bias_w, bias0, o_cmp4, o_sel_s, gate4, ni)
    rows_s = zs[:, nq:nq + 4 * NSA_KV].reshape(db, DEC_SEQ, 4, NSA_KV_HEADS, HEAD_DIM)
    new_win = zs[:, nq + 4 * NSA_KV:nq + 6 * NSA_KV].reshape(db, DEC_SEQ, 2, NSA_KV_HEADS, HEAD_DIM)
    win_all = jnp.concatenate([state_nsa_win[ni], new_win], axis=1)
    keep_s = min(WINDOW, wb + DEC_SEQ)
    win_s = win_all[:, wb + DEC_SEQ - keep_s:]
    return o_p.reshape(bt, nq), o_s.reshape(db, nq), rows_p, rows_s, win_p, win_s


def kernel(x_prompt, x_sample, c_prompt, c_sample, state_ret, cache_nsa, state_nsa_win, page_table, rel_bias,
           ada_w, ada_b, ln_g, ln_b, ret_w_in, ret_gn_w, ret_w_out, nsa_w_in, nsa_cmp_pe, nsa_cmp_w1, nsa_cmp_b1,
           nsa_cmp_w2, nsa_cmp_b2, nsa_w_out, moe_w_router, moe_b_router, moe_w_up, moe_b_up, moe_w_down, moe_b_down):
    b, t, d = x_prompt.shape
    db, ds, _ = x_sample.shape
    assert ds == DEC_SEQ == 1 and db <= SUBLANE and t % ROW_TILE == 0 and t % QBLK == 0
    bt = b * t
    npad = bt + ROW_TILE
    ntok = bt + db
    tiles_per_seq = t // ROW_TILE
    past_len = page_table.shape[1] * PAGE_SIZE

    x_all = jnp.zeros((npad, d), F32).at[:bt].set(x_prompt.reshape(bt, d)).at[bt:ntok].set(x_sample.reshape(db, d))
    rc = -(-(b + db) // SUBLANE) * SUBLANE
    c_all = jnp.zeros((rc, d), F32).at[:b].set(c_prompt).at[b:b + db].set(c_sample)
    mods = ada_all(c_all, ada_w, ada_b)

    def mod_slabs(li, sub):
        m = mods[li * 2 + sub]
        return tuple(_slabs(m[:, k * d:(k + 1) * d], b, db) for k in range(3))

    def tail_rows(a_s, width, dtype):
        return jnp.zeros((ROW_TILE, width), dtype).at[:db].set(a_s.astype(dtype))

    ret_sp, ret_ss, rows_p, rows_s, win_p, win_s = [], [], [], [], [], []
    for li in range(DEPTH):
        sh, sc, gt = mod_slabs(li, 0)
        h_all = modulate(x_all, sc, sh, tiles_per_seq, b)
        if li % N_MIXERS == 0:
            ri = li // N_MIXERS
            z_p, z_s = dense_prompt_and_sample(h_all, h_all, ret_w_in, (ri,), bt)
            a_p, sp = retention(z_p.reshape(b, t, RET_IN), jnp.arange(t), None, ret_gn_w, ri, min(RET_CHUNK, t) if t % RET_CHUNK == 0 else t)
            zs3 = jnp.zeros((db, SUBLANE, RET_IN), F32).at[:, 0].set(z_s[:db])
            a_s, ss = retention(zs3, jnp.full((SUBLANE,), past_len), state_ret, ret_gn_w, ri, DEC_SEQ)
            ret_sp.append(sp)
            ret_ss.append(ss)
            nv = RET_HEADS * RET_DV
            y_all = dense_prompt_and_sample(a_p.reshape(bt, nv), tail_rows(a_s[:, 0], nv, BF16), ret_w_out, (ri,), bt,
                                            into_rows=npad)
        else:
            ni = li // N_MIXERS
            o_p, o_s, rp, rs, wp, wsb = nsa_layer(h_all, bt, b, t, db, ni, cache_nsa, state_nsa_win, page_table, rel_bias,
                                                  nsa_w_in, nsa_cmp_pe, nsa_cmp_w1, nsa_cmp_b1, nsa_cmp_w2, nsa_cmp_b2)
            rows_p.append(rp)
            rows_s.append(rs)
            win_p.append(wp)
            win_s.append(wsb)
            nq = NSA_HEADS * HEAD_DIM
            y_all = dense_prompt_and_sample(o_p, tail_rows(o_s, nq, BF16), nsa_w_out, (ni,), bt, into_rows=npad)
        x_all = resid_ln(x_all, y_all, gt, ln_g, ln_b, li, 0, tiles_per_seq, b)

        sh, sc, gt = mod_slabs(li, 1)
        h_f32, logits = modulate_router(x_all, sc, sh, moe_w_router, moe_b_router, li, tiles_per_seq, b)
        y_all = moe(h_f32, logits, ntok, li, moe_w_up, moe_b_up, moe_w_down, moe_b_down)
        x_all = resid_ln(x_all, y_all, gt, ln_g, ln_b, li, 1, tiles_per_seq, b)

    y_prompt = x_all[:bt].reshape(b, t, d)
    y_sample = x_all[bt:ntok].reshape(db, ds, d)
    return (y_prompt, y_sample, jnp.stack(ret_sp), jnp.stack(ret_ss), jnp.stack(rows_p, axis=1),
            jnp.stack(rows_s, axis=1), jnp.stack(win_p), jnp.stack(win_s))
```

```python
import functools
import math

import jax
import jax.numpy as jnp
from jax import lax
from jax.experimental import pallas as pl
from jax.experimental.pallas import tpu as pltpu

D_MODEL = 4096
BATCH = 4
SEQ = 2048
DEPTH = 2
DEC_BATCH = 8
DEC_SEQ = 1
PAST_LEN = 16384
PAGE_SIZE = 128

N_MIXERS = 2
N_RET_LAYERS = (DEPTH + 1) // 2
N_NSA_LAYERS = DEPTH // 2

DN_ALPHA = (2.0 * DEPTH) ** 0.25
LN_EPS = 1e-5
NEG_INF = -1e30
NEG_SCORE = -1e9

RET_HEADS = 16
RET_DK = D_MODEL // RET_HEADS
RET_DV = 2 * RET_DK
RET_CHUNK = 128
ROPE_BASE = 10000.0

NSA_HEADS = 32
HEAD_DIM = D_MODEL // NSA_HEADS
NSA_KV_HEADS = 4
NSA_GROUP = NSA_HEADS // NSA_KV_HEADS
NSA_KV = NSA_KV_HEADS * HEAD_DIM
CMP_LEN = 32
CMP_STRIDE = 16
CMP_HIDDEN = 2 * HEAD_DIM
SEL_BLK = 64
SEL_TOPK = 16
WINDOW = 512
QBLK = 128
FORCE_BONUS = 1e6

REL_BUCKETS = 32
REL_MAX_DIST = 128

N_EXPERTS = 32
TOP_K = 4
D_EXPERT = D_MODEL // 2
SWIGLU_ALPHA = 1.702
SWIGLU_LIMIT = 7.0

RET_IN = 2 * RET_HEADS * RET_DK + 2 * RET_HEADS * RET_DV
NSA_IN = NSA_HEADS * HEAD_DIM + 6 * NSA_KV + 3 * NSA_HEADS

F32 = jnp.float32
BF16 = jnp.bfloat16
HI = lax.Precision.HIGHEST

LANE = 128
SUBLANE = 8
VMEM_LIMIT_BYTES = 56 * 1024 * 1024

ROW_TILE = 256
MOE_TM = 256
MOE_UP_CHUNK = 512
MOE_GATHER_ROWS = 256
MOE_COMBINE_ROWS = 128
CMP_PAGES = 8


def _cp(*sem, vmem=VMEM_LIMIT_BYTES):
    return pltpu.CompilerParams(dimension_semantics=sem, vmem_limit_bytes=vmem)


def _dot(a, b):
    return jnp.dot(a, b, preferred_element_type=F32)


def _dot_nt(a, b):
    return lax.dot_general(a, b, (((1,), (1,)), ((), ())), preferred_element_type=F32)


def _dot_tn(a, b):
    return lax.dot_general(a, b, (((0,), (0,)), ((), ())), preferred_element_type=F32)


def _pick(n, *cands):
    for c in cands:
        if n % c == 0:
            return c
    return n


def _ada_kernel(c_ref, w_ref, b_ref, o_ref):
    c = c_ref[...]
    s = (c / (1.0 + jnp.exp(-c))).astype(BF16)
    o_ref[...] = _dot(s, w_ref[...].astype(BF16)) + b_ref[...]


def ada_all(c_all, ada_w, ada_b):
    rc, d = c_all.shape
    nl = ada_w.shape[0] * ada_w.shape[1]
    w = ada_w.reshape(nl, d, 3 * d)
    b = ada_b.reshape(nl, 1, 3 * d)
    tn = _pick(3 * d, 512, 256, 128)
    return pl.pallas_call(
        _ada_kernel,
        out_shape=jax.ShapeDtypeStruct((nl, rc, 3 * d), F32),
        grid=(nl, 3 * d // tn),
        in_specs=[pl.BlockSpec((rc, d), lambda l, j: (0, 0)),
                  pl.BlockSpec((None, d, tn), lambda l, j: (l, 0, j)),
                  pl.BlockSpec((None, 1, tn), lambda l, j: (l, 0, j))],
        out_specs=pl.BlockSpec((None, rc, tn), lambda l, j: (l, 0, j)),
        compiler_params=_cp("arbitrary", "arbitrary"),
        name="ada_mod",
    )(c_all, w, b)


def _slabs(mod, nb, ndb):
    d = mod.shape[-1]
    p = jnp.broadcast_to(mod[:nb, None, :], (nb, SUBLANE, d))
    s = jnp.zeros((1, SUBLANE, d), F32).at[0, :ndb].set(mod[nb:nb + ndb])
    return jnp.concatenate([p, s], axis=0)


def _mod_kernel(x_ref, sc_ref, sh_ref, o_ref):
    tm, d = x_ref.shape
    x = x_ref[...].reshape(tm // SUBLANE, SUBLANE, d)
    h = x * (1.0 + sc_ref[...][None]) + sh_ref[...][None]
    o_ref[...] = h.reshape(tm, d).astype(o_ref.dtype)


def _mod_router_kernel(x_ref, sc_ref, sh_ref, wr_ref, br_ref, o_ref, lg_ref):
    tm, d = x_ref.shape
    x = x_ref[...].reshape(tm // SUBLANE, SUBLANE, d)
    h = (x * (1.0 + sc_ref[...][None]) + sh_ref[...][None]).reshape(tm, d)
    o_ref[...] = h
    lg_ref[...] = jnp.dot(h, wr_ref[...], precision=HI, preferred_element_type=F32) + br_ref[...]


def _slab_spec(d, tiles_per_seq, nb):
    return pl.BlockSpec((None, SUBLANE, d), lambda i: (jnp.minimum(i // tiles_per_seq, nb), 0, 0))


def modulate(x_all, sc, sh, tiles_per_seq, nb):
    n, d = x_all.shape
    return pl.pallas_call(
        _mod_kernel,
        out_shape=jax.ShapeDtypeStruct((n, d), BF16),
        grid=(n // ROW_TILE,),
        in_specs=[pl.BlockSpec((ROW_TILE, d), lambda i: (i, 0)),
                  _slab_spec(d, tiles_per_seq, nb), _slab_spec(d, tiles_per_seq, nb)],
        out_specs=pl.BlockSpec((ROW_TILE, d), lambda i: (i, 0)),
        compiler_params=_cp("arbitrary"),
        name="modulate",
    )(x_all, sc, sh)


def modulate_router(x_all, sc, sh, w_r, b_r, li, tiles_per_seq, nb):
    n, d = x_all.shape
    ne = w_r.shape[-1]
    b_r3 = b_r.reshape(b_r.shape[0], 1, ne)
    return pl.pallas_call(
        _mod_router_kernel,
        out_shape=(jax.ShapeDtypeStruct((n, d), F32), jax.ShapeDtypeStruct((n, ne), F32)),
        grid=(n // ROW_TILE,),
        in_specs=[pl.BlockSpec((ROW_TILE, d), lambda i: (i, 0)),
                  _slab_spec(d, tiles_per_seq, nb), _slab_spec(d, tiles_per_seq, nb),
                  pl.BlockSpec((None, d, ne), lambda i: (li, 0, 0)),
                  pl.BlockSpec((None, 1, ne), lambda i: (li, 0, 0))],
        out_specs=(pl.BlockSpec((ROW_TILE, d), lambda i: (i, 0)),
                   pl.BlockSpec((ROW_TILE, ne), lambda i: (i, 0))),
        compiler_params=_cp("arbitrary"),
        name="modulate_router",
    )(x_all, sc, sh, w_r, b_r3)


def _resid_ln_kernel(x_ref, y_ref, gt_ref, g_ref, b_ref, o_ref):
    tm, d = x_ref.shape
    x = x_ref[...].reshape(tm // SUBLANE, SUBLANE, d)
    y = y_ref[...].reshape(tm // SUBLANE, SUBLANE, d)
    v = (DN_ALPHA * x + gt_ref[...][None] * y).reshape(tm, d)
    mu = jnp.mean(v, axis=-1, keepdims=True)
    c = v - mu
    var = jnp.mean(c * c, axis=-1, keepdims=True)
    o_ref[...] = c * lax.rsqrt(var + LN_EPS) * g_ref[...] + b_ref[...]


def resid_ln(x_all, y_all, gt, ln_g, ln_b, li, sub, tiles_per_seq, nb):
    n, d = x_all.shape
    g3 = ln_g.reshape(ln_g.shape[0] * ln_g.shape[1], 1, d)
    b3 = ln_b.reshape(ln_b.shape[0] * ln_b.shape[1], 1, d)
    idx = li * 2 + sub
    return pl.pallas_call(
        _resid_ln_kernel,
        out_shape=jax.ShapeDtypeStruct((n, d), F32),
        grid=(n // ROW_TILE,),
        in_specs=[pl.BlockSpec((ROW_TILE, d), lambda i: (i, 0)),
                  pl.BlockSpec((ROW_TILE, d), lambda i: (i, 0)),
                  _slab_spec(d, tiles_per_seq, nb),
                  pl.BlockSpec((None, 1, d), lambda i: (idx, 0, 0)),
                  pl.BlockSpec((None, 1, d), lambda i: (idx, 0, 0))],
        out_specs=pl.BlockSpec((ROW_TILE, d), lambda i: (i, 0)),
        compiler_params=_cp("arbitrary"),
        name="resid_ln",
    )(x_all, y_all, gt, g3, b3)


def _mm_kernel(x_ref, w_ref, o_ref):
    @pl.when(pl.program_id(2) == 0)
    def _():
        o_ref[...] = jnp.zeros_like(o_ref)

    o_ref[...] += _dot(x_ref[...], w_ref[...].astype(BF16))


def _mm_into_kernel(prev_ref, x_ref, w_ref, o_ref):
    del prev_ref
    _mm_kernel(x_ref, w_ref, o_ref)


def matmul(x, w, wsel, *, row0, rows, tm, tn, tk, ncols=None, out_rows=None, out_row0=0, into=None):
    kdim = x.shape[1]
    n = w.shape[-1] if ncols is None else ncols
    assert n % tn == 0
    nj = n // tn
    assert rows % tm == 0 and row0 % tm == 0 and out_row0 % tm == 0 and kdim % tk == 0
    rb0, ob0 = row0 // tm, out_row0 // tm
    out_rows = rows if out_rows is None else out_rows
    nlead = len(wsel)
    in_specs = [pl.BlockSpec((tm, tk), lambda i, j, k: (rb0 + i, k)),
                pl.BlockSpec((None,) * nlead + (tk, tn), lambda i, j, k: tuple(wsel) + (k, j))]
    args = [x, w]
    kern, aliases = _mm_kernel, {}
    if into is not None:
        assert into.shape == (out_rows, nj * tn)
        in_specs = [pl.BlockSpec(memory_space=pl.ANY)] + in_specs
        args = [into] + args
        kern, aliases = _mm_into_kernel, {0: 0}
    return pl.pallas_call(
        kern,
        out_shape=jax.ShapeDtypeStruct((out_rows, nj * tn), F32),
        grid=(rows // tm, nj, kdim // tk),
        in_specs=in_specs,
        out_specs=pl.BlockSpec((tm, tn), lambda i, j, k: (ob0 + i, j)),
        input_output_aliases=aliases,
        compiler_params=_cp("arbitrary", "arbitrary", "arbitrary"),
        name="matmul",
    )(*args)


def dense_prompt_and_sample(x_p, x_s, w, wsel, n_prompt, *, ncols=None, into_rows=None):
    kdim = x_p.shape[1]
    n = w.shape[-1] if ncols is None else ncols
    tm = _pick(n_prompt, 2048, 1024, 512, 256)
    tn = _pick(n, 1024, 512, 256, 128)
    tk = _pick(kdim, 1024, 512, 256)
    s_row0 = x_s.shape[0] - ROW_TILE
    if into_rows is None:
        z_p = matmul(x_p, w, wsel, row0=0, rows=n_prompt, tm=tm, tn=tn, tk=tk, ncols=ncols)
        z_s = matmul(x_s, w, wsel, row0=s_row0, rows=ROW_TILE, tm=ROW_TILE, tn=tn, tk=tk, ncols=ncols)
        return z_p, z_s
    y = jnp.zeros((into_rows, n), F32)
    y = matmul(x_p, w, wsel, row0=0, rows=n_prompt, tm=tm, tn=tn, tk=tk, ncols=ncols, out_rows=into_rows, into=y)
    return matmul(x_s, w, wsel, row0=s_row0, rows=ROW_TILE, tm=ROW_TILE, tn=tn, tk=tk, ncols=ncols,
                  out_rows=into_rows, out_row0=n_prompt, into=y)


def _ret_kernel(*refs, has_s0, nchunks):
    (q_ref, k_ref, v_ref, g_ref, cos_ref, sin_ref, dec_ref, qd_ref, kd_ref, cd_ref, gn_ref), rest = refs[:11], refs[11:]
    if has_s0:
        s0_ref, a_ref, so_ref, s_scr = rest
    else:
        a_ref, so_ref, s_scr = rest
    c = pl.program_id(2)

    @pl.when(c == 0)
    def _():
        s_scr[...] = s0_ref[...] if has_s0 else jnp.zeros_like(s_scr)

    half = RET_DK // 2
    cos = cos_ref[...]
    sin = sin_ref[...]

    def rot(x):
        x1, x2 = x[:, :half], x[:, half:]
        return jnp.concatenate([x1 * cos - x2 * sin, x1 * sin + x2 * cos], axis=-1)

    q = rot(q_ref[...])
    k = rot(k_ref[...]) * (RET_DK ** -0.5)
    qb = q.astype(BF16)
    vb = v_ref[...].astype(BF16)
    s = s_scr[...]
    att = _dot_nt(qb, k.astype(BF16)) * dec_ref[...]
    o = _dot(att.astype(BF16), vb) + _dot(qb, s.astype(BF16)) * qd_ref[...]
    s_new = s * cd_ref[...] + _dot_tn((k * kd_ref[...]).astype(BF16), vb)
    s_scr[...] = s_new
    mu = jnp.mean(o, axis=-1, keepdims=True)
    oc = o - mu
    var = jnp.mean(oc * oc, axis=-1, keepdims=True)
    on = oc * lax.rsqrt(var + LN_EPS) * gn_ref[...]
    g = g_ref[...]
    a_ref[...] = ((g / (1.0 + jnp.exp(-g))) * on).astype(BF16)

    @pl.when(c == nchunks - 1)
    def _():
        so_ref[...] = s_new


def retention(z3, pos, s0, gn_w, ri, true_chunk):
    b, tpad, _ = z3.shape
    cpad = min(RET_CHUNK, tpad)
    nchunks = tpad // cpad
    nk, nv = RET_HEADS * RET_DK, RET_HEADS * RET_DV
    half = RET_DK // 2
    inv = 1.0 / (ROPE_BASE ** (jnp.arange(half, dtype=F32) / half))
    ang = pos.astype(F32)[:, None] * inv[None, :]
    cos, sin = jnp.cos(ang), jnp.sin(ang)
    lg = jnp.log1p(-jnp.exp2(-5.0 - jnp.arange(RET_HEADS, dtype=F32)))
    i = jnp.arange(cpad, dtype=F32)
    diff = i[:, None] - i[None, :]
    decay = jnp.where(diff >= 0, jnp.exp(jnp.maximum(diff, 0.0)[None] * lg[:, None, None]), 0.0)
    q_dec = jnp.exp((i + 1.0)[None, :] * lg[:, None])[:, :, None]
    k_dec = jnp.exp((true_chunk - 1.0 - i)[None, :] * lg[:, None])[:, :, None]
    c_dec = jnp.exp(true_chunk * lg)[:, None, None]
    kq, kv = RET_DK, RET_DV
    in_specs = [
        pl.BlockSpec((None, cpad, kq), lambda bi, h, c: (bi, c, h)),
        pl.BlockSpec((None, cpad, kq), lambda bi, h, c: (bi, c, nk // kq + h)),
        pl.BlockSpec((None, cpad, kv), lambda bi, h, c: (bi, c, 2 * nk // kv + h)),
        pl.BlockSpec((None, cpad, kv), lambda bi, h, c: (bi, c, (2 * nk + nv) // kv + h)),
        pl.BlockSpec((cpad, half), lambda bi, h, c: (c, 0)),
        pl.BlockSpec((cpad, half), lambda bi, h, c: (c, 0)),
        pl.BlockSpec((None, cpad, cpad), lambda bi, h, c: (h, 0, 0)),
        pl.BlockSpec((None, cpad, 1), lambda bi, h, c: (h, 0, 0)),
        pl.BlockSpec((None, cpad, 1), lambda bi, h, c: (h, 0, 0)),
        pl.BlockSpec((None, 1, 1), lambda bi, h, c: (h, 0, 0)),
        pl.BlockSpec((None, 1, kv), lambda bi, h, c: (ri, 0, h)),
    ]
    args = [z3, z3, z3, z3, cos, sin, decay, q_dec, k_dec, c_dec, gn_w.reshape(gn_w.shape[0], 1, nv)]
    if s0 is not None:
        in_specs.append(pl.BlockSpec((None, None, None, kq, kv), lambda bi, h, c: (ri, bi, h, 0, 0)))
        args.append(s0)
    return pl.pallas_call(
        functools.partial(_ret_kernel, has_s0=s0 is not None, nchunks=nchunks),
        out_shape=(jax.ShapeDtypeStruct((b, tpad, nv), BF16),
                   jax.ShapeDtypeStruct((b, RET_HEADS, kq, kv), F32)),
        grid=(b, RET_HEADS, nchunks),
        in_specs=in_specs,
        out_specs=(pl.BlockSpec((None, cpad, kv), lambda bi, h, c: (bi, c, h)),
                   pl.BlockSpec((None, None, kq, kv), lambda bi, h, c: (bi, h, 0, 0))),
        scratch_shapes=[pltpu.VMEM((kq, kv), F32)],
        compiler_params=_cp("arbitrary", "arbitrary", "arbitrary"),
        name="retention",
    )(*args)


DMA_UNROLL = 8


def _moe_gather_kernel(tok_ref, nu_ref, h_hbm, o_ref, buf, sem, *, rb):
    i = pl.program_id(0)
    n_steps = nu_ref[0] * MOE_TM // rb

    def row_copy(step, r):
        slot = step % 2
        return pltpu.make_async_copy(h_hbm.at[pl.ds(tok_ref[step * rb + r], 1)], buf.at[slot, pl.ds(r, 1)], sem.at[slot])

    def start_block(step):
        def body(r8, carry):
            for u in range(DMA_UNROLL):
                row_copy(step, r8 * DMA_UNROLL + u).start(priority=u % 2)
            return carry

        lax.fori_loop(0, rb // DMA_UNROLL, body, 0)

    def wait_block(step):
        def body(r8, carry):
            for u in range(DMA_UNROLL):
                row_copy(step, r8 * DMA_UNROLL + u).wait()
            return carry

        lax.fori_loop(0, rb // DMA_UNROLL, body, 0)

    @pl.when(jnp.logical_and(i == 0, n_steps > 0))
    def _():
        start_block(0)

    @pl.when(i + 1 < n_steps)
    def _():
        start_block(i + 1)

    @pl.when(i < n_steps)
    def _():
        wait_block(i)
        o_ref[...] = buf[i % 2].astype(BF16)

    @pl.when(i >= n_steps)
    def _():
        o_ref[...] = jnp.zeros_like(o_ref)


def moe_gather(h_all, row_tok, n_used):
    rows = row_tok.shape[0]
    d = h_all.shape[1]
    rb = MOE_GATHER_ROWS
    return pl.pallas_call(
        functools.partial(_moe_gather_kernel, rb=rb),
        out_shape=jax.ShapeDtypeStruct((rows, d), BF16),
        grid_spec=pltpu.PrefetchScalarGridSpec(
            num_scalar_prefetch=2,
            grid=(rows // rb,),
            in_specs=[pl.BlockSpec(memory_space=pl.ANY)],
            out_specs=pl.BlockSpec((rb, d), lambda i, tok, nu: (i, 0)),
            scratch_shapes=[pltpu.VMEM((2, rb, d), F32), pltpu.SemaphoreType.DMA((2,))]),
        compiler_params=_cp("arbitrary"),
        name="moe_gather",
    )(row_tok, n_used, h_all)


def _moe_up_kernel(be_ref, first_ref, nu_ref, x_ref, w_ref, b_ref, o_ref, wbf, *, tn):
    blk = pl.program_id(1)

    @pl.when(blk < nu_ref[0])
    def _():
        @pl.when(first_ref[blk] == 1)
        def _():
            wbf[...] = w_ref[...].astype(BF16)

        x = x_ref[...]
        pick_even = (lax.broadcasted_iota(jnp.int32, (2 * LANE, LANE), 0)
                     == 2 * lax.broadcasted_iota(jnp.int32, (2 * LANE, LANE), 1)).astype(BF16)
        cw = MOE_UP_CHUNK
        nchunk = tn // cw

        def up(c):
            cols = slice(c * cw, (c + 1) * cw)
            return _dot(x, wbf[:, cols]) + b_ref[:, cols]

        def activate(hu):
            lin = pltpu.roll(hu, cw - 1, 1)
            glu = jnp.minimum(hu, SWIGLU_LIMIT)
            linc = jnp.clip(lin, -SWIGLU_LIMIT, SWIGLU_LIMIT)
            return ((glu / (1.0 + jnp.exp(-SWIGLU_ALPHA * glu))) * (linc + 1.0)).astype(BF16)

        hu = up(0)
        for c in range(nchunk):
            hu_next = up(c + 1) if c + 1 < nchunk else None
            act = activate(hu)
            for s in range(cw // (2 * LANE)):
                o0 = (c * cw // (2 * LANE) + s) * LANE
                o_ref[:, o0:o0 + LANE] = _dot(act[:, s * 2 * LANE:(s + 1) * 2 * LANE], pick_even).astype(BF16)
            hu = hu_next

    @pl.when(blk >= nu_ref[0])
    def _():
        o_ref[...] = jnp.zeros_like(o_ref)


def _moe_down_kernel(be_ref, first_ref, nu_ref, x_ref, w_ref, b_ref, o_ref, wbf):
    blk = pl.program_id(1)

    @pl.when(blk < nu_ref[0])
    def _():
        @pl.when(first_ref[blk] == 1)
        def _():
            wbf[...] = w_ref[...].astype(BF16)

        o_ref[...] = _dot(x_ref[...], wbf[...]) + b_ref[...]

    @pl.when(blk >= nu_ref[0])
    def _():
        o_ref[...] = jnp.zeros_like(o_ref)


def _moe_grouped(kern, x, w, bias, li, blk_e, first, n_used, tn, out_cols, out_tn, out_dtype, name):
    rows, kdim = x.shape
    n = w.shape[-1]
    n_blk = rows // MOE_TM
    b4 = bias.reshape(bias.shape[0], bias.shape[1], 1, n)

    def bc(blk, nu):
        return jnp.minimum(blk, nu[0] - 1)

    return pl.pallas_call(
        kern,
        out_shape=jax.ShapeDtypeStruct((rows, out_cols), out_dtype),
        grid_spec=pltpu.PrefetchScalarGridSpec(
            num_scalar_prefetch=3,
            grid=(n // tn, n_blk),
            in_specs=[pl.BlockSpec((MOE_TM, kdim), lambda j, blk, be, fi, nu: (bc(blk, nu), 0)),
                      pl.BlockSpec((None, None, kdim, tn), lambda j, blk, be, fi, nu: (li, be[bc(blk, nu)], 0, j)),
                      pl.BlockSpec((None, None, 1, tn), lambda j, blk, be, fi, nu: (li, be[bc(blk, nu)], 0, j))],
            out_specs=pl.BlockSpec((MOE_TM, out_tn), lambda j, blk, be, fi, nu: (blk, j)),
            scratch_shapes=[pltpu.VMEM((kdim, tn), BF16)]),
        compiler_params=_cp("arbitrary", "arbitrary"),
        name=name,
    )(blk_e, first, n_used, x, w, b4)


def _moe_combine_kernel(pos_ref, tv_ref, yb_hbm, o_ref, buf, sem, *, tt):
    i = pl.program_id(0)
    n_steps = pl.num_programs(0)
    rows_per_iter = DMA_UNROLL // TOP_K if DMA_UNROLL >= TOP_K else 1

    def row_copy(step, r, k):
        slot = step % 2
        return pltpu.make_async_copy(yb_hbm.at[pl.ds(pos_ref[(step * tt + r) * TOP_K + k], 1)],
                                     buf.at[slot, k, pl.ds(r, 1)], sem.at[slot])

    def start_tile(step):
        def body(rr, carry):
            for u in range(rows_per_iter):
                for k in range(TOP_K):
                    row_copy(step, rr * rows_per_iter + u, k).start(priority=k % 2)
            return carry

        lax.fori_loop(0, tt // rows_per_iter, body, 0)

    def wait_tile(step):
        def body(rr, carry):
            for u in range(rows_per_iter):
                for k in range(TOP_K):
                    row_copy(step, rr * rows_per_iter + u, k).wait()
            return carry

        lax.fori_loop(0, tt // rows_per_iter, body, 0)

    @pl.when(i == 0)
    def _():
        start_tile(0)

    @pl.when(i + 1 < n_steps)
    def _():
        start_tile(i + 1)

    wait_tile(i)
    slot = i % 2
    tv = tv_ref[...]
    e = jnp.exp(tv - jnp.max(tv, axis=-1, keepdims=True))
    gate = e / jnp.sum(e, axis=-1, keepdims=True)
    acc = gate[:, 0:1] * buf[slot, 0]
    for k in range(1, TOP_K):
        acc = acc + gate[:, k:k + 1] * buf[slot, k]
    o_ref[...] = acc


def moe_combine(yb, pos_flat, top_v):
    n = top_v.shape[0]
    d = yb.shape[1]
    tt = MOE_COMBINE_ROWS
    return pl.pallas_call(
        functools.partial(_moe_combine_kernel, tt=tt),
        out_shape=jax.ShapeDtypeStruct((n, d), F32),
        grid_spec=pltpu.PrefetchScalarGridSpec(
            num_scalar_prefetch=1,
            grid=(n // tt,),
            in_specs=[pl.BlockSpec((tt, TOP_K), lambda i, pos: (i, 0)),
                      pl.BlockSpec(memory_space=pl.ANY)],
            out_specs=pl.BlockSpec((tt, d), lambda i, pos: (i, 0)),
            scratch_shapes=[pltpu.VMEM((2, TOP_K, tt, d), F32), pltpu.SemaphoreType.DMA((2,))]),
        compiler_params=_cp("arbitrary"),
        name="moe_combine",
    )(pos_flat, top_v, yb)


def moe(h_all, logits, ntok, li, w_up, b_up, w_down, b_down):
    npad, d = h_all.shape
    tm = MOE_TM
    top_v, top_e = lax.top_k(logits[:ntok], TOP_K)
    n_asg = ntok * TOP_K
    n_blk = (n_asg + N_EXPERTS * (tm - 1) + tm - 1) // tm
    rows = n_blk * tm
    flat_e = top_e.reshape(n_asg)
    onehot = (jnp.arange(N_EXPERTS, dtype=jnp.int32)[:, None] == flat_e[None, :]).astype(jnp.int32)
    prefix = jnp.cumsum(onehot, axis=1)
    counts = prefix[:, -1]
    pcounts = (counts + tm - 1) // tm * tm
    pend = jnp.cumsum(pcounts)
    pstart = pend - pcounts
    dest = jnp.sum(onehot * (prefix - 1 + pstart[:, None]), axis=0).astype(jnp.int32)
    row_tok = jnp.zeros((rows,), jnp.int32).at[dest].set(jnp.arange(n_asg, dtype=jnp.int32) // TOP_K)
    pos = jnp.zeros((npad * TOP_K,), jnp.int32).at[:n_asg].set(dest)
    blk_e = jnp.minimum(jnp.sum((pend[None, :] <= (jnp.arange(n_blk) * tm)[:, None]).astype(jnp.int32), axis=1),
                        N_EXPERTS - 1).astype(jnp.int32)
    first = jnp.concatenate([jnp.ones((1,), jnp.int32), (blk_e[1:] != blk_e[:-1]).astype(jnp.int32)])
    n_used = (pend[-1] // tm).astype(jnp.int32).reshape(1)
    tv_pad = jnp.zeros((npad, TOP_K), F32).at[:ntok].set(top_v)

    xs = moe_gather(h_all, row_tok, n_used)
    tn_up = _pick(2 * D_EXPERT, 1024, 512, 256)
    act = _moe_grouped(functools.partial(_moe_up_kernel, tn=tn_up), xs, w_up, b_up, li, blk_e, first, n_used,
                       tn_up, D_EXPERT, tn_up // 2, BF16, "moe_up")
    tn_dn = _pick(d, 2048, 1024, 512, 256)
    yb = _moe_grouped(_moe_down_kernel, act, w_down, b_down, li, blk_e, first, n_used,
                      tn_dn, d, tn_dn, F32, "moe_down")
    return moe_combine(yb, pos, tv_pad)


def _t5_bucket(dist):
    n = jnp.maximum(dist, 0)
    exact = REL_BUCKETS // 2
    nf = jnp.maximum(n, 1).astype(F32)
    large = exact + (jnp.log(nf / exact) / math.log(REL_MAX_DIST / exact) * (REL_BUCKETS - exact)).astype(jnp.int32)
    large = jnp.minimum(large, REL_BUCKETS - 1)
    return jnp.where(n < exact, n, large)


def _bias_of_dist(dist, table):
    onehot = (_t5_bucket(dist)[None] == jnp.arange(REL_BUCKETS).reshape((REL_BUCKETS,) + (1,) * dist.ndim)).astype(F32)
    return jnp.tensordot(table.astype(F32).T, onehot, axes=((1,), (0,)), precision=HI)


def _cmp_geometry(tk):
    ncb = (tk - CMP_LEN) // CMP_STRIDE + 1
    nch = -(-(ncb + CMP_LEN // CMP_STRIDE - 1) // (CMP_PAGES * SUBLANE)) * (CMP_PAGES * SUBLANE)
    nsb = -(-tk // SEL_BLK)
    nsbp = -(-nsb // LANE) * LANE
    return ncb, nch, nsb, nsbp


def _cmpa_kernel(*refs, npage):
    page_refs, w_ref, o_ref, pg = refs[1:npage + 1], refs[npage + 1], refs[npage + 2], refs[npage + 3]
    cpp = PAGE_SIZE // CMP_STRIDE
    m_rows = NSA_KV_HEADS * npage * cpp
    blk = 2 * HEAD_DIM
    per = CMP_LEN // CMP_STRIDE
    for p in range(npage):
        for slot in range(2):
            for g in range(NSA_KV_HEADS):
                pg[p, slot * NSA_KV_HEADS + g] = page_refs[p][:, slot, g, :]
    for slot in range(2):
        accs = [jnp.zeros((m_rows, CMP_HIDDEN), F32) for _ in range(per)]
        for s2 in range(CMP_STRIDE // 2):
            parts = []
            for g in range(NSA_KV_HEADS):
                for p in range(npage):
                    c = slot * NSA_KV_HEADS + g
                    x0 = pg[p, c, pl.ds(2 * s2, cpp, stride=CMP_STRIDE), :]
                    x1 = pg[p, c, pl.ds(2 * s2 + 1, cpp, stride=CMP_STRIDE), :]
                    parts.append(jnp.concatenate([x0, x1], axis=1))
            xm = jnp.concatenate(parts, axis=0).astype(BF16)
            for m in range(per):
                w = w_ref[slot, pl.ds(m * CMP_STRIDE * HEAD_DIM + s2 * blk, blk), :].astype(BF16)
                accs[m] = accs[m] + _dot(xm, w)
        o_ref[slot] = jnp.concatenate(accs, axis=1).reshape(NSA_KV_HEADS, npage * cpp, per * CMP_HIDDEN)


def compress_chunks(src6, page_rows, li, w1, ni, nb, nch):
    npg = page_rows.shape[1]
    cpp = PAGE_SIZE // CMP_STRIDE
    steps = nch // (CMP_PAGES * cpp)
    per = CMP_LEN // CMP_STRIDE
    idx = jnp.minimum(jnp.arange(steps * CMP_PAGES), npg - 1)
    pr = page_rows[:, idx].reshape(-1).astype(jnp.int32)

    def page_spec(p):
        return pl.BlockSpec((None, None, PAGE_SIZE, 2, NSA_KV_HEADS, HEAD_DIM),
                            lambda b, s, pr_ref: (pr_ref[(b * steps + s) * CMP_PAGES + p], li, 0, 0, 0, 0))

    return pl.pallas_call(
        functools.partial(_cmpa_kernel, npage=CMP_PAGES),
        out_shape=jax.ShapeDtypeStruct((2, nb, NSA_KV_HEADS, nch, per * CMP_HIDDEN), F32),
        grid_spec=pltpu.PrefetchScalarGridSpec(
            num_scalar_prefetch=1,
            grid=(nb, steps),
            in_specs=[page_spec(p) for p in range(CMP_PAGES)]
            + [pl.BlockSpec((None, 2, CMP_LEN * HEAD_DIM, CMP_HIDDEN), lambda b, s, pr_ref: (ni, 0, 0, 0))],
            out_specs=pl.BlockSpec((2, None, NSA_KV_HEADS, CMP_PAGES * cpp, per * CMP_HIDDEN),
                                   lambda b, s, pr_ref: (0, b, 0, s, 0)),
            scratch_shapes=[pltpu.VMEM((CMP_PAGES, 2 * NSA_KV_HEADS, PAGE_SIZE, HEAD_DIM), F32)]),
        compiler_params=_cp("arbitrary", "arbitrary"),
        name="compress_chunks",
    )(pr, *([src6] * CMP_PAGES), w1)


def _cmpb_kernel(a_ref, pe_ref, w1_ref, b1_ref, w2_ref, b2_ref, o_ref):
    nch = a_ref.shape[0]
    pew = _dot(pe_ref[...].astype(BF16), w1_ref[...].astype(BF16))[0:1]
    a = a_ref[...]
    hid = b1_ref[...] + pew
    hid = hid + a[:, :CMP_HIDDEN]
    hid = hid + pltpu.roll(a[:, CMP_HIDDEN:], nch - 1, 0)
    act = jax.nn.gelu(hid, approximate=True)
    o_ref[...] = _dot(act.astype(BF16), w2_ref[...].astype(BF16)) + b2_ref[...]


def compress_blocks(a, pe, w1, b1, w2, b2, ni):
    _, nb, g, nch, _ = a.shape
    assert CMP_LEN // CMP_STRIDE == 2
    pe8 = jnp.broadcast_to(pe.reshape(pe.shape[0], 2, 1, CMP_LEN * HEAD_DIM), (pe.shape[0], 2, SUBLANE, CMP_LEN * HEAD_DIM))
    return pl.pallas_call(
        _cmpb_kernel,
        out_shape=jax.ShapeDtypeStruct((2, nb, nch, g * HEAD_DIM), F32),
        grid=(2, nb, g),
        in_specs=[pl.BlockSpec((None, None, None, nch, 2 * CMP_HIDDEN), lambda s, b, gi: (s, b, gi, 0, 0)),
                  pl.BlockSpec((None, None, SUBLANE, CMP_LEN * HEAD_DIM), lambda s, b, gi: (ni, s, 0, 0)),
                  pl.BlockSpec((None, None, CMP_LEN * HEAD_DIM, CMP_HIDDEN), lambda s, b, gi: (ni, s, 0, 0)),
                  pl.BlockSpec((None, None, 1, CMP_HIDDEN), lambda s, b, gi: (ni, s, 0, 0)),
                  pl.BlockSpec((None, None, CMP_HIDDEN, HEAD_DIM), lambda s, b, gi: (ni, s, 0, 0)),
                  pl.BlockSpec((None, None, 1, HEAD_DIM), lambda s, b, gi: (ni, s, 0, 0))],
        out_specs=pl.BlockSpec((None, None, nch, HEAD_DIM), lambda s, b, gi: (s, b, 0, gi)),
        compiler_params=_cp("arbitrary", "arbitrary", "arbitrary"),
        name="compress_blocks",
    )(a, pe8, w1, b1.reshape(b1.shape[0], 2, 1, CMP_HIDDEN), w2, b2.reshape(b2.shape[0], 2, 1, HEAD_DIM))


def _cmp_attn_kernel(q_ref, kc_ref, vc_ref, bias_ref, cov_ref, o_ref, sc_ref):
    kc = kc_ref[...].astype(BF16)
    vc = vc_ref[...].astype(BF16)
    scale = HEAD_DIM ** -0.5
    tq = q_ref.shape[0]
    nch = kc.shape[0]
    qall = jnp.concatenate([q_ref[:, r * HEAD_DIM:(r + 1) * HEAD_DIM] for r in range(NSA_GROUP)], axis=0).astype(BF16)
    lg = _dot_nt(qall, kc) * scale + bias_ref[...].reshape(NSA_GROUP * tq, nch)
    mx = jnp.max(lg, axis=-1, keepdims=True)
    e = jnp.exp(lg - mx)
    p = e / jnp.sum(e, axis=-1, keepdims=True) * (mx > 0.1 * NEG_INF).astype(F32)
    o = _dot(p.astype(BF16), vc)
    for r in range(NSA_GROUP):
        o_ref[:, r * HEAD_DIM:(r + 1) * HEAD_DIM] = o[r * tq:(r + 1) * tq]
    p_heads = jnp.sum(p.reshape(NSA_GROUP, tq, nch), axis=0)
    sc_ref[...] = jnp.dot(p_heads, cov_ref[...], precision=HI, preferred_element_type=F32)


def cmp_attention(z3, kvc, bias, cover, tq):
    b, t, _ = z3.shape
    nch = kvc.shape[2]
    nsbp = cover.shape[1]
    gw = NSA_GROUP * HEAD_DIM
    return pl.pallas_call(
        _cmp_attn_kernel,
        out_shape=(jax.ShapeDtypeStruct((b, t, NSA_HEADS * HEAD_DIM), F32),
                   jax.ShapeDtypeStruct((b, NSA_KV_HEADS, t, nsbp), F32)),
        grid=(b, NSA_KV_HEADS, t // tq),
        in_specs=[pl.BlockSpec((None, tq, gw), lambda bi, g, i: (bi, i, g)),
                  pl.BlockSpec((None, None, nch, HEAD_DIM), lambda bi, g, i: (0, bi, 0, g)),
                  pl.BlockSpec((None, None, nch, HEAD_DIM), lambda bi, g, i: (1, bi, 0, g)),
                  pl.BlockSpec((NSA_GROUP, tq, nch), lambda bi, g, i: (g, i, 0)),
                  pl.BlockSpec((nch, nsbp), lambda bi, g, i: (0, 0))],
        out_specs=(pl.BlockSpec((None, tq, gw), lambda bi, g, i: (bi, i, g)),
                   pl.BlockSpec((None, None, tq, nsbp), lambda bi, g, i: (bi, g, i, 0))),
        compiler_params=_cp("arbitrary", "arbitrary", "arbitrary"),
        name="cmp_attention",
    )(z3, kvc, kvc, bias, cover)


def cmp_tables(q_pos, tk, nch, nsbp, table):
    ncb = (tk - CMP_LEN) // CMP_STRIDE + 1
    nsb = -(-tk // SEL_BLK)
    n = jnp.arange(nch)
    dist = q_pos[:, None] - (n * CMP_STRIDE + CMP_LEN - 1)[None, :]
    vis = (dist >= 0) & (n < ncb)[None, :]
    bias = jnp.where(vis[None], _bias_of_dist(dist, table), NEG_INF)
    j = jnp.arange(nsbp)[None, :]
    i = n[:, None]
    cover = ((i * CMP_STRIDE < (j + 1) * SEL_BLK) & (i * CMP_STRIDE + CMP_LEN > j * SEL_BLK)
             & (i < ncb) & (j < nsb)).astype(F32)
    return bias, cover


def select_blocks(score, q_pos, nsb):
    j = jnp.arange(nsb)[None, :]
    cur = (q_pos // SEL_BLK)[:, None]
    forced = (j == 0) | (j == cur) | (j == cur - 1)
    s = score[..., :nsb]
    s = jnp.where((j > cur)[None, None], NEG_SCORE, s + jnp.where(forced, FORCE_BONUS, 0.0)[None, None])
    _, idx = lax.top_k(s, min(SEL_TOPK, nsb))
    return idx


def _nsa_attn_kernel(q_ref, ks_ref, vs_ref, kw_ref, vw_ref, msk_ref, tb_ref, oc_ref, gt_ref, o_ref):
    i = pl.program_id(2)
    tq = QBLK
    rows = NSA_GROUP * tq
    scale = HEAD_DIM ** -0.5
    qall = jnp.concatenate([q_ref[:, r * HEAD_DIM:(r + 1) * HEAD_DIM] for r in range(NSA_GROUP)], axis=0).astype(BF16)
    kpos0 = lax.broadcasted_iota(jnp.int32, (tq, tq), 0)
    qpos = i * tq + lax.broadcasted_iota(jnp.int32, (tq, tq), 1)
    selm_t = msk_ref[...]
    nsbp = selm_t.shape[0]
    per_blk = tq // SEL_BLK
    srow = lax.broadcasted_iota(jnp.int32, (tq, nsbp), 0) // SEL_BLK
    jcol = lax.broadcasted_iota(jnp.int32, (tq, nsbp), 1)
    pass_cols = rows

    def sel_valid(m):
        expand_t = (jcol == per_blk * m + srow).astype(F32)
        picked = _dot(expand_t, selm_t)
        return (picked > 0.5) & (m * tq + kpos0 <= qpos)

    def win_valid(m):
        dist = qpos - (m * tq + kpos0)
        return (dist >= 0) & (dist < WINDOW)

    def branch(k_ref, v_ref, lo, valid_fn):
        def body(m, carry):
            mx, l, acc = carry
            r0 = pl.multiple_of(m * tq, tq)
            k = k_ref[pl.ds(r0, tq), :].astype(BF16)
            v_t = v_ref[pl.ds(r0, tq), :].T.astype(BF16)
            valid = valid_fn(m)
            valid = jnp.concatenate([valid] * (pass_cols // tq), axis=1)
            bias = tb_ref[jnp.minimum(i - m, 2)]
            mx_o, l_o, acc_o = [], [], []
            for c in range(rows // pass_cols):
                cs = slice(c * pass_cols, (c + 1) * pass_cols)
                s = _dot_nt(k, qall[cs]) * scale + bias[:, cs]
                s = jnp.where(valid, s, NEG_INF)
                mx_new = jnp.maximum(mx[:, cs], jnp.max(s, axis=0, keepdims=True))
                alpha = jnp.exp(mx[:, cs] - mx_new)
                p = jnp.exp(s - mx_new)
                mx_o.append(mx_new)
                l_o.append(alpha * l[:, cs] + jnp.sum(p, axis=0, keepdims=True))
                acc_o.append(alpha * acc[:, cs] + _dot(v_t, p.astype(BF16)))
            return jnp.concatenate(mx_o, axis=1), jnp.concatenate(l_o, axis=1), jnp.concatenate(acc_o, axis=1)

        init = (jnp.full((1, rows), NEG_INF, F32), jnp.zeros((1, rows), F32), jnp.zeros((HEAD_DIM, rows), F32))
        _, l, acc = lax.fori_loop(lo, i + 1, body, init)
        return acc / l

    o_sel = branch(ks_ref, vs_ref, 0, sel_valid)
    o_win = branch(kw_ref, vw_ref, jnp.maximum(i - WINDOW // tq, 0), win_valid)
    gt = gt_ref[...]
    gates = 1.0 / (1.0 + jnp.exp(-gt))
    for r in range(NSA_GROUP):
        sl = slice(r * HEAD_DIM, (r + 1) * HEAD_DIM)
        o = (gates[:, 3 * r:3 * r + 1] * oc_ref[:, sl]
             + gates[:, 3 * r + 1:3 * r + 2] * o_sel[:, r * tq:(r + 1) * tq].T
             + gates[:, 3 * r + 2:3 * r + 3] * o_win[:, r * tq:(r + 1) * tq].T)
        o_ref[:, sl] = o.astype(BF16)


def nsa_attention_prompt(z3, sel_mask, tb, o_cmp, gate_lin):
    b, t, _ = z3.shape
    gw = NSA_GROUP * HEAD_DIM
    nsbp = sel_mask.shape[-2]
    kv0 = NSA_HEADS * HEAD_DIM // HEAD_DIM
    g4 = NSA_KV_HEADS

    def kv_spec(slot):
        return pl.BlockSpec((None, t, HEAD_DIM), lambda bi, g, i: (bi, 0, kv0 + slot * g4 + g))

    return pl.pallas_call(
        _nsa_attn_kernel,
        out_shape=jax.ShapeDtypeStruct((b, t, NSA_HEADS * HEAD_DIM), BF16),
        grid=(b, NSA_KV_HEADS, t // QBLK),
        in_specs=[pl.BlockSpec((None, QBLK, gw), lambda bi, g, i: (bi, i, g)),
                  kv_spec(2), kv_spec(3), kv_spec(4), kv_spec(5),
                  pl.BlockSpec((None, None, nsbp, QBLK), lambda bi, g, i: (bi, g, 0, i)),
                  pl.BlockSpec((3, None, QBLK, NSA_GROUP * QBLK), lambda bi, g, i: (0, g, 0, 0)),
                  pl.BlockSpec((None, QBLK, gw), lambda bi, g, i: (bi, i, g)),
                  pl.BlockSpec((None, None, QBLK, 3 * NSA_GROUP), lambda bi, g, i: (bi, g, i, 0))],
        out_specs=pl.BlockSpec((None, QBLK, gw), lambda bi, g, i: (bi, i, g)),
        compiler_params=_cp("arbitrary", "arbitrary", "arbitrary"),
        name="nsa_attention_prompt",
    )(z3, z3, z3, z3, z3, sel_mask, tb, o_cmp, gate_lin)


def _sel_sample_kernel(rb_ref, js_ref, q_ref, k_ref, v_ref, kn_ref, vn_ref, b_ref, o_ref, m_s, l_s, a_s, kv_s, *, jlast, ksel):
    bi, g, kk = pl.program_id(0), pl.program_id(1), pl.program_id(2)
    j = js_ref[(bi * NSA_KV_HEADS + g) * ksel + kk]
    scale = HEAD_DIM ** -0.5

    @pl.when(kk == 0)
    def _():
        m_s[...] = jnp.full_like(m_s, NEG_INF)
        l_s[...] = jnp.zeros_like(l_s)
        a_s[...] = jnp.zeros_like(a_s)

    for gg in range(NSA_KV_HEADS):
        @pl.when(g == gg)
        def _():
            kv_s[0] = k_ref[:, gg, :]
            kv_s[1] = v_ref[:, gg, :]

    is_new = j == jlast
    first_row = lax.broadcasted_iota(jnp.int32, (SEL_BLK, HEAD_DIM), 0) == 0
    k = jnp.where(is_new, jnp.where(first_row, kn_ref[...], 0.0), kv_s[0])
    v = jnp.where(is_new, jnp.where(first_row, vn_ref[...], 0.0), kv_s[1])
    s = _dot_nt(q_ref[...].astype(BF16), k.astype(BF16)) * scale + b_ref[...]
    valid = jnp.logical_or(jnp.logical_not(is_new), lax.broadcasted_iota(jnp.int32, s.shape, 1) == 0)
    s = jnp.where(valid, s, NEG_INF)
    mx = m_s[...]
    mx_new = jnp.maximum(mx, jnp.max(s, axis=-1, keepdims=True))
    alpha = jnp.exp(mx - mx_new)
    p = jnp.exp(s - mx_new)
    l_new = alpha * l_s[...] + jnp.sum(p, axis=-1, keepdims=True)
    a_new = alpha * a_s[...] + _dot(p.astype(BF16), v.astype(BF16))
    m_s[...] = mx_new
    l_s[...] = l_new
    a_s[...] = a_new

    @pl.when(kk == ksel - 1)
    def _():
        o_ref[...] = a_new / l_new


def sel_attention_sample(q4, cache, row_blk, jsel, k_new, v_new, bias, ni, jlast):
    db, g, r, hd = q4.shape
    ksel = jsel.shape[-1]
    per_page = PAGE_SIZE // SEL_BLK

    def blk_spec(slot):
        def index(bi, gi, kk, rb, js):
            blk = rb[(bi * g + gi) * ksel + kk]
            return (blk // per_page, ni, blk % per_page, slot, 0, 0)

        return pl.BlockSpec((None, None, SEL_BLK, None, g, HEAD_DIM), index)

    return pl.pallas_call(
        functools.partial(_sel_sample_kernel, jlast=jlast, ksel=ksel),
        out_shape=jax.ShapeDtypeStruct((db, g, r, hd), F32),
        grid_spec=pltpu.PrefetchScalarGridSpec(
            num_scalar_prefetch=2,
            grid=(db, g, ksel),
            in_specs=[pl.BlockSpec((None, None, r, hd), lambda bi, gi, kk, rb, js: (bi, gi, 0, 0)),
                      blk_spec(2), blk_spec(3),
                      pl.BlockSpec((None, None, 1, hd), lambda bi, gi, kk, rb, js: (bi, gi, 0, 0)),
                      pl.BlockSpec((None, None, 1, hd), lambda bi, gi, kk, rb, js: (bi, gi, 0, 0)),
                      pl.BlockSpec((None, None, r, SEL_BLK),
                                   lambda bi, gi, kk, rb, js: (gi, js[(bi * g + gi) * ksel + kk], 0, 0))],
            out_specs=pl.BlockSpec((None, None, r, hd), lambda bi, gi, kk, rb, js: (bi, gi, 0, 0)),
            scratch_shapes=[pltpu.VMEM((r, 1), F32), pltpu.VMEM((r, 1), F32), pltpu.VMEM((r, hd), F32),
                            pltpu.VMEM((2, SEL_BLK, hd), F32)]),
        compiler_params=_cp("arbitrary", "arbitrary", "arbitrary"),
        name="sel_attention_sample",
    )(row_blk.reshape(-1), jsel.reshape(-1), q4, cache, cache, k_new, v_new, bias)


def _win_sample_kernel(q_ref, kb_ref, vb_ref, kn_ref, vn_ref, bw_ref, b0_ref, oc_ref, os_ref, gt_ref, o_ref):
    scale = HEAD_DIM ** -0.5
    q = q_ref[...]
    wb = kb_ref.shape[0]
    s = _dot_nt(q.astype(BF16), kb_ref[...].astype(BF16)) * scale + bw_ref[...]
    dist = wb - lax.broadcasted_iota(jnp.int32, s.shape, 1)
    s = jnp.where(dist < WINDOW, s, NEG_INF)
    s_new = jnp.sum(q * kn_ref[...], axis=-1, keepdims=True) * scale + b0_ref[...][:, 0:1]
    mx = jnp.maximum(jnp.max(s, axis=-1, keepdims=True), s_new)
    e = jnp.exp(s - mx)
    e_new = jnp.exp(s_new - mx)
    l = jnp.sum(e, axis=-1, keepdims=True) + e_new
    o_win = (_dot(e.astype(BF16), vb_ref[...].astype(BF16)) + e_new * vn_ref[...]) / l
    gates = 1.0 / (1.0 + jnp.exp(-gt_ref[...]))
    o_ref[...] = gates[:, 0:1] * oc_ref[...] + gates[:, 1:2] * os_ref[...] + gates[:, 2:3] * o_win


def win_attention_sample(q4, win3, k_new, v_new, bias_w, bias0, o_cmp4, o_sel4, gate4, ni):
    db, g, r, hd = q4.shape
    wb = win3.shape[1]

    def small(shape_last):
        return pl.BlockSpec((None, None, r, shape_last), lambda bi, gi: (bi, gi, 0, 0))

    return pl.pallas_call(
        _win_sample_kernel,
        out_shape=jax.ShapeDtypeStruct((db, g, r, hd), F32),
        grid=(db, g),
        in_specs=[small(hd),
                  pl.BlockSpec((None, wb, hd), lambda bi, gi: (ni * db + bi, 0, gi)),
                  pl.BlockSpec((None, wb, hd), lambda bi, gi: (ni * db + bi, 0, g + gi)),
                  pl.BlockSpec((None, None, 1, hd), lambda bi, gi: (bi, gi, 0, 0)),
                  pl.BlockSpec((None, None, 1, hd), lambda bi, gi: (bi, gi, 0, 0)),
                  pl.BlockSpec((None, r, wb), lambda bi, gi: (gi, 0, 0)),
                  pl.BlockSpec((None, r, LANE), lambda bi, gi: (gi, 0, 0)),
                  small(hd), small(hd), small(3)],
        out_specs=small(hd),
        compiler_params=_cp("arbitrary", "arbitrary"),
        name="win_attention_sample",
    )(q4, win3, win3, k_new, v_new, bias_w, bias0, o_cmp4, o_sel4, gate4)


def nsa_layer(h_all, bt, b, t, db, ni, cache_nsa, state_nsa_win, page_table, rel_bias,
              nsa_w_in, nsa_cmp_pe, nsa_cmp_w1, nsa_cmp_b1, nsa_cmp_w2, nsa_cmp_b2):
    nq = NSA_HEADS * HEAD_DIM
    nqkv = nq + 6 * NSA_KV
    z_p, z_s = dense_prompt_and_sample(h_all, h_all, nsa_w_in, (ni,), bt, ncols=nqkv)
    w_gate = jnp.zeros((h_all.shape[1], LANE), F32).at[:, :3 * NSA_HEADS].set(nsa_w_in[ni, :, nqkv:])
    zg_p, zg_s = dense_prompt_and_sample(h_all, h_all, w_gate, (), bt)
    wz = z_p.shape[1]
    z3 = z_p.reshape(b, t, wz)
    n_layers = cache_nsa.shape[1]
    past_len = page_table.shape[1] * PAGE_SIZE
    pos_p = jnp.arange(t)

    ncb, nch, nsb, nsbp = _cmp_geometry(t)
    page_rows = (jnp.arange(b)[:, None] * (t // PAGE_SIZE) + jnp.arange(t // PAGE_SIZE)[None, :])
    rows_p = z3[:, :, nq:nq + 4 * NSA_KV].reshape(b, t, 4, NSA_KV_HEADS, HEAD_DIM)
    src_p = rows_p.reshape(bt // PAGE_SIZE, 1, PAGE_SIZE, 4, NSA_KV_HEADS, HEAD_DIM)
    a_p = compress_chunks(src_p, page_rows, 0, nsa_cmp_w1, ni, b, nch)
    kvc_p = compress_blocks(a_p, nsa_cmp_pe, nsa_cmp_w1, nsa_cmp_b1, nsa_cmp_w2, nsa_cmp_b2, ni)
    bias_p, cover_p = cmp_tables(pos_p, t, nch, nsbp, rel_bias)
    o_cmp_p, score_p = cmp_attention(z3, kvc_p, bias_p, cover_p, QBLK)
    idx_p = select_blocks(score_p, pos_p, nsb)
    sel_mask = (idx_p[:, :, None, :, :] == jnp.arange(nsbp)[None, None, :, None, None]).any(axis=-1).astype(F32)
    ii = jnp.arange(QBLK)
    tb = _bias_of_dist(jnp.arange(3)[:, None, None] * QBLK + ii[None, None, :] - ii[None, :, None], rel_bias)
    tb = tb.reshape(NSA_KV_HEADS, NSA_GROUP, 3, QBLK, QBLK).transpose((2, 0, 3, 1, 4))
    tb = tb.reshape(3, NSA_KV_HEADS, QBLK, NSA_GROUP * QBLK)
    gate_lin = zg_p[:, :3 * NSA_HEADS].reshape(b, t, NSA_KV_HEADS, 3 * NSA_GROUP).transpose((0, 2, 1, 3))
    o_p = nsa_attention_prompt(z3, sel_mask, tb, o_cmp_p, gate_lin)
    keep = min(WINDOW, t)
    win_p = z3[:, t - keep:, nq + 4 * NSA_KV:nq + 6 * NSA_KV].reshape(b, keep, 2, NSA_KV_HEADS, HEAD_DIM)

    zs = z_s[:db]
    tk = past_len + DEC_SEQ
    ncb_s, nch_s, nsb_s, nsbp_s = _cmp_geometry(tk)
    assert (ncb_s + 1) * CMP_STRIDE <= past_len
    a_s = compress_chunks(cache_nsa, page_table, ni, nsa_cmp_w1, ni, db, nch_s)
    kvc_s = compress_blocks(a_s, nsa_cmp_pe, nsa_cmp_w1, nsa_cmp_b1, nsa_cmp_w2, nsa_cmp_b2, ni)
    pos_s = jnp.full((SUBLANE,), past_len, jnp.int32)
    bias_s, cover_s = cmp_tables(pos_s, tk, nch_s, nsbp_s, rel_bias)
    zs3 = jnp.zeros((db, SUBLANE, wz), F32).at[:, 0].set(zs)
    o_cmp_s, score_s = cmp_attention(zs3, kvc_s, bias_s, cover_s, SUBLANE)
    idx_s = select_blocks(score_s[:, :, 0:1], pos_s[0:1], nsb_s)[:, :, 0]
    jlast = past_len // SEL_BLK
    per_page = PAGE_SIZE // SEL_BLK
    page_of = jnp.take_along_axis(page_table, jnp.minimum(idx_s, jlast - 1).reshape(db, -1) // per_page, axis=1)
    row_blk = (page_of.reshape(idx_s.shape) * per_page + jnp.minimum(idx_s, jlast - 1) % per_page).astype(jnp.int32)
    q4 = zs[:, :nq].reshape(db, NSA_KV_HEADS, NSA_GROUP, HEAD_DIM)
    kv_new = zs[:, nq:nq + 6 * NSA_KV].reshape(db, 6, NSA_KV_HEADS, 1, HEAD_DIM)
    kpos = jnp.arange(nsb_s * SEL_BLK).reshape(nsb_s, SEL_BLK)
    bias_sel = _bias_of_dist(past_len - kpos, rel_bias).reshape(NSA_KV_HEADS, NSA_GROUP, nsb_s, SEL_BLK).transpose((0, 2, 1, 3))
    o_sel_s = sel_attention_sample(q4, cache_nsa, row_blk, idx_s.astype(jnp.int32), kv_new[:, 2], kv_new[:, 3],
                                   bias_sel, ni, jlast)
    wb = state_nsa_win.shape[2]
    win3 = state_nsa_win.reshape(state_nsa_win.shape[0] * db, wb, 2 * NSA_KV)
    bias_w = _bias_of_dist(wb - jnp.arange(wb), rel_bias).reshape(NSA_KV_HEADS, NSA_GROUP, wb)
    bias0 = jnp.broadcast_to(_bias_of_dist(jnp.zeros((1,), jnp.int32), rel_bias).reshape(NSA_KV_HEADS, NSA_GROUP, 1),
                             (NSA_KV_HEADS, NSA_GROUP, LANE))
    gate4 = zg_s[:db, :3 * NSA_HEADS].reshape(db, NSA_KV_HEADS, NSA_GROUP, 3)
    o_cmp4 = o_cmp_s[:, 0].reshape(db, NSA_KV_HEADS, NSA_GROUP, HEAD_DIM)
    o_s = win_attention_sample(q4, win3, kv_new[:, 4], kv_new[:, 5], bias_w, bias0, o_cmp4, o_sel_s, gate4, ni)
    rows_s = zs[:, nq:nq + 4 * NSA_KV].reshape(db, DEC_SEQ, 4, NSA_KV_HEADS, HEAD_DIM)
    new_win = zs[:, nq + 4 * NSA_KV:nq + 6 * NSA_KV].reshape(db, DEC_SEQ, 2, NSA_KV_HEADS, HEAD_DIM)
    win_all = jnp.concatenate([state_nsa_win[ni], new_win], axis=1)
    keep_s = min(WINDOW, wb + DEC_SEQ)
    win_s = win_all[:, wb + DEC_SEQ - keep_s:]
    return o_p.reshape(bt, nq), o_s.reshape(db, nq), rows_p, rows_s, win_p, win_s


def kernel(x_prompt, x_sample, c_prompt, c_sample, state_ret, cache_nsa, state_nsa_win, page_table, rel_bias,
           ada_w, ada_b, ln_g, ln_b, ret_w_in, ret_gn_w, ret_w_out, nsa_w_in, nsa_cmp_pe, nsa_cmp_w1, nsa_cmp_b1,
           nsa_cmp_w2, nsa_cmp_b2, nsa_w_out, moe_w_router, moe_b_router, moe_w_up, moe_b_up, moe_w_down, moe_b_down):
    b, t, d = x_prompt.shape
    db, ds, _ = x_sample.shape
    assert ds == DEC_SEQ == 1 and db <= SUBLANE and t % ROW_TILE == 0 and t % QBLK == 0
    bt = b * t
    npad = bt + ROW_TILE
    ntok = bt + db
    tiles_per_seq = t // ROW_TILE
    past_len = page_table.shape[1] * PAGE_SIZE

    x_all = jnp.zeros((npad, d), F32).at[:bt].set(x_prompt.reshape(bt, d)).at[bt:ntok].set(x_sample.reshape(db, d))
    rc = -(-(b + db) // SUBLANE) * SUBLANE
    c_all = jnp.zeros((rc, d), F32).at[:b].set(c_prompt).at[b:b + db].set(c_sample)
    mods = ada_all(c_all, ada_w, ada_b)

    def mod_slabs(li, sub):
        m = mods[li * 2 + sub]
        return tuple(_slabs(m[:, k * d:(k + 1) * d], b, db) for k in range(3))

    def tail_rows(a_s, width, dtype):
        return jnp.zeros((ROW_TILE, width), dtype).at[:db].set(a_s.astype(dtype))

    ret_sp, ret_ss, rows_p, rows_s, win_p, win_s = [], [], [], [], [], []
    for li in range(DEPTH):
        sh, sc, gt = mod_slabs(li, 0)
        h_all = modulate(x_all, sc, sh, tiles_per_seq, b)
        if li % N_MIXERS == 0:
            ri = li // N_MIXERS
            z_p, z_s = dense_prompt_and_sample(h_all, h_all, ret_w_in, (ri,), bt)
            a_p, sp = retention(z_p.reshape(b, t, RET_IN), jnp.arange(t), None, ret_gn_w, ri, min(RET_CHUNK, t) if t % RET_CHUNK == 0 else t)
            zs3 = jnp.zeros((db, SUBLANE, RET_IN), F32).at[:, 0].set(z_s[:db])
            a_s, ss = retention(zs3, jnp.full((SUBLANE,), past_len), state_ret, ret_gn_w, ri, DEC_SEQ)
            ret_sp.append(sp)
            ret_ss.append(ss)
            nv = RET_HEADS * RET_DV
            y_all = dense_prompt_and_sample(a_p.reshape(bt, nv), tail_rows(a_s[:, 0], nv, BF16), ret_w_out, (ri,), bt,
                                            into_rows=npad)
        else:
            ni = li // N_MIXERS
            o_p, o_s, rp, rs, wp, wsb = nsa_layer(h_all, bt, b, t, db, ni, cache_nsa, state_nsa_win, page_table, rel_bias,
                                                  nsa_w_in, nsa_cmp_pe, nsa_cmp_w1, nsa_cmp_b1, nsa_cmp_w2, nsa_cmp_b2)
            rows_p.append(rp)
            rows_s.append(rs)
            win_p.append(wp)
            win_s.append(wsb)
            nq = NSA_HEADS * HEAD_DIM
            y_all = dense_prompt_and_sample(o_p, tail_rows(o_s, nq, BF16), nsa_w_out, (ni,), bt, into_rows=npad)
        x_all = resid_ln(x_all, y_all, gt, ln_g, ln_b, li, 0, tiles_per_seq, b)

        sh, sc, gt = mod_slabs(li, 1)
        h_f32, logits = modulate_router(x_all, sc, sh, moe_w_router, moe_b_router, li, tiles_per_seq, b)
        y_all = moe(h_f32, logits, ntok, li, moe_w_up, moe_b_up, moe_w_down, moe_b_down)
        x_all = resid_ln(x_all, y_all, gt, ln_g, ln_b, li, 1, tiles_per_seq, b)

    y_prompt = x_all[:bt].reshape(b, t, d)
    y_sample = x_all[bt:ntok].reshape(db, ds, d)
    return (y_prompt, y_sample, jnp.stack(ret_sp), jnp.stack(ret_ss), jnp.stack(rows_p, axis=1),
            jnp.stack(rows_s, axis=1), jnp.stack(win_p), jnp.stack(win_s))
```

```python
import functools
import math

import jax
import jax.numpy as jnp
from jax import lax
from jax.experimental import pallas as pl
from jax.experimental.pallas import tpu as pltpu

D_MODEL = 4096
BATCH = 4
SEQ = 2048
DEPTH = 2
DEC_BATCH = 8
DEC_SEQ = 1
PAST_LEN = 16384
PAGE_SIZE = 128

N_MIXERS = 2
N_RET_LAYERS = (DEPTH + 1) // 2
N_NSA_LAYERS = DEPTH // 2

DN_ALPHA = (2.0 * DEPTH) ** 0.25
LN_EPS = 1e-5
NEG_INF = -1e30
NEG_SCORE = -1e9

RET_HEADS = 16
RET_DK = D_MODEL // RET_HEADS
RET_DV = 2 * RET_DK
RET_CHUNK = 128
ROPE_BASE = 10000.0

NSA_HEADS = 32
HEAD_DIM = D_MODEL // NSA_HEADS
NSA_KV_HEADS = 4
NSA_GROUP = NSA_HEADS // NSA_KV_HEADS
NSA_KV = NSA_KV_HEADS * HEAD_DIM
CMP_LEN = 32
CMP_STRIDE = 16
CMP_HIDDEN = 2 * HEAD_DIM
SEL_BLK = 64
SEL_TOPK = 16
WINDOW = 512
QBLK = 128
FORCE_BONUS = 1e6

REL_BUCKETS = 32
REL_MAX_DIST = 128

N_EXPERTS = 32
TOP_K = 4
D_EXPERT = D_MODEL // 2
SWIGLU_ALPHA = 1.702
SWIGLU_LIMIT = 7.0

RET_IN = 2 * RET_HEADS * RET_DK + 2 * RET_HEADS * RET_DV
NSA_IN = NSA_HEADS * HEAD_DIM + 6 * NSA_KV + 3 * NSA_HEADS

F32 = jnp.float32
BF16 = jnp.bfloat16
HI = lax.Precision.HIGHEST

LANE = 128
SUBLANE = 8
VMEM_LIMIT_BYTES = 56 * 1024 * 1024

ROW_TILE = 256
MOE_TM = 256
RET_HPS = 4
MOE_UP_CHUNK = 512
MOE_GATHER_ROWS = 256
MOE_COMBINE_ROWS = 128
CMP_PAGES = 8


def _cp(*sem, vmem=VMEM_LIMIT_BYTES):
    return pltpu.CompilerParams(dimension_semantics=sem, vmem_limit_bytes=vmem)


def _dot(a, b):
    return jnp.dot(a, b, preferred_element_type=F32)


def _dot_nt(a, b):
    return lax.dot_general(a, b, (((1,), (1,)), ((), ())), preferred_element_type=F32)


def _dot_tn(a, b):
    return lax.dot_general(a, b, (((0,), (0,)), ((), ())), preferred_element_type=F32)


def _pick(n, *cands):
    for c in cands:
        if n % c == 0:
            return c
    return n


def _ada_kernel(c_ref, w_ref, b_ref, o_ref):
    c = c_ref[...]
    s = (c / (1.0 + jnp.exp(-c))).astype(BF16)
    o_ref[...] = _dot(s, w_ref[...].astype(BF16)) + b_ref[...]


def ada_all(c_all, ada_w, ada_b):
    rc, d = c_all.shape
    nl = ada_w.shape[0] * ada_w.shape[1]
    w = ada_w.reshape(nl, d, 3 * d)
    b = ada_b.reshape(nl, 1, 3 * d)
    tn = _pick(3 * d, 512, 256, 128)
    return pl.pallas_call(
        _ada_kernel,
        out_shape=jax.ShapeDtypeStruct((nl, rc, 3 * d), F32),
        grid=(nl, 3 * d // tn),
        in_specs=[pl.BlockSpec((rc, d), lambda l, j: (0, 0)),
                  pl.BlockSpec((None, d, tn), lambda l, j: (l, 0, j)),
                  pl.BlockSpec((None, 1, tn), lambda l, j: (l, 0, j))],
        out_specs=pl.BlockSpec((None, rc, tn), lambda l, j: (l, 0, j)),
        compiler_params=_cp("arbitrary", "arbitrary"),
        name="ada_mod",
    )(c_all, w, b)


def _slabs(mod, nb, ndb):
    d = mod.shape[-1]
    p = jnp.broadcast_to(mod[:nb, None, :], (nb, SUBLANE, d))
    s = jnp.zeros((1, SUBLANE, d), F32).at[0, :ndb].set(mod[nb:nb + ndb])
    return jnp.concatenate([p, s], axis=0)


def _mod_kernel(x_ref, sc_ref, sh_ref, o_ref):
    tm, d = x_ref.shape
    x = x_ref[...].reshape(tm // SUBLANE, SUBLANE, d)
    h = x * (1.0 + sc_ref[...][None]) + sh_ref[...][None]
    o_ref[...] = h.reshape(tm, d).astype(o_ref.dtype)


def _mod_router_kernel(x_ref, sc_ref, sh_ref, wr_ref, br_ref, o_ref, lg_ref):
    tm, d = x_ref.shape
    x = x_ref[...].reshape(tm // SUBLANE, SUBLANE, d)
    h = (x * (1.0 + sc_ref[...][None]) + sh_ref[...][None]).reshape(tm, d)
    o_ref[...] = h
    lg_ref[...] = jnp.dot(h, wr_ref[...], precision=HI, preferred_element_type=F32) + br_ref[...]


def _slab_spec(d, tiles_per_seq, nb):
    return pl.BlockSpec((None, SUBLANE, d), lambda i: (jnp.minimum(i // tiles_per_seq, nb), 0, 0))


def modulate(x_all, sc, sh, tiles_per_seq, nb):
    n, d = x_all.shape
    return pl.pallas_call(
        _mod_kernel,
        out_shape=jax.ShapeDtypeStruct((n, d), BF16),
        grid=(n // ROW_TILE,),
        in_specs=[pl.BlockSpec((ROW_TILE, d), lambda i: (i, 0)),
                  _slab_spec(d, tiles_per_seq, nb), _slab_spec(d, tiles_per_seq, nb)],
        out_specs=pl.BlockSpec((ROW_TILE, d), lambda i: (i, 0)),
        compiler_params=_cp("arbitrary"),
        name="modulate",
    )(x_all, sc, sh)


def modulate_router(x_all, sc, sh, w_r, b_r, li, tiles_per_seq, nb):
    n, d = x_all.shape
    ne = w_r.shape[-1]
    b_r3 = b_r.reshape(b_r.shape[0], 1, ne)
    return pl.pallas_call(
        _mod_router_kernel,
        out_shape=(jax.ShapeDtypeStruct((n, d), F32), jax.ShapeDtypeStruct((n, ne), F32)),
        grid=(n // ROW_TILE,),
        in_specs=[pl.BlockSpec((ROW_TILE, d), lambda i: (i, 0)),
                  _slab_spec(d, tiles_per_seq, nb), _slab_spec(d, tiles_per_seq, nb),
                  pl.BlockSpec((None, d, ne), lambda i: (li, 0, 0)),
                  pl.BlockSpec((None, 1, ne), lambda i: (li, 0, 0))],
        out_specs=(pl.BlockSpec((ROW_TILE, d), lambda i: (i, 0)),
                   pl.BlockSpec((ROW_TILE, ne), lambda i: (i, 0))),
        compiler_params=_cp("arbitrary"),
        name="modulate_router",
    )(x_all, sc, sh, w_r, b_r3)


def _resid_ln_kernel(x_ref, y_ref, gt_ref, g_ref, b_ref, o_ref):
    tm, d = x_ref.shape
    x = x_ref[...].reshape(tm // SUBLANE, SUBLANE, d)
    y = y_ref[...].reshape(tm // SUBLANE, SUBLANE, d)
    v = (DN_ALPHA * x + gt_ref[...][None] * y).reshape(tm, d)
    mu = jnp.mean(v, axis=-1, keepdims=True)
    c = v - mu
    var = jnp.mean(c * c, axis=-1, keepdims=True)
    o_ref[...] = c * lax.rsqrt(var + LN_EPS) * g_ref[...] + b_ref[...]


def resid_ln(x_all, y_all, gt, ln_g, ln_b, li, sub, tiles_per_seq, nb):
    n, d = x_all.shape
    g3 = ln_g.reshape(ln_g.shape[0] * ln_g.shape[1], 1, d)
    b3 = ln_b.reshape(ln_b.shape[0] * ln_b.shape[1], 1, d)
    idx = li * 2 + sub
    return pl.pallas_call(
        _resid_ln_kernel,
        out_shape=jax.ShapeDtypeStruct((n, d), F32),
        grid=(n // ROW_TILE,),
        in_specs=[pl.BlockSpec((ROW_TILE, d), lambda i: (i, 0)),
                  pl.BlockSpec((ROW_TILE, d), lambda i: (i, 0)),
                  _slab_spec(d, tiles_per_seq, nb),
                  pl.BlockSpec((None, 1, d), lambda i: (idx, 0, 0)),
                  pl.BlockSpec((None, 1, d), lambda i: (idx, 0, 0))],
        out_specs=pl.BlockSpec((ROW_TILE, d), lambda i: (i, 0)),
        compiler_params=_cp("arbitrary"),
        name="resid_ln",
    )(x_all, y_all, gt, g3, b3)


def _mm_kernel(x_ref, w_ref, o_ref):
    @pl.when(pl.program_id(2) == 0)
    def _():
        o_ref[...] = jnp.zeros_like(o_ref)

    o_ref[...] += _dot(x_ref[...], w_ref[...].astype(BF16))


def _mm_into_kernel(prev_ref, x_ref, w_ref, o_ref):
    del prev_ref
    _mm_kernel(x_ref, w_ref, o_ref)


def matmul(x, w, wsel, *, row0, rows, tm, tn, tk, ncols=None, out_rows=None, out_row0=0, into=None):
    kdim = x.shape[1]
    n = w.shape[-1] if ncols is None else ncols
    assert n % tn == 0
    nj = n // tn
    assert rows % tm == 0 and row0 % tm == 0 and out_row0 % tm == 0 and kdim % tk == 0
    rb0, ob0 = row0 // tm, out_row0 // tm
    out_rows = rows if out_rows is None else out_rows
    nlead = len(wsel)
    in_specs = [pl.BlockSpec((tm, tk), lambda i, j, k: (rb0 + i, k)),
                pl.BlockSpec((None,) * nlead + (tk, tn), lambda i, j, k: tuple(wsel) + (k, j))]
    args = [x, w]
    kern, aliases = _mm_kernel, {}
    if into is not None:
        assert into.shape == (out_rows, nj * tn)
        in_specs = [pl.BlockSpec(memory_space=pl.ANY)] + in_specs
        args = [into] + args
        kern, aliases = _mm_into_kernel, {0: 0}
    return pl.pallas_call(
        kern,
        out_shape=jax.ShapeDtypeStruct((out_rows, nj * tn), F32),
        grid=(rows // tm, nj, kdim // tk),
        in_specs=in_specs,
        out_specs=pl.BlockSpec((tm, tn), lambda i, j, k: (ob0 + i, j)),
        input_output_aliases=aliases,
        compiler_params=_cp("arbitrary", "arbitrary", "arbitrary"),
        name="matmul",
    )(*args)


def dense_prompt_and_sample(x_p, x_s, w, wsel, n_prompt, *, ncols=None, into_rows=None):
    kdim = x_p.shape[1]
    n = w.shape[-1] if ncols is None else ncols
    tm = _pick(n_prompt, 2048, 1024, 512, 256)
    tn = _pick(n, 1024, 512, 256, 128)
    tk = _pick(kdim, 1024, 512, 256)
    s_row0 = x_s.shape[0] - ROW_TILE
    if into_rows is None:
        z_p = matmul(x_p, w, wsel, row0=0, rows=n_prompt, tm=tm, tn=tn, tk=tk, ncols=ncols)
        z_s = matmul(x_s, w, wsel, row0=s_row0, rows=ROW_TILE, tm=ROW_TILE, tn=tn, tk=tk, ncols=ncols)
        return z_p, z_s
    y = jnp.zeros((into_rows, n), F32)
    y = matmul(x_p, w, wsel, row0=0, rows=n_prompt, tm=tm, tn=tn, tk=tk, ncols=ncols, out_rows=into_rows, into=y)
    return matmul(x_s, w, wsel, row0=s_row0, rows=ROW_TILE, tm=ROW_TILE, tn=tn, tk=tk, ncols=ncols,
                  out_rows=into_rows, out_row0=n_prompt, into=y)


def _ret_kernel(*refs, has_s0, nchunks):
    (q_ref, k_ref, v_ref, g_ref, cos_ref, sin_ref, dec_ref, qd_ref, kd_ref, cd_ref, gn_ref), rest = refs[:11], refs[11:]
    if has_s0:
        s0_ref, a_ref, so_ref, s_scr = rest
    else:
        a_ref, so_ref, s_scr = rest
    c = pl.program_id(2)

    @pl.when(c == 0)
    def _():
        s_scr[...] = s0_ref[...] if has_s0 else jnp.zeros_like(s_scr)

    half = RET_DK // 2
    cos = cos_ref[...]
    sin = sin_ref[...]

    def rot(x):
        x1, x2 = x[:, :half], x[:, half:]
        return jnp.concatenate([x1 * cos - x2 * sin, x1 * sin + x2 * cos], axis=-1)

    for hp in range(s_scr.shape[0]):
        ks = slice(hp * RET_DK, (hp + 1) * RET_DK)
        vs = slice(hp * RET_DV, (hp + 1) * RET_DV)
        q = rot(q_ref[:, ks])
        k = rot(k_ref[:, ks]) * (RET_DK ** -0.5)
        qb = q.astype(BF16)
        vb = v_ref[:, vs].astype(BF16)
        s = s_scr[hp]
        att = _dot_nt(qb, k.astype(BF16)) * dec_ref[hp]
        o = _dot(att.astype(BF16), vb) + _dot(qb, s.astype(BF16)) * qd_ref[hp]
        s_new = s * cd_ref[hp] + _dot_tn((k * kd_ref[hp]).astype(BF16), vb)
        s_scr[hp] = s_new
        mu = jnp.mean(o, axis=-1, keepdims=True)
        oc = o - mu
        var = jnp.mean(oc * oc, axis=-1, keepdims=True)
        on = oc * lax.rsqrt(var + LN_EPS) * gn_ref[:, vs]
        g = g_ref[:, vs]
        a_ref[:, vs] = ((g / (1.0 + jnp.exp(-g))) * on).astype(BF16)

    @pl.when(c == nchunks - 1)
    def _():
        so_ref[...] = s_scr[...]


def retention(z3, pos, s0, gn_w, ri, true_chunk):
    b, tpad, _ = z3.shape
    cpad = min(RET_CHUNK, tpad)
    nchunks = tpad // cpad
    nk, nv = RET_HEADS * RET_DK, RET_HEADS * RET_DV
    half = RET_DK // 2
    inv = 1.0 / (ROPE_BASE ** (jnp.arange(half, dtype=F32) / half))
    ang = pos.astype(F32)[:, None] * inv[None, :]
    cos, sin = jnp.cos(ang), jnp.sin(ang)
    lg = jnp.log1p(-jnp.exp2(-5.0 - jnp.arange(RET_HEADS, dtype=F32)))
    i = jnp.arange(cpad, dtype=F32)
    diff = i[:, None] - i[None, :]
    decay = jnp.where(diff >= 0, jnp.exp(jnp.maximum(diff, 0.0)[None] * lg[:, None, None]), 0.0)
    q_dec = jnp.exp((i + 1.0)[None, :] * lg[:, None])[:, :, None]
    k_dec = jnp.exp((true_chunk - 1.0 - i)[None, :] * lg[:, None])[:, :, None]
    c_dec = jnp.exp(true_chunk * lg)[:, None, None]
    hps = math.gcd(RET_HPS, RET_HEADS)
    kq, kv = hps * RET_DK, hps * RET_DV
    in_specs = [
        pl.BlockSpec((None, cpad, kq), lambda bi, h, c: (bi, c, h)),
        pl.BlockSpec((None, cpad, kq), lambda bi, h, c: (bi, c, nk // kq + h)),
        pl.BlockSpec((None, cpad, kv), lambda bi, h, c: (bi, c, 2 * nk // kv + h)),
        pl.BlockSpec((None, cpad, kv), lambda bi, h, c: (bi, c, (2 * nk + nv) // kv + h)),
        pl.BlockSpec((cpad, half), lambda bi, h, c: (c, 0)),
        pl.BlockSpec((cpad, half), lambda bi, h, c: (c, 0)),
        pl.BlockSpec((hps, cpad, cpad), lambda bi, h, c: (h, 0, 0)),
        pl.BlockSpec((hps, cpad, 1), lambda bi, h, c: (h, 0, 0)),
        pl.BlockSpec((hps, cpad, 1), lambda bi, h, c: (h, 0, 0)),
        pl.BlockSpec((hps, 1, 1), lambda bi, h, c: (h, 0, 0)),
        pl.BlockSpec((None, 1, kv), lambda bi, h, c: (ri, 0, h)),
    ]
    args = [z3, z3, z3, z3, cos, sin, decay, q_dec, k_dec, c_dec, gn_w.reshape(gn_w.shape[0], 1, nv)]
    if s0 is not None:
        in_specs.append(pl.BlockSpec((None, None, hps, RET_DK, RET_DV), lambda bi, h, c: (ri, bi, h, 0, 0)))
        args.append(s0)
    return pl.pallas_call(
        functools.partial(_ret_kernel, has_s0=s0 is not None, nchunks=nchunks),
        out_shape=(jax.ShapeDtypeStruct((b, tpad, nv), BF16),
                   jax.ShapeDtypeStruct((b, RET_HEADS, RET_DK, RET_DV), F32)),
        grid=(b, RET_HEADS // hps, nchunks),
        in_specs=in_specs,
        out_specs=(pl.BlockSpec((None, cpad, kv), lambda bi, h, c: (bi, c, h)),
                   pl.BlockSpec((None, hps, RET_DK, RET_DV), lambda bi, h, c: (bi, h, 0, 0))),
        scratch_shapes=[pltpu.VMEM((hps, RET_DK, RET_DV), F32)],
        compiler_params=_cp("arbitrary", "arbitrary", "arbitrary"),
        name="retention",
    )(*args)


DMA_UNROLL = 8


def _moe_gather_kernel(tok_ref, nu_ref, h_hbm, o_ref, buf, sem, *, rb):
    i = pl.program_id(0)
    n_steps = nu_ref[0] * MOE_TM // rb

    def row_copy(step, r):
        slot = step % 2
        return pltpu.make_async_copy(h_hbm.at[pl.ds(tok_ref[step * rb + r], 1)], buf.at[slot, pl.ds(r, 1)], sem.at[slot])

    def start_block(step):
        def body(r8, carry):
            for u in range(DMA_UNROLL):
                row_copy(step, r8 * DMA_UNROLL + u).start(priority=u % 2)
            return carry

        lax.fori_loop(0, rb // DMA_UNROLL, body, 0)

    def wait_block(step):
        def body(r8, carry):
            for u in range(DMA_UNROLL):
                row_copy(step, r8 * DMA_UNROLL + u).wait()
            return carry

        lax.fori_loop(0, rb // DMA_UNROLL, body, 0)

    @pl.when(jnp.logical_and(i == 0, n_steps > 0))
    def _():
        start_block(0)

    @pl.when(i + 1 < n_steps)
    def _():
        start_block(i + 1)

    @pl.when(i < n_steps)
    def _():
        wait_block(i)
        o_ref[...] = buf[i % 2].astype(BF16)

    @pl.when(i >= n_steps)
    def _():
        o_ref[...] = jnp.zeros_like(o_ref)


def moe_gather(h_all, row_tok, n_used):
    rows = row_tok.shape[0]
    d = h_all.shape[1]
    rb = MOE_GATHER_ROWS
    return pl.pallas_call(
        functools.partial(_moe_gather_kernel, rb=rb),
        out_shape=jax.ShapeDtypeStruct((rows, d), BF16),
        grid_spec=pltpu.PrefetchScalarGridSpec(
            num_scalar_prefetch=2,
            grid=(rows // rb,),
            in_specs=[pl.BlockSpec(memory_space=pl.ANY)],
            out_specs=pl.BlockSpec((rb, d), lambda i, tok, nu: (i, 0)),
            scratch_shapes=[pltpu.VMEM((2, rb, d), F32), pltpu.SemaphoreType.DMA((2,))]),
        compiler_params=_cp("arbitrary"),
        name="moe_gather",
    )(row_tok, n_used, h_all)


def _moe_block_paths(blk, first_ref, half_ref, nu_ref, w_ref, o_ref, wbf, compute_rows):
    used = blk < nu_ref[0]

    @pl.when(jnp.logical_and(used, first_ref[blk] == 1))
    def _():
        wbf[...] = w_ref[...].astype(BF16)

    @pl.when(jnp.logical_and(used, half_ref[blk] == 0))
    def _():
        compute_rows(MOE_TM)

    @pl.when(jnp.logical_and(used, half_ref[blk] == 1))
    def _():
        compute_rows(MOE_TM // 2)
        o_ref[MOE_TM // 2:, :] = jnp.zeros((MOE_TM // 2, o_ref.shape[1]), o_ref.dtype)

    @pl.when(jnp.logical_not(used))
    def _():
        o_ref[...] = jnp.zeros_like(o_ref)


def _moe_up_kernel(be_ref, first_ref, half_ref, nu_ref, x_ref, w_ref, b_ref, o_ref, wbf, *, tn):
    cw = MOE_UP_CHUNK
    nchunk = tn // cw

    def compute_rows(nr):
        x = x_ref[:nr, :]
        pick_even = (lax.broadcasted_iota(jnp.int32, (2 * LANE, LANE), 0)
                     == 2 * lax.broadcasted_iota(jnp.int32, (2 * LANE, LANE), 1)).astype(BF16)

        def up(c):
            cols = slice(c * cw, (c + 1) * cw)
            return _dot(x, wbf[:, cols]) + b_ref[:, cols]

        def activate(hu):
            lin = pltpu.roll(hu, cw - 1, 1)
            glu = jnp.minimum(hu, SWIGLU_LIMIT)
            linc = jnp.clip(lin, -SWIGLU_LIMIT, SWIGLU_LIMIT)
            return ((glu / (1.0 + jnp.exp(-SWIGLU_ALPHA * glu))) * (linc + 1.0)).astype(BF16)

        hu = up(0)
        for c in range(nchunk):
            hu_next = up(c + 1) if c + 1 < nchunk else None
            act = activate(hu)
            for s in range(cw // (2 * LANE)):
                o0 = (c * cw // (2 * LANE) + s) * LANE
                o_ref[:nr, o0:o0 + LANE] = _dot(act[:, s * 2 * LANE:(s + 1) * 2 * LANE], pick_even).astype(BF16)
            hu = hu_next

    _moe_block_paths(pl.program_id(1), first_ref, half_ref, nu_ref, w_ref, o_ref, wbf, compute_rows)


def _moe_down_kernel(be_ref, first_ref, half_ref, nu_ref, x_ref, w_ref, b_ref, o_ref, wbf):
    def compute_rows(nr):
        o_ref[:nr, :] = _dot(x_ref[:nr, :], wbf[...]) + b_ref[...]

    _moe_block_paths(pl.program_id(1), first_ref, half_ref, nu_ref, w_ref, o_ref, wbf, compute_rows)


def _moe_grouped(kern, x, w, bias, li, blk_e, first, half, n_used, tn, out_cols, out_tn, out_dtype, name):
    rows, kdim = x.shape
    n = w.shape[-1]
    n_blk = rows // MOE_TM
    b4 = bias.reshape(bias.shape[0], bias.shape[1], 1, n)

    def bc(blk, nu):
        return jnp.minimum(blk, nu[0] - 1)

    return pl.pallas_call(
        kern,
        out_shape=jax.ShapeDtypeStruct((rows, out_cols), out_dtype),
        grid_spec=pltpu.PrefetchScalarGridSpec(
            num_scalar_prefetch=4,
            grid=(n // tn, n_blk),
            in_specs=[pl.BlockSpec((MOE_TM, kdim), lambda j, blk, be, fi, ha, nu: (bc(blk, nu), 0)),
                      pl.BlockSpec((None, None, kdim, tn), lambda j, blk, be, fi, ha, nu: (li, be[bc(blk, nu)], 0, j)),
                      pl.BlockSpec((None, None, 1, tn), lambda j, blk, be, fi, ha, nu: (li, be[bc(blk, nu)], 0, j))],
            out_specs=pl.BlockSpec((MOE_TM, out_tn), lambda j, blk, be, fi, ha, nu: (blk, j)),
            scratch_shapes=[pltpu.VMEM((kdim, tn), BF16)]),
        compiler_params=_cp("arbitrary", "arbitrary"),
        name=name,
    )(blk_e, first, half, n_used, x, w, b4)


def _moe_combine_kernel(pos_ref, tv_ref, yb_hbm, o_ref, buf, sem, *, tt):
    i = pl.program_id(0)
    n_steps = pl.num_programs(0)
    rows_per_iter = DMA_UNROLL // TOP_K if DMA_UNROLL >= TOP_K else 1

    def row_copy(step, r, k):
        slot = step % 2
        return pltpu.make_async_copy(yb_hbm.at[pl.ds(pos_ref[(step * tt + r) * TOP_K + k], 1)],
                                     buf.at[slot, k, pl.ds(r, 1)], sem.at[slot])

    def start_tile(step):
        def body(rr, carry):
            for u in range(rows_per_iter):
                for k in range(TOP_K):
                    row_copy(step, rr * rows_per_iter + u, k).start(priority=k % 2)
            return carry

        lax.fori_loop(0, tt // rows_per_iter, body, 0)

    def wait_tile(step):
        def body(rr, carry):
            for u in range(rows_per_iter):
                for k in range(TOP_K):
                    row_copy(step, rr * rows_per_iter + u, k).wait()
            return carry

        lax.fori_loop(0, tt // rows_per_iter, body, 0)

    @pl.when(i == 0)
    def _():
        start_tile(0)

    @pl.when(i + 1 < n_steps)
    def _():
        start_tile(i + 1)

    wait_tile(i)
    slot = i % 2
    tv = tv_ref[...]
    e = jnp.exp(tv - jnp.max(tv, axis=-1, keepdims=True))
    gate = e / jnp.sum(e, axis=-1, keepdims=True)
    acc = gate[:, 0:1] * buf[slot, 0]
    for k in range(1, TOP_K):
        acc = acc + gate[:, k:k + 1] * buf[slot, k]
    o_ref[...] = acc


def moe_combine(yb, pos_flat, top_v):
    n = top_v.shape[0]
    d = yb.shape[1]
    tt = MOE_COMBINE_ROWS
    return pl.pallas_call(
        functools.partial(_moe_combine_kernel, tt=tt),
        out_shape=jax.ShapeDtypeStruct((n, d), F32),
        grid_spec=pltpu.PrefetchScalarGridSpec(
            num_scalar_prefetch=1,
            grid=(n // tt,),
            in_specs=[pl.BlockSpec((tt, TOP_K), lambda i, pos: (i, 0)),
                      pl.BlockSpec(memory_space=pl.ANY)],
            out_specs=pl.BlockSpec((tt, d), lambda i, pos: (i, 0)),
            scratch_shapes=[pltpu.VMEM((2, TOP_K, tt, d), F32), pltpu.SemaphoreType.DMA((2,))]),
        compiler_params=_cp("arbitrary"),
        name="moe_combine",
    )(pos_flat, top_v, yb)


def moe(h_all, logits, ntok, li, w_up, b_up, w_down, b_down):
    npad, d = h_all.shape
    tm = MOE_TM
    top_v, top_e = lax.top_k(logits[:ntok], TOP_K)
    n_asg = ntok * TOP_K
    n_blk = (n_asg + N_EXPERTS * (tm - 1) + tm - 1) // tm
    rows = n_blk * tm
    flat_e = top_e.reshape(n_asg)
    onehot = (jnp.arange(N_EXPERTS, dtype=jnp.int32)[:, None] == flat_e[None, :]).astype(jnp.int32)
    prefix = jnp.cumsum(onehot, axis=1)
    counts = prefix[:, -1]
    pcounts = (counts + tm - 1) // tm * tm
    pend = jnp.cumsum(pcounts)
    pstart = pend - pcounts
    dest = jnp.sum(onehot * (prefix - 1 + pstart[:, None]), axis=0).astype(jnp.int32)
    row_tok = jnp.zeros((rows,), jnp.int32).at[dest].set(jnp.arange(n_asg, dtype=jnp.int32) // TOP_K)
    pos = jnp.zeros((npad * TOP_K,), jnp.int32).at[:n_asg].set(dest)
    blk_e = jnp.minimum(jnp.sum((pend[None, :] <= (jnp.arange(n_blk) * tm)[:, None]).astype(jnp.int32), axis=1),
                        N_EXPERTS - 1).astype(jnp.int32)
    first = jnp.concatenate([jnp.ones((1,), jnp.int32), (blk_e[1:] != blk_e[:-1]).astype(jnp.int32)])
    n_used = (pend[-1] // tm).astype(jnp.int32).reshape(1)
    row_end = jnp.sum((jnp.arange(N_EXPERTS)[None, :] == blk_e[:, None]) * (pstart + counts)[None, :], axis=1)
    half = (row_end - jnp.arange(n_blk) * tm <= tm // 2).astype(jnp.int32)
    tv_pad = jnp.zeros((npad, TOP_K), F32).at[:ntok].set(top_v)

    xs = moe_gather(h_all, row_tok, n_used)
    tn_up = _pick(2 * D_EXPERT, 1024, 512, 256)
    act = _moe_grouped(functools.partial(_moe_up_kernel, tn=tn_up), xs, w_up, b_up, li, blk_e, first, half, n_used,
                       tn_up, D_EXPERT, tn_up // 2, BF16, "moe_up")
    tn_dn = _pick(d, 2048, 1024, 512, 256)
    yb = _moe_grouped(_moe_down_kernel, act, w_down, b_down, li, blk_e, first, half, n_used,
                      tn_dn, d, tn_dn, F32, "moe_down")
    return moe_combine(yb, pos, tv_pad)


def _t5_bucket(dist):
    n = jnp.maximum(dist, 0)
    exact = REL_BUCKETS // 2
    nf = jnp.maximum(n, 1).astype(F32)
    large = exact + (jnp.log(nf / exact) / math.log(REL_MAX_DIST / exact) * (REL_BUCKETS - exact)).astype(jnp.int32)
    large = jnp.minimum(large, REL_BUCKETS - 1)
    return jnp.where(n < exact, n, large)


def _bias_of_dist(dist, table):
    onehot = (_t5_bucket(dist)[None] == jnp.arange(REL_BUCKETS).reshape((REL_BUCKETS,) + (1,) * dist.ndim)).astype(F32)
    return jnp.tensordot(table.astype(F32).T, onehot, axes=((1,), (0,)), precision=HI)


def _cmp_geometry(tk):
    ncb = (tk - CMP_LEN) // CMP_STRIDE + 1
    nch = -(-(ncb + CMP_LEN // CMP_STRIDE - 1) // (CMP_PAGES * SUBLANE)) * (CMP_PAGES * SUBLANE)
    nsb = -(-tk // SEL_BLK)
    nsbp = -(-nsb // LANE) * LANE
    return ncb, nch, nsb, nsbp


def _cmpa_kernel(*refs, npage):
    page_refs, w_ref, o_ref, pg = refs[1:npage + 1], refs[npage + 1], refs[npage + 2], refs[npage + 3]
    cpp = PAGE_SIZE // CMP_STRIDE
    m_rows = NSA_KV_HEADS * npage * cpp
    blk = 2 * HEAD_DIM
    per = CMP_LEN // CMP_STRIDE
    for p in range(npage):
        for slot in range(2):
            for g in range(NSA_KV_HEADS):
                pg[p, slot * NSA_KV_HEADS + g] = page_refs[p][:, slot, g, :]
    for slot in range(2):
        accs = [jnp.zeros((m_rows, CMP_HIDDEN), F32) for _ in range(per)]
        for s2 in range(CMP_STRIDE // 2):
            parts = []
            for g in range(NSA_KV_HEADS):
                for p in range(npage):
                    c = slot * NSA_KV_HEADS + g
                    x0 = pg[p, c, pl.ds(2 * s2, cpp, stride=CMP_STRIDE), :]
                    x1 = pg[p, c, pl.ds(2 * s2 + 1, cpp, stride=CMP_STRIDE), :]
                    parts.append(jnp.concatenate([x0, x1], axis=1))
            xm = jnp.concatenate(parts, axis=0).astype(BF16)
            for m in range(per):
                w = w_ref[slot, pl.ds(m * CMP_STRIDE * HEAD_DIM + s2 * blk, blk), :].astype(BF16)
                accs[m] = accs[m] + _dot(xm, w)
        o_ref[slot] = jnp.concatenate(accs, axis=1).reshape(NSA_KV_HEADS, npage * cpp, per * CMP_HIDDEN)


def compress_chunks(src6, page_rows, li, w1, ni, nb, nch):
    npg = page_rows.shape[1]
    cpp = PAGE_SIZE // CMP_STRIDE
    steps = nch // (CMP_PAGES * cpp)
    per = CMP_LEN // CMP_STRIDE
    idx = jnp.minimum(jnp.arange(steps * CMP_PAGES), npg - 1)
    pr = page_rows[:, idx].reshape(-1).astype(jnp.int32)

    def page_spec(p):
        return pl.BlockSpec((None, None, PAGE_SIZE, 2, NSA_KV_HEADS, HEAD_DIM),
                            lambda b, s, pr_ref: (pr_ref[(b * steps + s) * CMP_PAGES + p], li, 0, 0, 0, 0))

    return pl.pallas_call(
        functools.partial(_cmpa_kernel, npage=CMP_PAGES),
        out_shape=jax.ShapeDtypeStruct((2, nb, NSA_KV_HEADS, nch, per * CMP_HIDDEN), F32),
        grid_spec=pltpu.PrefetchScalarGridSpec(
            num_scalar_prefetch=1,
            grid=(nb, steps),
            in_specs=[page_spec(p) for p in range(CMP_PAGES)]
            + [pl.BlockSpec((None, 2, CMP_LEN * HEAD_DIM, CMP_HIDDEN), lambda b, s, pr_ref: (ni, 0, 0, 0))],
            out_specs=pl.BlockSpec((2, None, NSA_KV_HEADS, CMP_PAGES * cpp, per * CMP_HIDDEN),
                                   lambda b, s, pr_ref: (0, b, 0, s, 0)),
            scratch_shapes=[pltpu.VMEM((CMP_PAGES, 2 * NSA_KV_HEADS, PAGE_SIZE, HEAD_DIM), F32)]),
        compiler_params=_cp("arbitrary", "arbitrary"),
        name="compress_chunks",
    )(pr, *([src6] * CMP_PAGES), w1)


def _cmpb_kernel(a_ref, pe_ref, w1_ref, b1_ref, w2_ref, b2_ref, o_ref):
    nch = a_ref.shape[0]
    pew = _dot(pe_ref[...].astype(BF16), w1_ref[...].astype(BF16))[0:1]
    a = a_ref[...]
    hid = b1_ref[...] + pew
    hid = hid + a[:, :CMP_HIDDEN]
    hid = hid + pltpu.roll(a[:, CMP_HIDDEN:], nch - 1, 0)
    act = jax.nn.gelu(hid, approximate=True)
    o_ref[...] = _dot(act.astype(BF16), w2_ref[...].astype(BF16)) + b2_ref[...]


def compress_blocks(a, pe, w1, b1, w2, b2, ni):
    _, nb, g, nch, _ = a.shape
    assert CMP_LEN // CMP_STRIDE == 2
    pe8 = jnp.broadcast_to(pe.reshape(pe.shape[0], 2, 1, CMP_LEN * HEAD_DIM), (pe.shape[0], 2, SUBLANE, CMP_LEN * HEAD_DIM))
    return pl.pallas_call(
        _cmpb_kernel,
        out_shape=jax.ShapeDtypeStruct((2, nb, nch, g * HEAD_DIM), F32),
        grid=(2, nb, g),
        in_specs=[pl.BlockSpec((None, None, None, nch, 2 * CMP_HIDDEN), lambda s, b, gi: (s, b, gi, 0, 0)),
                  pl.BlockSpec((None, None, SUBLANE, CMP_LEN * HEAD_DIM), lambda s, b, gi: (ni, s, 0, 0)),
                  pl.BlockSpec((None, None, CMP_LEN * HEAD_DIM, CMP_HIDDEN), lambda s, b, gi: (ni, s, 0, 0)),
                  pl.BlockSpec((None, None, 1, CMP_HIDDEN), lambda s, b, gi: (ni, s, 0, 0)),
                  pl.BlockSpec((None, None, CMP_HIDDEN, HEAD_DIM), lambda s, b, gi: (ni, s, 0, 0)),
                  pl.BlockSpec((None, None, 1, HEAD_DIM), lambda s, b, gi: (ni, s, 0, 0))],
        out_specs=pl.BlockSpec((None, None, nch, HEAD_DIM), lambda s, b, gi: (s, b, 0, gi)),
        compiler_params=_cp("arbitrary", "arbitrary", "arbitrary"),
        name="compress_blocks",
    )(a, pe8, w1, b1.reshape(b1.shape[0], 2, 1, CMP_HIDDEN), w2, b2.reshape(b2.shape[0], 2, 1, HEAD_DIM))


def _cmp_attn_kernel(q_ref, kc_ref, vc_ref, bias_ref, cov_ref, o_ref, sc_ref):
    kc = kc_ref[...].astype(BF16)
    vc = vc_ref[...].astype(BF16)
    scale = HEAD_DIM ** -0.5
    tq = q_ref.shape[0]
    nch = kc.shape[0]
    qall = jnp.concatenate([q_ref[:, r * HEAD_DIM:(r + 1) * HEAD_DIM] for r in range(NSA_GROUP)], axis=0).astype(BF16)
    lg = _dot_nt(qall, kc) * scale + bias_ref[...].reshape(NSA_GROUP * tq, nch)
    mx = jnp.max(lg, axis=-1, keepdims=True)
    e = jnp.exp(lg - mx)
    p = e / jnp.sum(e, axis=-1, keepdims=True) * (mx > 0.1 * NEG_INF).astype(F32)
    o = _dot(p.astype(BF16), vc)
    for r in range(NSA_GROUP):
        o_ref[:, r * HEAD_DIM:(r + 1) * HEAD_DIM] = o[r * tq:(r + 1) * tq]
    p_heads = jnp.sum(p.reshape(NSA_GROUP, tq, nch), axis=0)
    sc_ref[...] = jnp.dot(p_heads, cov_ref[...], precision=HI, preferred_element_type=F32)


def cmp_attention(z3, kvc, bias, cover, tq):
    b, t, _ = z3.shape
    nch = kvc.shape[2]
    nsbp = cover.shape[1]
    gw = NSA_GROUP * HEAD_DIM
    return pl.pallas_call(
        _cmp_attn_kernel,
        out_shape=(jax.ShapeDtypeStruct((b, t, NSA_HEADS * HEAD_DIM), F32),
                   jax.ShapeDtypeStruct((b, NSA_KV_HEADS, t, nsbp), F32)),
        grid=(b, NSA_KV_HEADS, t // tq),
        in_specs=[pl.BlockSpec((None, tq, gw), lambda bi, g, i: (bi, i, g)),
                  pl.BlockSpec((None, None, nch, HEAD_DIM), lambda bi, g, i: (0, bi, 0, g)),
                  pl.BlockSpec((None, None, nch, HEAD_DIM), lambda bi, g, i: (1, bi, 0, g)),
                  pl.BlockSpec((NSA_GROUP, tq, nch), lambda bi, g, i: (g, i, 0)),
                  pl.BlockSpec((nch, nsbp), lambda bi, g, i: (0, 0))],
        out_specs=(pl.BlockSpec((None, tq, gw), lambda bi, g, i: (bi, i, g)),
                   pl.BlockSpec((None, None, tq, nsbp), lambda bi, g, i: (bi, g, i, 0))),
        compiler_params=_cp("arbitrary", "arbitrary", "arbitrary"),
        name="cmp_attention",
    )(z3, kvc, kvc, bias, cover)


def cmp_tables(q_pos, tk, nch, nsbp, table):
    ncb = (tk - CMP_LEN) // CMP_STRIDE + 1
    nsb = -(-tk // SEL_BLK)
    n = jnp.arange(nch)
    dist = q_pos[:, None] - (n * CMP_STRIDE + CMP_LEN - 1)[None, :]
    vis = (dist >= 0) & (n < ncb)[None, :]
    bias = jnp.where(vis[None], _bias_of_dist(dist, table), NEG_INF)
    j = jnp.arange(nsbp)[None, :]
    i = n[:, None]
    cover = ((i * CMP_STRIDE < (j + 1) * SEL_BLK) & (i * CMP_STRIDE + CMP_LEN > j * SEL_BLK)
             & (i < ncb) & (j < nsb)).astype(F32)
    return bias, cover


def select_blocks(score, q_pos, nsb):
    j = jnp.arange(nsb)[None, :]
    cur = (q_pos // SEL_BLK)[:, None]
    forced = (j == 0) | (j == cur) | (j == cur - 1)
    s = score[..., :nsb]
    s = jnp.where((j > cur)[None, None], NEG_SCORE, s + jnp.where(forced, FORCE_BONUS, 0.0)[None, None])
    _, idx = lax.top_k(s, min(SEL_TOPK, nsb))
    return idx


def _nsa_attn_kernel(q_ref, ks_ref, vs_ref, kw_ref, vw_ref, msk_ref, tb_ref, oc_ref, gt_ref, o_ref):
    i = pl.program_id(2)
    tq = QBLK
    rows = NSA_GROUP * tq
    scale = HEAD_DIM ** -0.5
    qall = jnp.concatenate([q_ref[:, r * HEAD_DIM:(r + 1) * HEAD_DIM] for r in range(NSA_GROUP)], axis=0).astype(BF16)
    kpos0 = lax.broadcasted_iota(jnp.int32, (tq, tq), 0)
    qpos = i * tq + lax.broadcasted_iota(jnp.int32, (tq, tq), 1)
    selm_t = msk_ref[...]
    nsbp = selm_t.shape[0]
    per_blk = tq // SEL_BLK
    srow = lax.broadcasted_iota(jnp.int32, (tq, nsbp), 0) // SEL_BLK
    jcol = lax.broadcasted_iota(jnp.int32, (tq, nsbp), 1)
    pass_cols = rows

    def sel_valid(m):
        expand_t = (jcol == per_blk * m + srow).astype(F32)
        picked = _dot(expand_t, selm_t)
        return (picked > 0.5) & (m * tq + kpos0 <= qpos)

    def win_valid(m):
        dist = qpos - (m * tq + kpos0)
        return (dist >= 0) & (dist < WINDOW)

    def branch(k_ref, v_ref, lo, valid_fn):
        def body(m, carry):
            mx, l, acc = carry
            r0 = pl.multiple_of(m * tq, tq)
            k = k_ref[pl.ds(r0, tq), :].astype(BF16)
            v_t = v_ref[pl.ds(r0, tq), :].T.astype(BF16)
            valid = valid_fn(m)
            valid = jnp.concatenate([valid] * (pass_cols // tq), axis=1)
            bias = tb_ref[jnp.minimum(i - m, 2)]
            mx_o, l_o, acc_o = [], [], []
            for c in range(rows // pass_cols):
                cs = slice(c * pass_cols, (c + 1) * pass_cols)
                s = _dot_nt(k, qall[cs]) * scale + bias[:, cs]
                s = jnp.where(valid, s, NEG_INF)
                mx_new = jnp.maximum(mx[:, cs], jnp.max(s, axis=0, keepdims=True))
                alpha = jnp.exp(mx[:, cs] - mx_new)
                p = jnp.exp(s - mx_new)
                mx_o.append(mx_new)
                l_o.append(alpha * l[:, cs] + jnp.sum(p, axis=0, keepdims=True))
                acc_o.append(alpha * acc[:, cs] + _dot(v_t, p.astype(BF16)))
            return jnp.concatenate(mx_o, axis=1), jnp.concatenate(l_o, axis=1), jnp.concatenate(acc_o, axis=1)

        init = (jnp.full((1, rows), NEG_INF, F32), jnp.zeros((1, rows), F32), jnp.zeros((HEAD_DIM, rows), F32))
        _, l, acc = lax.fori_loop(lo, i + 1, body, init)
        return acc / l

    o_sel = branch(ks_ref, vs_ref, 0, sel_valid)
    o_win = branch(kw_ref, vw_ref, jnp.maximum(i - WINDOW // tq, 0), win_valid)
    gt = gt_ref[...]
    gates = 1.0 / (1.0 + jnp.exp(-gt))
    for r in range(NSA_GROUP):
        sl = slice(r * HEAD_DIM, (r + 1) * HEAD_DIM)
        o = (gates[:, 3 * r:3 * r + 1] * oc_ref[:, sl]
             + gates[:, 3 * r + 1:3 * r + 2] * o_sel[:, r * tq:(r + 1) * tq].T
             + gates[:, 3 * r + 2:3 * r + 3] * o_win[:, r * tq:(r + 1) * tq].T)
        o_ref[:, sl] = o.astype(BF16)


def nsa_attention_prompt(z3, sel_mask, tb, o_cmp, gate_lin):
    b, t, _ = z3.shape
    gw = NSA_GROUP * HEAD_DIM
    nsbp = sel_mask.shape[-2]
    kv0 = NSA_HEADS * HEAD_DIM // HEAD_DIM
    g4 = NSA_KV_HEADS

    def kv_spec(slot):
        return pl.BlockSpec((None, t, HEAD_DIM), lambda bi, g, i: (bi, 0, kv0 + slot * g4 + g))

    return pl.pallas_call(
        _nsa_attn_kernel,
        out_shape=jax.ShapeDtypeStruct((b, t, NSA_HEADS * HEAD_DIM), BF16),
        grid=(b, NSA_KV_HEADS, t // QBLK),
        in_specs=[pl.BlockSpec((None, QBLK, gw), lambda bi, g, i: (bi, i, g)),
                  kv_spec(2), kv_spec(3), kv_spec(4), kv_spec(5),
                  pl.BlockSpec((None, None, nsbp, QBLK), lambda bi, g, i: (bi, g, 0, i)),
                  pl.BlockSpec((3, None, QBLK, NSA_GROUP * QBLK), lambda bi, g, i: (0, g, 0, 0)),
                  pl.BlockSpec((None, QBLK, gw), lambda bi, g, i: (bi, i, g)),
                  pl.BlockSpec((None, None, QBLK, 3 * NSA_GROUP), lambda bi, g, i: (bi, g, i, 0))],
        out_specs=pl.BlockSpec((None, QBLK, gw), lambda bi, g, i: (bi, i, g)),
        compiler_params=_cp("arbitrary", "arbitrary", "arbitrary"),
        name="nsa_attention_prompt",
    )(z3, z3, z3, z3, z3, sel_mask, tb, o_cmp, gate_lin)


def _sel_sample_kernel(rb_ref, js_ref, q_ref, *refs, jlast, ksel):
    ng = NSA_KV_HEADS
    k_refs, v_refs, b_refs = refs[:ng], refs[ng:2 * ng], refs[2 * ng + 2:3 * ng + 2]
    kn_ref, vn_ref = refs[2 * ng], refs[2 * ng + 1]
    o_ref, m_s, l_s, a_s = refs[3 * ng + 2:]
    bi, kk = pl.program_id(0), pl.program_id(1)
    scale = HEAD_DIM ** -0.5

    @pl.when(kk == 0)
    def _():
        m_s[...] = jnp.full_like(m_s, NEG_INF)
        l_s[...] = jnp.zeros_like(l_s)
        a_s[...] = jnp.zeros_like(a_s)

    first_row = lax.broadcasted_iota(jnp.int32, (SEL_BLK, HEAD_DIM), 0) == 0
    for g in range(ng):
        is_new = js_ref[(bi * ng + g) * ksel + kk] == jlast
        k = jnp.where(is_new, jnp.where(first_row, kn_ref[g], 0.0), k_refs[g][:, g, :])
        v = jnp.where(is_new, jnp.where(first_row, vn_ref[g], 0.0), v_refs[g][:, g, :])
        s = _dot_nt(q_ref[g].astype(BF16), k.astype(BF16)) * scale + b_refs[g][...]
        valid = jnp.logical_or(jnp.logical_not(is_new), lax.broadcasted_iota(jnp.int32, s.shape, 1) == 0)
        s = jnp.where(valid, s, NEG_INF)
        mx = m_s[g]
        mx_new = jnp.maximum(mx, jnp.max(s, axis=-1, keepdims=True))
        alpha = jnp.exp(mx - mx_new)
        p = jnp.exp(s - mx_new)
        m_s[g] = mx_new
        l_s[g] = alpha * l_s[g] + jnp.sum(p, axis=-1, keepdims=True)
        a_s[g] = alpha * a_s[g] + _dot(p.astype(BF16), v.astype(BF16))

    @pl.when(kk == ksel - 1)
    def _():
        o_ref[...] = a_s[...] / l_s[...]


def sel_attention_sample(q4, cache, row_blk, jsel, k_new, v_new, bias, ni, jlast):
    db, g, r, hd = q4.shape
    ksel = jsel.shape[-1]
    per_page = PAGE_SIZE // SEL_BLK

    def blk_spec(slot, gi):
        def index(bi, kk, rb, js):
            blk = rb[(bi * g + gi) * ksel + kk]
            return (blk // per_page, ni, blk % per_page, slot, 0, 0)

        return pl.BlockSpec((None, None, SEL_BLK, None, g, HEAD_DIM), index)

    def bias_spec(gi):
        return pl.BlockSpec((None, None, r, SEL_BLK), lambda bi, kk, rb, js: (gi, js[(bi * g + gi) * ksel + kk], 0, 0))

    per_seq = pl.BlockSpec((None, g, r, hd), lambda bi, kk, rb, js: (bi, 0, 0, 0))
    new_row = pl.BlockSpec((None, g, 1, hd), lambda bi, kk, rb, js: (bi, 0, 0, 0))
    return pl.pallas_call(
        functools.partial(_sel_sample_kernel, jlast=jlast, ksel=ksel),
        out_shape=jax.ShapeDtypeStruct((db, g, r, hd), F32),
        grid_spec=pltpu.PrefetchScalarGridSpec(
            num_scalar_prefetch=2,
            grid=(db, ksel),
            in_specs=[per_seq] + [blk_spec(2, gi) for gi in range(g)] + [blk_spec(3, gi) for gi in range(g)]
            + [new_row, new_row] + [bias_spec(gi) for gi in range(g)],
            out_specs=per_seq,
            scratch_shapes=[pltpu.VMEM((g, r, 1), F32), pltpu.VMEM((g, r, 1), F32), pltpu.VMEM((g, r, hd), F32)]),
        compiler_params=_cp("arbitrary", "arbitrary"),
        name="sel_attention_sample",
    )(row_blk.reshape(-1), jsel.reshape(-1), q4, *([cache] * (2 * g)), k_new, v_new, *([bias] * g))


def _win_sample_kernel(q_ref, kb_ref, vb_ref, kn_ref, vn_ref, bw_ref, b0_ref, oc_ref, os_ref, gt_ref, o_ref):
    scale = HEAD_DIM ** -0.5
    q = q_ref[...]
    wb = kb_ref.shape[0]
    s = _dot_nt(q.astype(BF16), kb_ref[...].astype(BF16)) * scale + bw_ref[...]
    dist = wb - lax.broadcasted_iota(jnp.int32, s.shape, 1)
    s = jnp.where(dist < WINDOW, s, NEG_INF)
    s_new = jnp.sum(q * kn_ref[...], axis=-1, keepdims=True) * scale + b0_ref[...][:, 0:1]
    mx = jnp.maximum(jnp.max(s, axis=-1, keepdims=True), s_new)
    e = jnp.exp(s - mx)
    e_new = jnp.exp(s_new - mx)
    l = jnp.sum(e, axis=-1, keepdims=True) + e_new
    o_win = (_dot(e.astype(BF16), vb_ref[...].astype(BF16)) + e_new * vn_ref[...]) / l
    gates = 1.0 / (1.0 + jnp.exp(-gt_ref[...]))
    o_ref[...] = gates[:, 0:1] * oc_ref[...] + gates[:, 1:2] * os_ref[...] + gates[:, 2:3] * o_win


def win_attention_sample(q4, win3, k_new, v_new, bias_w, bias0, o_cmp4, o_sel4, gate4, ni):
    db, g, r, hd = q4.shape
    wb = win3.shape[1]

    def small(shape_last):
        return pl.BlockSpec((None, None, r, shape_last), lambda bi, gi: (bi, gi, 0, 0))

    return pl.pallas_call(
        _win_sample_kernel,
        out_shape=jax.ShapeDtypeStruct((db, g, r, hd), F32),
        grid=(db, g),
        in_specs=[small(hd),
                  pl.BlockSpec((None, wb, hd), lambda bi, gi: (ni * db + bi, 0, gi)),
                  pl.BlockSpec((None, wb, hd), lambda bi, gi: (ni * db + bi, 0, g + gi)),
                  pl.BlockSpec((None, None, 1, hd), lambda bi, gi: (bi, gi, 0, 0)),
                  pl.BlockSpec((None, None, 1, hd), lambda bi, gi: (bi, gi, 0, 0)),
                  pl.BlockSpec((None, r, wb), lambda bi, gi: (gi, 0, 0)),
                  pl.BlockSpec((None, r, LANE), lambda bi, gi: (gi, 0, 0)),
                  small(hd), small(hd), small(3)],
        out_specs=small(hd),
        compiler_params=_cp("arbitrary", "arbitrary"),
        name="win_attention_sample",
    )(q4, win3, win3, k_new, v_new, bias_w, bias0, o_cmp4, o_sel4, gate4)


def nsa_layer(h_all, bt, b, t, db, ni, cache_nsa, state_nsa_win, page_table, rel_bias,
              nsa_w_in, nsa_cmp_pe, nsa_cmp_w1, nsa_cmp_b1, nsa_cmp_w2, nsa_cmp_b2):
    nq = NSA_HEADS * HEAD_DIM
    nqkv = nq + 6 * NSA_KV
    z_p, z_s = dense_prompt_and_sample(h_all, h_all, nsa_w_in, (ni,), bt, ncols=nqkv)
    w_gate = jnp.zeros((h_all.shape[1], LANE), F32).at[:, :3 * NSA_HEADS].set(nsa_w_in[ni, :, nqkv:])
    zg_p, zg_s = dense_prompt_and_sample(h_all, h_all, w_gate, (), bt)
    wz = z_p.shape[1]
    z3 = z_p.reshape(b, t, wz)
    n_layers = cache_nsa.shape[1]
    past_len = page_table.shape[1] * PAGE_SIZE
    pos_p = jnp.arange(t)

    ncb, nch, nsb, nsbp = _cmp_geometry(t)
    page_rows = (jnp.arange(b)[:, None] * (t // PAGE_SIZE) + jnp.arange(t // PAGE_SIZE)[None, :])
    rows_p = z3[:, :, nq:nq + 4 * NSA_KV].reshape(b, t, 4, NSA_KV_HEADS, HEAD_DIM)
    src_p = rows_p.reshape(bt // PAGE_SIZE, 1, PAGE_SIZE, 4, NSA_KV_HEADS, HEAD_DIM)
    a_p = compress_chunks(src_p, page_rows, 0, nsa_cmp_w1, ni, b, nch)
    kvc_p = compress_blocks(a_p, nsa_cmp_pe, nsa_cmp_w1, nsa_cmp_b1, nsa_cmp_w2, nsa_cmp_b2, ni)
    bias_p, cover_p = cmp_tables(pos_p, t, nch, nsbp, rel_bias)
    o_cmp_p, score_p = cmp_attention(z3, kvc_p, bias_p, cover_p, QBLK)
    idx_p = select_blocks(score_p, pos_p, nsb)
    sel_mask = (idx_p[:, :, None, :, :] == jnp.arange(nsbp)[None, None, :, None, None]).any(axis=-1).astype(F32)
    ii = jnp.arange(QBLK)
    tb = _bias_of_dist(jnp.arange(3)[:, None, None] * QBLK + ii[None, None, :] - ii[None, :, None], rel_bias)
    tb = tb.reshape(NSA_KV_HEADS, NSA_GROUP, 3, QBLK, QBLK).transpose((2, 0, 3, 1, 4))
    tb = tb.reshape(3, NSA_KV_HEADS, QBLK, NSA_GROUP * QBLK)
    gate_lin = zg_p[:, :3 * NSA_HEADS].reshape(b, t, NSA_KV_HEADS, 3 * NSA_GROUP).transpose((0, 2, 1, 3))
    o_p = nsa_attention_prompt(z3, sel_mask, tb, o_cmp_p, gate_lin)
    keep = min(WINDOW, t)
    win_p = z3[:, t - keep:, nq + 4 * NSA_KV:nq + 6 * NSA_KV].reshape(b, keep, 2, NSA_KV_HEADS, HEAD_DIM)

    zs = z_s[:db]
    tk = past_len + DEC_SEQ
    ncb_s, nch_s, nsb_s, nsbp_s = _cmp_geometry(tk)
    assert (ncb_s + 1) * CMP_STRIDE <= past_len
    a_s = compress_chunks(cache_nsa, page_table, ni, nsa_cmp_w1, ni, db, nch_s)
    kvc_s = compress_blocks(a_s, nsa_cmp_pe, nsa_cmp_w1, nsa_cmp_b1, nsa_cmp_w2, nsa_cmp_b2, ni)
    pos_s = jnp.full((SUBLANE,), past_len, jnp.int32)
    bias_s, cover_s = cmp_tables(pos_s, tk, nch_s, nsbp_s, rel_bias)
    zs3 = jnp.zeros((db, SUBLANE, wz), F32).at[:, 0].set(zs)
    o_cmp_s, score_s = cmp_attention(zs3, kvc_s, bias_s, cover_s, SUBLANE)
    idx_s = select_blocks(score_s[:, :, 0:1], pos_s[0:1], nsb_s)[:, :, 0]
    jlast = past_len // SEL_BLK
    per_page = PAGE_SIZE // SEL_BLK
    page_of = jnp.take_along_axis(page_table, jnp.minimum(idx_s, jlast - 1).reshape(db, -1) // per_page, axis=1)
    row_blk = (page_of.reshape(idx_s.shape) * per_page + jnp.minimum(idx_s, jlast - 1) % per_page).astype(jnp.int32)
    q4 = zs[:, :nq].reshape(db, NSA_KV_HEADS, NSA_GROUP, HEAD_DIM)
    kv_new = zs[:, nq:nq + 6 * NSA_KV].reshape(db, 6, NSA_KV_HEADS, 1, HEAD_DIM)
    kpos = jnp.arange(nsb_s * SEL_BLK).reshape(nsb_s, SEL_BLK)
    bias_sel = _bias_of_dist(past_len - kpos, rel_bias).reshape(NSA_KV_HEADS, NSA_GROUP, nsb_s, SEL_BLK).transpose((0, 2, 1, 3))
    o_sel_s = sel_attention_sample(q4, cache_nsa, row_blk, idx_s.astype(jnp.int32), kv_new[:, 2], kv_new[:, 3],
                                   bias_sel, ni, jlast)
    wb = state_nsa_win.shape[2]
    win3 = state_nsa_win.reshape(state_nsa_win.shape[0] * db, wb, 2 * NSA_KV)
    bias_w = _bias_of_dist(wb - jnp.arange(wb), rel_bias).reshape(NSA_KV_HEADS, NSA_GROUP, wb)
    bias0 = jnp.broadcast_to(_bias_of_dist(jnp.zeros((1,), jnp.int32), rel_bias).reshape(NSA_KV_HEADS, NSA_GROUP, 1),
                             (NSA_KV_HEADS, NSA_GROUP, LANE))
    gate4 = zg_s[:db, :3 * NSA_HEADS].reshape(db, NSA_KV_HEADS, NSA_GROUP, 3)
    o_cmp4 = o_cmp_s[:, 0].reshape(db, NSA_KV_HEADS, NSA_GROUP, HEAD_DIM)
    o_s = win_attention_sample(q4, win3, kv_new[:, 4], kv_new[:, 5], bias_w, bias0, o_cmp4, o_sel_s, gate4, ni)
    rows_s = zs[:, nq:nq + 4 * NSA_KV].reshape(db, DEC_SEQ, 4, NSA_KV_HEADS, HEAD_DIM)
    new_win = zs[:, nq + 4 * NSA_KV:nq + 6 * NSA_KV].reshape(db, DEC_SEQ, 2, NSA_KV_HEADS, HEAD_DIM)
    win_all = jnp.concatenate([state_nsa_win[ni], new_win], axis=1)
    keep_s = min(WINDOW, wb + DEC_SEQ)
    win_s = win_all[:, wb + DEC_SEQ - keep_s:]
    return o_p.reshape(bt, nq), o_s.reshape(db, nq), rows_p, rows_s, win_p, win_s


def kernel(x_prompt, x_sample, c_prompt, c_sample, state_ret, cache_nsa, state_nsa_win, page_table, rel_bias,
           ada_w, ada_b, ln_g, ln_b, ret_w_in, ret_gn_w, ret_w_out, nsa_w_in, nsa_cmp_pe, nsa_cmp_w1, nsa_cmp_b1,
           nsa_cmp_w2, nsa_cmp_b2, nsa_w_out, moe_w_router, moe_b_router, moe_w_up, moe_b_up, moe_w_down, moe_b_down):
    b, t, d = x_prompt.shape
    db, ds, _ = x_sample.shape
    assert ds == DEC_SEQ == 1 and db <= SUBLANE and t % ROW_TILE == 0 and t % QBLK == 0
    bt = b * t
    npad = bt + ROW_TILE
    ntok = bt + db
    tiles_per_seq = t // ROW_TILE
    past_len = page_table.shape[1] * PAGE_SIZE

    x_all = jnp.zeros((npad, d), F32).at[:bt].set(x_prompt.reshape(bt, d)).at[bt:ntok].set(x_sample.reshape(db, d))
    rc = -(-(b + db) // SUBLANE) * SUBLANE
    c_all = jnp.zeros((rc, d), F32).at[:b].set(c_prompt).at[b:b + db].set(c_sample)
    mods = ada_all(c_all, ada_w, ada_b)

    def mod_slabs(li, sub):
        m = mods[li * 2 + sub]
        return tuple(_slabs(m[:, k * d:(k + 1) * d], b, db) for k in range(3))

    def tail_rows(a_s, width, dtype):
        return jnp.zeros((ROW_TILE, width), dtype).at[:db].set(a_s.astype(dtype))

    ret_sp, ret_ss, rows_p, rows_s, win_p, win_s = [], [], [], [], [], []
    for li in range(DEPTH):
        sh, sc, gt = mod_slabs(li, 0)
        h_all = modulate(x_all, sc, sh, tiles_per_seq, b)
        if li % N_MIXERS == 0:
            ri = li // N_MIXERS
            z_p, z_s = dense_prompt_and_sample(h_all, h_all, ret_w_in, (ri,), bt)
            a_p, sp = retention(z_p.reshape(b, t, RET_IN), jnp.arange(t), None, ret_gn_w, ri, min(RET_CHUNK, t) if t % RET_CHUNK == 0 else t)
            zs3 = jnp.zeros((db, SUBLANE, RET_IN), F32).at[:, 0].set(z_s[:db])
            a_s, ss = retention(zs3, jnp.full((SUBLANE,), past_len), state_ret, ret_gn_w, ri, DEC_SEQ)
            ret_sp.append(sp)
            ret_ss.append(ss)
            nv = RET_HEADS * RET_DV
            y_all = dense_prompt_and_sample(a_p.reshape(bt, nv), tail_rows(a_s[:, 0], nv, BF16), ret_w_out, (ri,), bt,
                                            into_rows=npad)
        else:
            ni = li // N_MIXERS
            o_p, o_s, rp, rs, wp, wsb = nsa_layer(h_all, bt, b, t, db, ni, cache_nsa, state_nsa_win, page_table, rel_bias,
                                                  nsa_w_in, nsa_cmp_pe, nsa_cmp_w1, nsa_cmp_b1, nsa_cmp_w2, nsa_cmp_b2)
            rows_p.append(rp)
            rows_s.append(rs)
            win_p.append(wp)
            win_s.append(wsb)
            nq = NSA_HEADS * HEAD_DIM
            y_all = dense_prompt_and_sample(o_p, tail_rows(o_s, nq, BF16), nsa_w_out, (ni,), bt, into_rows=npad)
        x_all = resid_ln(x_all, y_all, gt, ln_g, ln_b, li, 0, tiles_per_seq, b)

        sh, sc, gt = mod_slabs(li, 1)
        h_f32, logits = modulate_router(x_all, sc, sh, moe_w_router, moe_b_router, li, tiles_per_seq, b)
        y_all = moe(h_f32, logits, ntok, li, moe_w_up, moe_b_up, moe_w_down, moe_b_down)
        x_all = resid_ln(x_all, y_all, gt, ln_g, ln_b, li, 1, tiles_per_seq, b)

    y_prompt = x_all[:bt].reshape(b, t, d)
    y_sample = x_all[bt:ntok].reshape(db, ds, d)
    return (y_prompt, y_sample, jnp.stack(ret_sp), jnp.stack(ret_ss), jnp.stack(rows_p, axis=1),
            jnp.stack(rows_s, axis=1), jnp.stack(win_p), jnp.stack(win_s))
```

```python
import functools
import math

import jax
import jax.numpy as jnp
from jax import lax
from jax.experimental import pallas as pl
from jax.experimental.pallas import tpu as pltpu

D_MODEL = 4096
BATCH = 4
SEQ = 2048
DEPTH = 2
DEC_BATCH = 8
DEC_SEQ = 1
PAST_LEN = 16384
PAGE_SIZE = 128

N_MIXERS = 2
N_RET_LAYERS = (DEPTH + 1) // 2
N_NSA_LAYERS = DEPTH // 2

DN_ALPHA = (2.0 * DEPTH) ** 0.25
LN_EPS = 1e-5
NEG_INF = -1e30
NEG_SCORE = -1e9

RET_HEADS = 16
RET_DK = D_MODEL // RET_HEADS
RET_DV = 2 * RET_DK
RET_CHUNK = 128
ROPE_BASE = 10000.0

NSA_HEADS = 32
HEAD_DIM = D_MODEL // NSA_HEADS
NSA_KV_HEADS = 4
NSA_GROUP = NSA_HEADS // NSA_KV_HEADS
NSA_KV = NSA_KV_HEADS * HEAD_DIM
CMP_LEN = 32
CMP_STRIDE = 16
CMP_HIDDEN = 2 * HEAD_DIM
SEL_BLK = 64
SEL_TOPK = 16
WINDOW = 512
QBLK = 128
FORCE_BONUS = 1e6

REL_BUCKETS = 32
REL_MAX_DIST = 128

N_EXPERTS = 32
TOP_K = 4
D_EXPERT = D_MODEL // 2
SWIGLU_ALPHA = 1.702
SWIGLU_LIMIT = 7.0

RET_IN = 2 * RET_HEADS * RET_DK + 2 * RET_HEADS * RET_DV
NSA_IN = NSA_HEADS * HEAD_DIM + 6 * NSA_KV + 3 * NSA_HEADS

F32 = jnp.float32
BF16 = jnp.bfloat16
HI = lax.Precision.HIGHEST

LANE = 128
SUBLANE = 8
VMEM_LIMIT_BYTES = 56 * 1024 * 1024

ROW_TILE = 256
MOE_TM = 256
RET_HPS = 4
MOE_UP_TN = 1024
MOE_DOWN_TN = 2048
MOE_UP_CHUNK = 512
MOE_GATHER_ROWS = 256
MOE_COMBINE_ROWS = 128
CMP_PAGES = 8


def _cp(*sem, vmem=VMEM_LIMIT_BYTES):
    return pltpu.CompilerParams(dimension_semantics=sem, vmem_limit_bytes=vmem)


def _dot(a, b):
    return jnp.dot(a, b, preferred_element_type=F32)


def _dot_nt(a, b):
    return lax.dot_general(a, b, (((1,), (1,)), ((), ())), preferred_element_type=F32)


def _dot_tn(a, b):
    return lax.dot_general(a, b, (((0,), (0,)), ((), ())), preferred_element_type=F32)


def _pick(n, *cands):
    for c in cands:
        if n % c == 0:
            return c
    return n


def _ada_kernel(c_ref, w_ref, b_ref, o_ref):
    c = c_ref[...]
    s = (c / (1.0 + jnp.exp(-c))).astype(BF16)
    o_ref[...] = _dot(s, w_ref[...].astype(BF16)) + b_ref[...]


def ada_all(c_all, ada_w, ada_b):
    rc, d = c_all.shape
    nl = ada_w.shape[0] * ada_w.shape[1]
    w = ada_w.reshape(nl, d, 3 * d)
    b = ada_b.reshape(nl, 1, 3 * d)
    tn = _pick(3 * d, 512, 256, 128)
    return pl.pallas_call(
        _ada_kernel,
        out_shape=jax.ShapeDtypeStruct((nl, rc, 3 * d), F32),
        grid=(nl, 3 * d // tn),
        in_specs=[pl.BlockSpec((rc, d), lambda l, j: (0, 0)),
                  pl.BlockSpec((None, d, tn), lambda l, j: (l, 0, j)),
                  pl.BlockSpec((None, 1, tn), lambda l, j: (l, 0, j))],
        out_specs=pl.BlockSpec((None, rc, tn), lambda l, j: (l, 0, j)),
        compiler_params=_cp("arbitrary", "arbitrary"),
        name="ada_mod",
    )(c_all, w, b)


def _slabs(mod, nb, ndb):
    d = mod.shape[-1]
    p = jnp.broadcast_to(mod[:nb, None, :], (nb, SUBLANE, d))
    s = jnp.zeros((1, SUBLANE, d), F32).at[0, :ndb].set(mod[nb:nb + ndb])
    return jnp.concatenate([p, s], axis=0)


def _mod_kernel(x_ref, sc_ref, sh_ref, o_ref):
    tm, d = x_ref.shape
    x = x_ref[...].reshape(tm // SUBLANE, SUBLANE, d)
    h = x * (1.0 + sc_ref[...][None]) + sh_ref[...][None]
    o_ref[...] = h.reshape(tm, d).astype(o_ref.dtype)


def _mod_router_kernel(x_ref, sc_ref, sh_ref, wr_ref, br_ref, o_ref, lg_ref):
    tm, d = x_ref.shape
    x = x_ref[...].reshape(tm // SUBLANE, SUBLANE, d)
    h = (x * (1.0 + sc_ref[...][None]) + sh_ref[...][None]).reshape(tm, d)
    o_ref[...] = h
    lg_ref[...] = jnp.dot(h, wr_ref[...], precision=HI, preferred_element_type=F32) + br_ref[...]


def _slab_spec(d, tiles_per_seq, nb):
    return pl.BlockSpec((None, SUBLANE, d), lambda i: (jnp.minimum(i // tiles_per_seq, nb), 0, 0))


def modulate(x_all, sc, sh, tiles_per_seq, nb):
    n, d = x_all.shape
    return pl.pallas_call(
        _mod_kernel,
        out_shape=jax.ShapeDtypeStruct((n, d), BF16),
        grid=(n // ROW_TILE,),
        in_specs=[pl.BlockSpec((ROW_TILE, d), lambda i: (i, 0)),
                  _slab_spec(d, tiles_per_seq, nb), _slab_spec(d, tiles_per_seq, nb)],
        out_specs=pl.BlockSpec((ROW_TILE, d), lambda i: (i, 0)),
        compiler_params=_cp("arbitrary"),
        name="modulate",
    )(x_all, sc, sh)


def modulate_router(x_all, sc, sh, w_r, b_r, li, tiles_per_seq, nb):
    n, d = x_all.shape
    ne = w_r.shape[-1]
    b_r3 = b_r.reshape(b_r.shape[0], 1, ne)
    return pl.pallas_call(
        _mod_router_kernel,
        out_shape=(jax.ShapeDtypeStruct((n, d), F32), jax.ShapeDtypeStruct((n, ne), F32)),
        grid=(n // ROW_TILE,),
        in_specs=[pl.BlockSpec((ROW_TILE, d), lambda i: (i, 0)),
                  _slab_spec(d, tiles_per_seq, nb), _slab_spec(d, tiles_per_seq, nb),
                  pl.BlockSpec((None, d, ne), lambda i: (li, 0, 0)),
                  pl.BlockSpec((None, 1, ne), lambda i: (li, 0, 0))],
        out_specs=(pl.BlockSpec((ROW_TILE, d), lambda i: (i, 0)),
                   pl.BlockSpec((ROW_TILE, ne), lambda i: (i, 0))),
        compiler_params=_cp("arbitrary"),
        name="modulate_router",
    )(x_all, sc, sh, w_r, b_r3)


def _resid_ln_kernel(x_ref, y_ref, gt_ref, g_ref, b_ref, o_ref):
    tm, d = x_ref.shape
    x = x_ref[...].reshape(tm // SUBLANE, SUBLANE, d)
    y = y_ref[...].reshape(tm // SUBLANE, SUBLANE, d)
    v = (DN_ALPHA * x + gt_ref[...][None] * y).reshape(tm, d)
    mu = jnp.mean(v, axis=-1, keepdims=True)
    c = v - mu
    var = jnp.mean(c * c, axis=-1, keepdims=True)
    o_ref[...] = c * lax.rsqrt(var + LN_EPS) * g_ref[...] + b_ref[...]


def resid_ln(x_all, y_all, gt, ln_g, ln_b, li, sub, tiles_per_seq, nb):
    n, d = x_all.shape
    g3 = ln_g.reshape(ln_g.shape[0] * ln_g.shape[1], 1, d)
    b3 = ln_b.reshape(ln_b.shape[0] * ln_b.shape[1], 1, d)
    idx = li * 2 + sub
    return pl.pallas_call(
        _resid_ln_kernel,
        out_shape=jax.ShapeDtypeStruct((n, d), F32),
        grid=(n // ROW_TILE,),
        in_specs=[pl.BlockSpec((ROW_TILE, d), lambda i: (i, 0)),
                  pl.BlockSpec((ROW_TILE, d), lambda i: (i, 0)),
                  _slab_spec(d, tiles_per_seq, nb),
                  pl.BlockSpec((None, 1, d), lambda i: (idx, 0, 0)),
                  pl.BlockSpec((None, 1, d), lambda i: (idx, 0, 0))],
        out_specs=pl.BlockSpec((ROW_TILE, d), lambda i: (i, 0)),
        compiler_params=_cp("arbitrary"),
        name="resid_ln",
    )(x_all, y_all, gt, g3, b3)


def _mm_kernel(x_ref, w_ref, o_ref):
    @pl.when(pl.program_id(2) == 0)
    def _():
        o_ref[...] = jnp.zeros_like(o_ref)

    o_ref[...] += _dot(x_ref[...], w_ref[...].astype(BF16))


def _mm_into_kernel(prev_ref, x_ref, w_ref, o_ref):
    del prev_ref
    _mm_kernel(x_ref, w_ref, o_ref)


def matmul(x, w, wsel, *, row0, rows, tm, tn, tk, ncols=None, out_rows=None, out_row0=0, into=None):
    kdim = x.shape[1]
    n = w.shape[-1] if ncols is None else ncols
    assert n % tn == 0
    nj = n // tn
    assert rows % tm == 0 and row0 % tm == 0 and out_row0 % tm == 0 and kdim % tk == 0
    rb0, ob0 = row0 // tm, out_row0 // tm
    out_rows = rows if out_rows is None else out_rows
    nlead = len(wsel)
    in_specs = [pl.BlockSpec((tm, tk), lambda i, j, k: (rb0 + i, k)),
                pl.BlockSpec((None,) * nlead + (tk, tn), lambda i, j, k: tuple(wsel) + (k, j))]
    args = [x, w]
    kern, aliases = _mm_kernel, {}
    if into is not None:
        assert into.shape == (out_rows, nj * tn)
        in_specs = [pl.BlockSpec(memory_space=pl.ANY)] + in_specs
        args = [into] + args
        kern, aliases = _mm_into_kernel, {0: 0}
    return pl.pallas_call(
        kern,
        out_shape=jax.ShapeDtypeStruct((out_rows, nj * tn), F32),
        grid=(rows // tm, nj, kdim // tk),
        in_specs=in_specs,
        out_specs=pl.BlockSpec((tm, tn), lambda i, j, k: (ob0 + i, j)),
        input_output_aliases=aliases,
        compiler_params=_cp("arbitrary", "arbitrary", "arbitrary"),
        name="matmul",
    )(*args)


def dense_prompt_and_sample(x_p, x_s, w, wsel, n_prompt, *, ncols=None, into_rows=None):
    kdim = x_p.shape[1]
    n = w.shape[-1] if ncols is None else ncols
    tm = _pick(n_prompt, 2048, 1024, 512, 256)
    tn = _pick(n, 1024, 512, 256, 128)
    tk = _pick(kdim, 1024, 512, 256)
    s_row0 = x_s.shape[0] - ROW_TILE
    if into_rows is None:
        z_p = matmul(x_p, w, wsel, row0=0, rows=n_prompt, tm=tm, tn=tn, tk=tk, ncols=ncols)
        z_s = matmul(x_s, w, wsel, row0=s_row0, rows=ROW_TILE, tm=ROW_TILE, tn=tn, tk=tk, ncols=ncols)
        return z_p, z_s
    y = jnp.zeros((into_rows, n), F32)
    y = matmul(x_p, w, wsel, row0=0, rows=n_prompt, tm=tm, tn=tn, tk=tk, ncols=ncols, out_rows=into_rows, into=y)
    return matmul(x_s, w, wsel, row0=s_row0, rows=ROW_TILE, tm=ROW_TILE, tn=tn, tk=tk, ncols=ncols,
                  out_rows=into_rows, out_row0=n_prompt, into=y)


def _ret_kernel(*refs, has_s0, nchunks):
    (q_ref, k_ref, v_ref, g_ref, cos_ref, sin_ref, dec_ref, qd_ref, kd_ref, cd_ref, gn_ref), rest = refs[:11], refs[11:]
    if has_s0:
        s0_ref, a_ref, so_ref, s_scr = rest
    else:
        a_ref, so_ref, s_scr = rest
    c = pl.program_id(2)

    @pl.when(c == 0)
    def _():
        s_scr[...] = s0_ref[...] if has_s0 else jnp.zeros_like(s_scr)

    half = RET_DK // 2
    cos = cos_ref[...]
    sin = sin_ref[...]

    def rot(x):
        x1, x2 = x[:, :half], x[:, half:]
        return jnp.concatenate([x1 * cos - x2 * sin, x1 * sin + x2 * cos], axis=-1)

    for hp in range(s_scr.shape[0]):
        ks = slice(hp * RET_DK, (hp + 1) * RET_DK)
        vs = slice(hp * RET_DV, (hp + 1) * RET_DV)
        q = rot(q_ref[:, ks])
        k = rot(k_ref[:, ks]) * (RET_DK ** -0.5)
        qb = q.astype(BF16)
        vb = v_ref[:, vs].astype(BF16)
        s = s_scr[hp]
        att = _dot_nt(qb, k.astype(BF16)) * dec_ref[hp]
        o = _dot(att.astype(BF16), vb) + _dot(qb, s.astype(BF16)) * qd_ref[hp]
        s_new = s * cd_ref[hp] + _dot_tn((k * kd_ref[hp]).astype(BF16), vb)
        s_scr[hp] = s_new
        mu = jnp.mean(o, axis=-1, keepdims=True)
        oc = o - mu
        var = jnp.mean(oc * oc, axis=-1, keepdims=True)
        on = oc * lax.rsqrt(var + LN_EPS) * gn_ref[:, vs]
        g = g_ref[:, vs]
        a_ref[:, vs] = ((g / (1.0 + jnp.exp(-g))) * on).astype(BF16)

    @pl.when(c == nchunks - 1)
    def _():
        so_ref[...] = s_scr[...]


def retention(z3, pos, s0, gn_w, ri, true_chunk):
    b, tpad, _ = z3.shape
    cpad = min(RET_CHUNK, tpad)
    nchunks = tpad // cpad
    nk, nv = RET_HEADS * RET_DK, RET_HEADS * RET_DV
    half = RET_DK // 2
    inv = 1.0 / (ROPE_BASE ** (jnp.arange(half, dtype=F32) / half))
    ang = pos.astype(F32)[:, None] * inv[None, :]
    cos, sin = jnp.cos(ang), jnp.sin(ang)
    lg = jnp.log1p(-jnp.exp2(-5.0 - jnp.arange(RET_HEADS, dtype=F32)))
    i = jnp.arange(cpad, dtype=F32)
    diff = i[:, None] - i[None, :]
    decay = jnp.where(diff >= 0, jnp.exp(jnp.maximum(diff, 0.0)[None] * lg[:, None, None]), 0.0)
    q_dec = jnp.exp((i + 1.0)[None, :] * lg[:, None])[:, :, None]
    k_dec = jnp.exp((true_chunk - 1.0 - i)[None, :] * lg[:, None])[:, :, None]
    c_dec = jnp.exp(true_chunk * lg)[:, None, None]
    hps = math.gcd(RET_HPS, RET_HEADS)
    kq, kv = hps * RET_DK, hps * RET_DV
    in_specs = [
        pl.BlockSpec((None, cpad, kq), lambda bi, h, c: (bi, c, h)),
        pl.BlockSpec((None, cpad, kq), lambda bi, h, c: (bi, c, nk // kq + h)),
        pl.BlockSpec((None, cpad, kv), lambda bi, h, c: (bi, c, 2 * nk // kv + h)),
        pl.BlockSpec((None, cpad, kv), lambda bi, h, c: (bi, c, (2 * nk + nv) // kv + h)),
        pl.BlockSpec((cpad, half), lambda bi, h, c: (c, 0)),
        pl.BlockSpec((cpad, half), lambda bi, h, c: (c, 0)),
        pl.BlockSpec((hps, cpad, cpad), lambda bi, h, c: (h, 0, 0)),
        pl.BlockSpec((hps, cpad, 1), lambda bi, h, c: (h, 0, 0)),
        pl.BlockSpec((hps, cpad, 1), lambda bi, h, c: (h, 0, 0)),
        pl.BlockSpec((hps, 1, 1), lambda bi, h, c: (h, 0, 0)),
        pl.BlockSpec((None, 1, kv), lambda bi, h, c: (ri, 0, h)),
    ]
    args = [z3, z3, z3, z3, cos, sin, decay, q_dec, k_dec, c_dec, gn_w.reshape(gn_w.shape[0], 1, nv)]
    if s0 is not None:
        in_specs.append(pl.BlockSpec((None, None, hps, RET_DK, RET_DV), lambda bi, h, c: (ri, bi, h, 0, 0)))
        args.append(s0)
    return pl.pallas_call(
        functools.partial(_ret_kernel, has_s0=s0 is not None, nchunks=nchunks),
        out_shape=(jax.ShapeDtypeStruct((b, tpad, nv), BF16),
                   jax.ShapeDtypeStruct((b, RET_HEADS, RET_DK, RET_DV), F32)),
        grid=(b, RET_HEADS // hps, nchunks),
        in_specs=in_specs,
        out_specs=(pl.BlockSpec((None, cpad, kv), lambda bi, h, c: (bi, c, h)),
                   pl.BlockSpec((None, hps, RET_DK, RET_DV), lambda bi, h, c: (bi, h, 0, 0))),
        scratch_shapes=[pltpu.VMEM((hps, RET_DK, RET_DV), F32)],
        compiler_params=_cp("arbitrary", "arbitrary", "arbitrary"),
        name="retention",
    )(*args)


DMA_UNROLL = 8


def _moe_gather_kernel(tok_ref, nu_ref, h_hbm, o_ref, buf, sem, *, rb):
    i = pl.program_id(0)
    n_steps = nu_ref[0] * MOE_TM // rb

    def row_copy(step, r):
        slot = step % 2
        return pltpu.make_async_copy(h_hbm.at[pl.ds(tok_ref[step * rb + r], 1)], buf.at[slot, pl.ds(r, 1)], sem.at[slot])

    def start_block(step):
        def body(r8, carry):
            for u in range(DMA_UNROLL):
                row_copy(step, r8 * DMA_UNROLL + u).start(priority=u % 2)
            return carry

        lax.fori_loop(0, rb // DMA_UNROLL, body, 0)

    def wait_block(step):
        def body(r8, carry):
            for u in range(DMA_UNROLL):
                row_copy(step, r8 * DMA_UNROLL + u).wait()
            return carry

        lax.fori_loop(0, rb // DMA_UNROLL, body, 0)

    @pl.when(jnp.logical_and(i == 0, n_steps > 0))
    def _():
        start_block(0)

    @pl.when(i + 1 < n_steps)
    def _():
        start_block(i + 1)

    @pl.when(i < n_steps)
    def _():
        wait_block(i)
        o_ref[...] = buf[i % 2].astype(BF16)

    @pl.when(i >= n_steps)
    def _():
        o_ref[...] = jnp.zeros_like(o_ref)


def moe_gather(h_all, row_tok, n_used):
    rows = row_tok.shape[0]
    d = h_all.shape[1]
    rb = MOE_GATHER_ROWS
    return pl.pallas_call(
        functools.partial(_moe_gather_kernel, rb=rb),
        out_shape=jax.ShapeDtypeStruct((rows, d), BF16),
        grid_spec=pltpu.PrefetchScalarGridSpec(
            num_scalar_prefetch=2,
            grid=(rows // rb,),
            in_specs=[pl.BlockSpec(memory_space=pl.ANY)],
            out_specs=pl.BlockSpec((rb, d), lambda i, tok, nu: (i, 0)),
            scratch_shapes=[pltpu.VMEM((2, rb, d), F32), pltpu.SemaphoreType.DMA((2,))]),
        compiler_params=_cp("arbitrary"),
        name="moe_gather",
    )(row_tok, n_used, h_all)


def _moe_block_paths(meta, li, tn, w_hbm, o_ref, wland, wbf, sem, compute_rows):
    be_ref, first_ref, half_ref, nxe_ref, last_ref, nu_ref = meta
    j, blk = pl.program_id(0), pl.program_id(1)
    used = blk < nu_ref[0]

    def tile_copy(e, jj):
        return pltpu.make_async_copy(w_hbm.at[li, e, :, pl.ds(pl.multiple_of(jj * tn, tn), tn)], wland, sem)

    @pl.when(jnp.logical_and(j == 0, blk == 0))
    def _():
        tile_copy(be_ref[0], j).start()

    @pl.when(jnp.logical_and(used, first_ref[blk] == 1))
    def _():
        tile_copy(be_ref[blk], j).wait()
        wbf[...] = wland[...].astype(BF16)
        j_next = j + last_ref[blk]

        @pl.when(j_next < pl.num_programs(0))
        def _():
            tile_copy(nxe_ref[blk], j_next).start()

    @pl.when(jnp.logical_and(used, half_ref[blk] == 0))
    def _():
        compute_rows(MOE_TM)

    @pl.when(jnp.logical_and(used, half_ref[blk] == 1))
    def _():
        compute_rows(MOE_TM // 2)
        o_ref[MOE_TM // 2:, :] = jnp.zeros((MOE_TM // 2, o_ref.shape[1]), o_ref.dtype)

    @pl.when(jnp.logical_not(used))
    def _():
        o_ref[...] = jnp.zeros_like(o_ref)


def _moe_up_kernel(*refs, tn, li):
    meta, (x_ref, w_hbm, b_ref, o_ref, wland, wbf, sem) = refs[:6], refs[6:]
    cw = min(MOE_UP_CHUNK, tn)
    nchunk = tn // cw

    def compute_rows(nr):
        x = x_ref[:nr, :]
        pick_even = (lax.broadcasted_iota(jnp.int32, (2 * LANE, LANE), 0)
                     == 2 * lax.broadcasted_iota(jnp.int32, (2 * LANE, LANE), 1)).astype(BF16)

        def up(c):
            cols = slice(c * cw, (c + 1) * cw)
            return _dot(x, wbf[:, cols]) + b_ref[:, cols]

        def activate(hu):
            lin = pltpu.roll(hu, cw - 1, 1)
            glu = jnp.minimum(hu, SWIGLU_LIMIT)
            linc = jnp.clip(lin, -SWIGLU_LIMIT, SWIGLU_LIMIT)
            return ((glu / (1.0 + jnp.exp(-SWIGLU_ALPHA * glu))) * (linc + 1.0)).astype(BF16)

        hu = up(0)
        for c in range(nchunk):
            hu_next = up(c + 1) if c + 1 < nchunk else None
            act = activate(hu)
            for s in range(cw // (2 * LANE)):
                o0 = (c * cw // (2 * LANE) + s) * LANE
                o_ref[:nr, o0:o0 + LANE] = _dot(act[:, s * 2 * LANE:(s + 1) * 2 * LANE], pick_even).astype(BF16)
            hu = hu_next

    _moe_block_paths(meta, li, tn, w_hbm, o_ref, wland, wbf, sem, compute_rows)


def _moe_down_kernel(*refs, tn, li):
    meta, (x_ref, w_hbm, b_ref, o_ref, wland, wbf, sem) = refs[:6], refs[6:]

    def compute_rows(nr):
        o_ref[:nr, :] = _dot(x_ref[:nr, :], wbf[...]) + b_ref[...]

    _moe_block_paths(meta, li, tn, w_hbm, o_ref, wland, wbf, sem, compute_rows)


def _moe_grouped(kern, x, w, bias, li, meta, tn, out_cols, out_tn, out_dtype, name):
    rows, kdim = x.shape
    n = w.shape[-1]
    n_blk = rows // MOE_TM
    b4 = bias.reshape(bias.shape[0], bias.shape[1], 1, n)

    def bc(blk, nu):
        return jnp.minimum(blk, nu[0] - 1)

    return pl.pallas_call(
        functools.partial(kern, tn=tn, li=li),
        out_shape=jax.ShapeDtypeStruct((rows, out_cols), out_dtype),
        grid_spec=pltpu.PrefetchScalarGridSpec(
            num_scalar_prefetch=6,
            grid=(n // tn, n_blk),
            in_specs=[pl.BlockSpec((MOE_TM, kdim), lambda j, blk, be, fi, ha, nx, la, nu: (bc(blk, nu), 0)),
                      pl.BlockSpec(memory_space=pl.ANY),
                      pl.BlockSpec((None, None, 1, tn), lambda j, blk, be, fi, ha, nx, la, nu: (li, be[bc(blk, nu)], 0, j))],
            out_specs=pl.BlockSpec((MOE_TM, out_tn), lambda j, blk, be, fi, ha, nx, la, nu: (blk, j)),
            scratch_shapes=[pltpu.VMEM((kdim, tn), F32), pltpu.VMEM((kdim, tn), BF16), pltpu.SemaphoreType.DMA(())]),
        compiler_params=_cp("arbitrary", "arbitrary"),
        name=name,
    )(*meta, x, w, b4)


def _moe_combine_kernel(pos_ref, tv_ref, yb_hbm, o_ref, buf, sem, *, tt):
    i = pl.program_id(0)
    n_steps = pl.num_programs(0)
    rows_per_iter = DMA_UNROLL // TOP_K if DMA_UNROLL >= TOP_K else 1

    def row_copy(step, r, k):
        slot = step % 2
        return pltpu.make_async_copy(yb_hbm.at[pl.ds(pos_ref[(step * tt + r) * TOP_K + k], 1)],
                                     buf.at[slot, k, pl.ds(r, 1)], sem.at[slot])

    def start_tile(step):
        def body(rr, carry):
            for u in range(rows_per_iter):
                for k in range(TOP_K):
                    row_copy(step, rr * rows_per_iter + u, k).start(priority=k % 2)
            return carry

        lax.fori_loop(0, tt // rows_per_iter, body, 0)

    def wait_tile(step):
        def body(rr, carry):
            for u in range(rows_per_iter):
                for k in range(TOP_K):
                    row_copy(step, rr * rows_per_iter + u, k).wait()
            return carry

        lax.fori_loop(0, tt // rows_per_iter, body, 0)

    @pl.when(i == 0)
    def _():
        start_tile(0)

    @pl.when(i + 1 < n_steps)
    def _():
        start_tile(i + 1)

    wait_tile(i)
    slot = i % 2
    tv = tv_ref[...]
    e = jnp.exp(tv - jnp.max(tv, axis=-1, keepdims=True))
    gate = e / jnp.sum(e, axis=-1, keepdims=True)
    acc = gate[:, 0:1] * buf[slot, 0]
    for k in range(1, TOP_K):
        acc = acc + gate[:, k:k + 1] * buf[slot, k]
    o_ref[...] = acc


def moe_combine(yb, pos_flat, top_v):
    n = top_v.shape[0]
    d = yb.shape[1]
    tt = MOE_COMBINE_ROWS
    return pl.pallas_call(
        functools.partial(_moe_combine_kernel, tt=tt),
        out_shape=jax.ShapeDtypeStruct((n, d), F32),
        grid_spec=pltpu.PrefetchScalarGridSpec(
            num_scalar_prefetch=1,
            grid=(n // tt,),
            in_specs=[pl.BlockSpec((tt, TOP_K), lambda i, pos: (i, 0)),
                      pl.BlockSpec(memory_space=pl.ANY)],
            out_specs=pl.BlockSpec((tt, d), lambda i, pos: (i, 0)),
            scratch_shapes=[pltpu.VMEM((2, TOP_K, tt, d), F32), pltpu.SemaphoreType.DMA((2,))]),
        compiler_params=_cp("arbitrary"),
        name="moe_combine",
    )(pos_flat, top_v, yb)


def moe(h_all, logits, ntok, li, w_up, b_up, w_down, b_down):
    npad, d = h_all.shape
    tm = MOE_TM
    top_v, top_e = lax.top_k(logits[:ntok], TOP_K)
    n_asg = ntok * TOP_K
    n_blk = (n_asg + N_EXPERTS * (tm - 1) + tm - 1) // tm
    rows = n_blk * tm
    flat_e = top_e.reshape(n_asg)
    onehot = (jnp.arange(N_EXPERTS, dtype=jnp.int32)[:, None] == flat_e[None, :]).astype(jnp.int32)
    prefix = jnp.cumsum(onehot, axis=1)
    counts = prefix[:, -1]
    pcounts = (counts + tm - 1) // tm * tm
    pend = jnp.cumsum(pcounts)
    pstart = pend - pcounts
    dest = jnp.sum(onehot * (prefix - 1 + pstart[:, None]), axis=0).astype(jnp.int32)
    row_tok = jnp.zeros((rows,), jnp.int32).at[dest].set(jnp.arange(n_asg, dtype=jnp.int32) // TOP_K)
    pos = jnp.zeros((npad * TOP_K,), jnp.int32).at[:n_asg].set(dest)
    blk_e = jnp.minimum(jnp.sum((pend[None, :] <= (jnp.arange(n_blk) * tm)[:, None]).astype(jnp.int32), axis=1),
                        N_EXPERTS - 1).astype(jnp.int32)
    first = jnp.concatenate([jnp.ones((1,), jnp.int32), (blk_e[1:] != blk_e[:-1]).astype(jnp.int32)])
    n_used = (pend[-1] // tm).astype(jnp.int32).reshape(1)
    row_end = jnp.sum((jnp.arange(N_EXPERTS)[None, :] == blk_e[:, None]) * (pstart + counts)[None, :], axis=1)
    half = (row_end - jnp.arange(n_blk) * tm <= tm // 2).astype(jnp.int32)
    tv_pad = jnp.zeros((npad, TOP_K), F32).at[:ntok].set(top_v)
    bidx = jnp.arange(n_blk)
    run_start = jnp.where((first == 1) & (bidx < n_used[0]), bidx, n_blk)
    next_start = jnp.concatenate([lax.cummin(run_start[::-1])[::-1][1:], jnp.full((1,), n_blk, run_start.dtype)])
    is_last = (next_start >= n_blk).astype(jnp.int32)
    onehot_next = (bidx[None, :] == jnp.where(is_last == 1, 0, next_start)[:, None]).astype(jnp.int32)
    next_e = jnp.sum(onehot_next * blk_e[None, :], axis=1).astype(jnp.int32)
    meta = (blk_e, first, half, next_e, is_last, n_used)

    xs = moe_gather(h_all, row_tok, n_used)
    tn_up = _pick(2 * D_EXPERT, MOE_UP_TN, 512, 256)
    act = _moe_grouped(_moe_up_kernel, xs, w_up, b_up, li, meta, tn_up, D_EXPERT, tn_up // 2, BF16, "moe_up")
    tn_dn = _pick(d, MOE_DOWN_TN, 1024, 512, 256)
    yb = _moe_grouped(_moe_down_kernel, act, w_down, b_down, li, meta, tn_dn, d, tn_dn, F32, "moe_down")
    return moe_combine(yb, pos, tv_pad)


def _t5_bucket(dist):
    n = jnp.maximum(dist, 0)
    exact = REL_BUCKETS // 2
    nf = jnp.maximum(n, 1).astype(F32)
    large = exact + (jnp.log(nf / exact) / math.log(REL_MAX_DIST / exact) * (REL_BUCKETS - exact)).astype(jnp.int32)
    large = jnp.minimum(large, REL_BUCKETS - 1)
    return jnp.where(n < exact, n, large)


def _bias_of_dist(dist, table):
    onehot = (_t5_bucket(dist)[None] == jnp.arange(REL_BUCKETS).reshape((REL_BUCKETS,) + (1,) * dist.ndim)).astype(F32)
    return jnp.tensordot(table.astype(F32).T, onehot, axes=((1,), (0,)), precision=HI)


def _cmp_geometry(tk):
    ncb = (tk - CMP_LEN) // CMP_STRIDE + 1
    nch = -(-(ncb + CMP_LEN // CMP_STRIDE - 1) // (CMP_PAGES * SUBLANE)) * (CMP_PAGES * SUBLANE)
    nsb = -(-tk // SEL_BLK)
    nsbp = -(-nsb // LANE) * LANE
    return ncb, nch, nsb, nsbp


def _cmpa_kernel(*refs, npage):
    page_refs, w_ref, o_ref, pg = refs[1:npage + 1], refs[npage + 1], refs[npage + 2], refs[npage + 3]
    cpp = PAGE_SIZE // CMP_STRIDE
    m_rows = NSA_KV_HEADS * npage * cpp
    blk = 2 * HEAD_DIM
    per = CMP_LEN // CMP_STRIDE
    for p in range(npage):
        for slot in range(2):
            for g in range(NSA_KV_HEADS):
                pg[p, slot * NSA_KV_HEADS + g] = page_refs[p][:, slot, g, :]
    for slot in range(2):
        accs = [jnp.zeros((m_rows, CMP_HIDDEN), F32) for _ in range(per)]
        for s2 in range(CMP_STRIDE // 2):
            parts = []
            for g in range(NSA_KV_HEADS):
                for p in range(npage):
                    c = slot * NSA_KV_HEADS + g
                    x0 = pg[p, c, pl.ds(2 * s2, cpp, stride=CMP_STRIDE), :]
                    x1 = pg[p, c, pl.ds(2 * s2 + 1, cpp, stride=CMP_STRIDE), :]
                    parts.append(jnp.concatenate([x0, x1], axis=1))
            xm = jnp.concatenate(parts, axis=0).astype(BF16)
            for m in range(per):
                w = w_ref[slot, pl.ds(m * CMP_STRIDE * HEAD_DIM + s2 * blk, blk), :].astype(BF16)
                accs[m] = accs[m] + _dot(xm, w)
        o_ref[slot] = jnp.concatenate(accs, axis=1).reshape(NSA_KV_HEADS, npage * cpp, per * CMP_HIDDEN)


def compress_chunks(src6, page_rows, li, w1, ni, nb, nch):
    npg = page_rows.shape[1]
    cpp = PAGE_SIZE // CMP_STRIDE
    steps = nch // (CMP_PAGES * cpp)
    per = CMP_LEN // CMP_STRIDE
    idx = jnp.minimum(jnp.arange(steps * CMP_PAGES), npg - 1)
    pr = page_rows[:, idx].reshape(-1).astype(jnp.int32)

    def page_spec(p):
        return pl.BlockSpec((None, None, PAGE_SIZE, 2, NSA_KV_HEADS, HEAD_DIM),
                            lambda b, s, pr_ref: (pr_ref[(b * steps + s) * CMP_PAGES + p], li, 0, 0, 0, 0))

    return pl.pallas_call(
        functools.partial(_cmpa_kernel, npage=CMP_PAGES),
        out_shape=jax.ShapeDtypeStruct((2, nb, NSA_KV_HEADS, nch, per * CMP_HIDDEN), F32),
        grid_spec=pltpu.PrefetchScalarGridSpec(
            num_scalar_prefetch=1,
            grid=(nb, steps),
            in_specs=[page_spec(p) for p in range(CMP_PAGES)]
            + [pl.BlockSpec((None, 2, CMP_LEN * HEAD_DIM, CMP_HIDDEN), lambda b, s, pr_ref: (ni, 0, 0, 0))],
            out_specs=pl.BlockSpec((2, None, NSA_KV_HEADS, CMP_PAGES * cpp, per * CMP_HIDDEN),
                                   lambda b, s, pr_ref: (0, b, 0, s, 0)),
            scratch_shapes=[pltpu.VMEM((CMP_PAGES, 2 * NSA_KV_HEADS, PAGE_SIZE, HEAD_DIM), F32)]),
        compiler_params=_cp("arbitrary", "arbitrary"),
        name="compress_chunks",
    )(pr, *([src6] * CMP_PAGES), w1)


def _cmpb_kernel(a_ref, pe_ref, w1_ref, b1_ref, w2_ref, b2_ref, o_ref):
    nch = a_ref.shape[0]
    pew = _dot(pe_ref[...].astype(BF16), w1_ref[...].astype(BF16))[0:1]
    a = a_ref[...]
    hid = b1_ref[...] + pew
    hid = hid + a[:, :CMP_HIDDEN]
    hid = hid + pltpu.roll(a[:, CMP_HIDDEN:], nch - 1, 0)
    act = jax.nn.gelu(hid, approximate=True)
    o_ref[...] = _dot(act.astype(BF16), w2_ref[...].astype(BF16)) + b2_ref[...]


def compress_blocks(a, pe, w1, b1, w2, b2, ni):
    _, nb, g, nch, _ = a.shape
    assert CMP_LEN // CMP_STRIDE == 2
    pe8 = jnp.broadcast_to(pe.reshape(pe.shape[0], 2, 1, CMP_LEN * HEAD_DIM), (pe.shape[0], 2, SUBLANE, CMP_LEN * HEAD_DIM))
    return pl.pallas_call(
        _cmpb_kernel,
        out_shape=jax.ShapeDtypeStruct((2, nb, nch, g * HEAD_DIM), F32),
        grid=(2, nb, g),
        in_specs=[pl.BlockSpec((None, None, None, nch, 2 * CMP_HIDDEN), lambda s, b, gi: (s, b, gi, 0, 0)),
                  pl.BlockSpec((None, None, SUBLANE, CMP_LEN * HEAD_DIM), lambda s, b, gi: (ni, s, 0, 0)),
                  pl.BlockSpec((None, None, CMP_LEN * HEAD_DIM, CMP_HIDDEN), lambda s, b, gi: (ni, s, 0, 0)),
                  pl.BlockSpec((None, None, 1, CMP_HIDDEN), lambda s, b, gi: (ni, s, 0, 0)),
                  pl.BlockSpec((None, None, CMP_HIDDEN, HEAD_DIM), lambda s, b, gi: (ni, s, 0, 0)),
                  pl.BlockSpec((None, None, 1, HEAD_DIM), lambda s, b, gi: (ni, s, 0, 0))],
        out_specs=pl.BlockSpec((None, None, nch, HEAD_DIM), lambda s, b, gi: (s, b, 0, gi)),
        compiler_params=_cp("arbitrary", "arbitrary", "arbitrary"),
        name="compress_blocks",
    )(a, pe8, w1, b1.reshape(b1.shape[0], 2, 1, CMP_HIDDEN), w2, b2.reshape(b2.shape[0], 2, 1, HEAD_DIM))


def _cmp_attn_kernel(q_ref, kc_ref, vc_ref, bias_ref, cov_ref, o_ref, sc_ref):
    kc = kc_ref[...].astype(BF16)
    vc = vc_ref[...].astype(BF16)
    scale = HEAD_DIM ** -0.5
    tq = q_ref.shape[0]
    nch = kc.shape[0]
    qall = jnp.concatenate([q_ref[:, r * HEAD_DIM:(r + 1) * HEAD_DIM] for r in range(NSA_GROUP)], axis=0).astype(BF16)
    lg = _dot_nt(qall, kc) * scale + bias_ref[...].reshape(NSA_GROUP * tq, nch)
    mx = jnp.max(lg, axis=-1, keepdims=True)
    e = jnp.exp(lg - mx)
    p = e / jnp.sum(e, axis=-1, keepdims=True) * (mx > 0.1 * NEG_INF).astype(F32)
    o = _dot(p.astype(BF16), vc)
    for r in range(NSA_GROUP):
        o_ref[:, r * HEAD_DIM:(r + 1) * HEAD_DIM] = o[r * tq:(r + 1) * tq]
    p_heads = jnp.sum(p.reshape(NSA_GROUP, tq, nch), axis=0)
    sc_ref[...] = jnp.dot(p_heads, cov_ref[...], precision=HI, preferred_element_type=F32)


def cmp_attention(z3, kvc, bias, cover, tq):
    b, t, _ = z3.shape
    nch = kvc.shape[2]
    nsbp = cover.shape[1]
    gw = NSA_GROUP * HEAD_DIM
    return pl.pallas_call(
        _cmp_attn_kernel,
        out_shape=(jax.ShapeDtypeStruct((b, t, NSA_HEADS * HEAD_DIM), F32),
                   jax.ShapeDtypeStruct((b, NSA_KV_HEADS, t, nsbp), F32)),
        grid=(b, NSA_KV_HEADS, t // tq),
        in_specs=[pl.BlockSpec((None, tq, gw), lambda bi, g, i: (bi, i, g)),
                  pl.BlockSpec((None, None, nch, HEAD_DIM), lambda bi, g, i: (0, bi, 0, g)),
                  pl.BlockSpec((None, None, nch, HEAD_DIM), lambda bi, g, i: (1, bi, 0, g)),
                  pl.BlockSpec((NSA_GROUP, tq, nch), lambda bi, g, i: (g, i, 0)),
                  pl.BlockSpec((nch, nsbp), lambda bi, g, i: (0, 0))],
        out_specs=(pl.BlockSpec((None, tq, gw), lambda bi, g, i: (bi, i, g)),
                   pl.BlockSpec((None, None, tq, nsbp), lambda bi, g, i: (bi, g, i, 0))),
        compiler_params=_cp("arbitrary", "arbitrary", "arbitrary"),
        name="cmp_attention",
    )(z3, kvc, kvc, bias, cover)


def cmp_tables(q_pos, tk, nch, nsbp, table):
    ncb = (tk - CMP_LEN) // CMP_STRIDE + 1
    nsb = -(-tk // SEL_BLK)
    n = jnp.arange(nch)
    dist = q_pos[:, None] - (n * CMP_STRIDE + CMP_LEN - 1)[None, :]
    vis = (dist >= 0) & (n < ncb)[None, :]
    bias = jnp.where(vis[None], _bias_of_dist(dist, table), NEG_INF)
    j = jnp.arange(nsbp)[None, :]
    i = n[:, None]
    cover = ((i * CMP_STRIDE < (j + 1) * SEL_BLK) & (i * CMP_STRIDE + CMP_LEN > j * SEL_BLK)
             & (i < ncb) & (j < nsb)).astype(F32)
    return bias, cover


def select_blocks(score, q_pos, nsb):
    j = jnp.arange(nsb)[None, :]
    cur = (q_pos // SEL_BLK)[:, None]
    forced = (j == 0) | (j == cur) | (j == cur - 1)
    s = score[..., :nsb]
    s = jnp.where((j > cur)[None, None], NEG_SCORE, s + jnp.where(forced, FORCE_BONUS, 0.0)[None, None])
    _, idx = lax.top_k(s, min(SEL_TOPK, nsb))
    return idx


def _nsa_attn_kernel(q_ref, ks_ref, vs_ref, kw_ref, vw_ref, msk_ref, tb_ref, oc_ref, gt_ref, o_ref):
    i = pl.program_id(2)
    tq = QBLK
    rows = NSA_GROUP * tq
    scale = HEAD_DIM ** -0.5
    qall = jnp.concatenate([q_ref[:, r * HEAD_DIM:(r + 1) * HEAD_DIM] for r in range(NSA_GROUP)], axis=0).astype(BF16)
    kpos0 = lax.broadcasted_iota(jnp.int32, (tq, tq), 0)
    qpos = i * tq + lax.broadcasted_iota(jnp.int32, (tq, tq), 1)
    selm_t = msk_ref[...]
    nsbp = selm_t.shape[0]
    per_blk = tq // SEL_BLK
    srow = lax.broadcasted_iota(jnp.int32, (tq, nsbp), 0) // SEL_BLK
    jcol = lax.broadcasted_iota(jnp.int32, (tq, nsbp), 1)
    pass_cols = rows

    def sel_valid(m):
        expand_t = (jcol == per_blk * m + srow).astype(F32)
        picked = _dot(expand_t, selm_t)
        return (picked > 0.5) & (m * tq + kpos0 <= qpos)

    def win_valid(m):
        dist = qpos - (m * tq + kpos0)
        return (dist >= 0) & (dist < WINDOW)

    def branch(k_ref, v_ref, lo, valid_fn):
        def body(m, carry):
            mx, l, acc = carry
            r0 = pl.multiple_of(m * tq, tq)
            k = k_ref[pl.ds(r0, tq), :].astype(BF16)
            v_t = v_ref[pl.ds(r0, tq), :].T.astype(BF16)
            valid = valid_fn(m)
            valid = jnp.concatenate([valid] * (pass_cols // tq), axis=1)
            bias = tb_ref[jnp.minimum(i - m, 2)]
            mx_o, l_o, acc_o = [], [], []
            for c in range(rows // pass_cols):
                cs = slice(c * pass_cols, (c + 1) * pass_cols)
                s = _dot_nt(k, qall[cs]) * scale + bias[:, cs]
                s = jnp.where(valid, s, NEG_INF)
                mx_new = jnp.maximum(mx[:, cs], jnp.max(s, axis=0, keepdims=True))
                alpha = jnp.exp(mx[:, cs] - mx_new)
                p = jnp.exp(s - mx_new)
                mx_o.append(mx_new)
                l_o.append(alpha * l[:, cs] + jnp.sum(p, axis=0, keepdims=True))
                acc_o.append(alpha * acc[:, cs] + _dot(v_t, p.astype(BF16)))
            return jnp.concatenate(mx_o, axis=1), jnp.concatenate(l_o, axis=1), jnp.concatenate(acc_o, axis=1)

        init = (jnp.full((1, rows), NEG_INF, F32), jnp.zeros((1, rows), F32), jnp.zeros((HEAD_DIM, rows), F32))
        _, l, acc = lax.fori_loop(lo, i + 1, body, init)
        return acc / l

    o_sel = branch(ks_ref, vs_ref, 0, sel_valid)
    o_win = branch(kw_ref, vw_ref, jnp.maximum(i - WINDOW // tq, 0), win_valid)
    gt = gt_ref[...]
    gates = 1.0 / (1.0 + jnp.exp(-gt))
    for r in range(NSA_GROUP):
        sl = slice(r * HEAD_DIM, (r + 1) * HEAD_DIM)
        o = (gates[:, 3 * r:3 * r + 1] * oc_ref[:, sl]
             + gates[:, 3 * r + 1:3 * r + 2] * o_sel[:, r * tq:(r + 1) * tq].T
             + gates[:, 3 * r + 2:3 * r + 3] * o_win[:, r * tq:(r + 1) * tq].T)
        o_ref[:, sl] = o.astype(BF16)


def nsa_attention_prompt(z3, sel_mask, tb, o_cmp, gate_lin):
    b, t, _ = z3.shape
    gw = NSA_GROUP * HEAD_DIM
    nsbp = sel_mask.shape[-2]
    kv0 = NSA_HEADS * HEAD_DIM // HEAD_DIM
    g4 = NSA_KV_HEADS

    def kv_spec(slot):
        return pl.BlockSpec((None, t, HEAD_DIM), lambda bi, g, i: (bi, 0, kv0 + slot * g4 + g))

    return pl.pallas_call(
        _nsa_attn_kernel,
        out_shape=jax.ShapeDtypeStruct((b, t, NSA_HEADS * HEAD_DIM), BF16),
        grid=(b, NSA_KV_HEADS, t // QBLK),
        in_specs=[pl.BlockSpec((None, QBLK, gw), lambda bi, g, i: (bi, i, g)),
                  kv_spec(2), kv_spec(3), kv_spec(4), kv_spec(5),
                  pl.BlockSpec((None, None, nsbp, QBLK), lambda bi, g, i: (bi, g, 0, i)),
                  pl.BlockSpec((3, None, QBLK, NSA_GROUP * QBLK), lambda bi, g, i: (0, g, 0, 0)),
                  pl.BlockSpec((None, QBLK, gw), lambda bi, g, i: (bi, i, g)),
                  pl.BlockSpec((None, None, QBLK, 3 * NSA_GROUP), lambda bi, g, i: (bi, g, i, 0))],
        out_specs=pl.BlockSpec((None, QBLK, gw), lambda bi, g, i: (bi, i, g)),
        compiler_params=_cp("arbitrary", "arbitrary", "arbitrary"),
        name="nsa_attention_prompt",
    )(z3, z3, z3, z3, z3, sel_mask, tb, o_cmp, gate_lin)


def _sel_sample_kernel(rb_ref, js_ref, q_ref, *refs, jlast, ksel):
    ng = NSA_KV_HEADS
    k_refs, v_refs, b_refs = refs[:ng], refs[ng:2 * ng], refs[2 * ng + 2:3 * ng + 2]
    kn_ref, vn_ref = refs[2 * ng], refs[2 * ng + 1]
    o_ref, m_s, l_s, a_s = refs[3 * ng + 2:]
    bi, kk = pl.program_id(0), pl.program_id(1)
    scale = HEAD_DIM ** -0.5

    @pl.when(kk == 0)
    def _():
        m_s[...] = jnp.full_like(m_s, NEG_INF)
        l_s[...] = jnp.zeros_like(l_s)
        a_s[...] = jnp.zeros_like(a_s)

    first_row = lax.broadcasted_iota(jnp.int32, (SEL_BLK, HEAD_DIM), 0) == 0
    for g in range(ng):
        is_new = js_ref[(bi * ng + g) * ksel + kk] == jlast
        k = jnp.where(is_new, jnp.where(first_row, kn_ref[g], 0.0), k_refs[g][:, g, :])
        v = jnp.where(is_new, jnp.where(first_row, vn_ref[g], 0.0), v_refs[g][:, g, :])
        s = _dot_nt(q_ref[g].astype(BF16), k.astype(BF16)) * scale + b_refs[g][...]
        valid = jnp.logical_or(jnp.logical_not(is_new), lax.broadcasted_iota(jnp.int32, s.shape, 1) == 0)
        s = jnp.where(valid, s, NEG_INF)
        mx = m_s[g]
        mx_new = jnp.maximum(mx, jnp.max(s, axis=-1, keepdims=True))
        alpha = jnp.exp(mx - mx_new)
        p = jnp.exp(s - mx_new)
        m_s[g] = mx_new
        l_s[g] = alpha * l_s[g] + jnp.sum(p, axis=-1, keepdims=True)
        a_s[g] = alpha * a_s[g] + _dot(p.astype(BF16), v.astype(BF16))

    @pl.when(kk == ksel - 1)
    def _():
        o_ref[...] = a_s[...] / l_s[...]


def sel_attention_sample(q4, cache, row_blk, jsel, k_new, v_new, bias, ni, jlast):
    db, g, r, hd = q4.shape
    ksel = jsel.shape[-1]
    per_page = PAGE_SIZE // SEL_BLK

    def blk_spec(slot, gi):
        def index(bi, kk, rb, js):
            blk = rb[(bi * g + gi) * ksel + kk]
            return (blk // per_page, ni, blk % per_page, slot, 0, 0)

        return pl.BlockSpec((None, None, SEL_BLK, None, g, HEAD_DIM), index)

    def bias_spec(gi):
        return pl.BlockSpec((None, None, r, SEL_BLK), lambda bi, kk, rb, js: (gi, js[(bi * g + gi) * ksel + kk], 0, 0))

    per_seq = pl.BlockSpec((None, g, r, hd), lambda bi, kk, rb, js: (bi, 0, 0, 0))
    new_row = pl.BlockSpec((None, g, 1, hd), lambda bi, kk, rb, js: (bi, 0, 0, 0))
    return pl.pallas_call(
        functools.partial(_sel_sample_kernel, jlast=jlast, ksel=ksel),
        out_shape=jax.ShapeDtypeStruct((db, g, r, hd), F32),
        grid_spec=pltpu.PrefetchScalarGridSpec(
            num_scalar_prefetch=2,
            grid=(db, ksel),
            in_specs=[per_seq] + [blk_spec(2, gi) for gi in range(g)] + [blk_spec(3, gi) for gi in range(g)]
            + [new_row, new_row] + [bias_spec(gi) for gi in range(g)],
            out_specs=per_seq,
            scratch_shapes=[pltpu.VMEM((g, r, 1), F32), pltpu.VMEM((g, r, 1), F32), pltpu.VMEM((g, r, hd), F32)]),
        compiler_params=_cp("arbitrary", "arbitrary"),
        name="sel_attention_sample",
    )(row_blk.reshape(-1), jsel.reshape(-1), q4, *([cache] * (2 * g)), k_new, v_new, *([bias] * g))


def _win_sample_kernel(q_ref, kb_ref, vb_ref, kn_ref, vn_ref, bw_ref, b0_ref, oc_ref, os_ref, gt_ref, o_ref):
    scale = HEAD_DIM ** -0.5
    q = q_ref[...]
    wb = kb_ref.shape[0]
    s = _dot_nt(q.astype(BF16), kb_ref[...].astype(BF16)) * scale + bw_ref[...]
    dist = wb - lax.broadcasted_iota(jnp.int32, s.shape, 1)
    s = jnp.where(dist < WINDOW, s, NEG_INF)
    s_new = jnp.sum(q * kn_ref[...], axis=-1, keepdims=True) * scale + b0_ref[...][:, 0:1]
    mx = jnp.maximum(jnp.max(s, axis=-1, keepdims=True), s_new)
    e = jnp.exp(s - mx)
    e_new = jnp.exp(s_new - mx)
    l = jnp.sum(e, axis=-1, keepdims=True) + e_new
    o_win = (_dot(e.astype(BF16), vb_ref[...].astype(BF16)) + e_new * vn_ref[...]) / l
    gates = 1.0 / (1.0 + jnp.exp(-gt_ref[...]))
    o_ref[...] = gates[:, 0:1] * oc_ref[...] + gates[:, 1:2] * os_ref[...] + gates[:, 2:3] * o_win


def win_attention_sample(q4, win3, k_new, v_new, bias_w, bias0, o_cmp4, o_sel4, gate4, ni):
    db, g, r, hd = q4.shape
    wb = win3.shape[1]

    def small(shape_last):
        return pl.BlockSpec((None, None, r, shape_last), lambda bi, gi: (bi, gi, 0, 0))

    return pl.pallas_call(
        _win_sample_kernel,
        out_shape=jax.ShapeDtypeStruct((db, g, r, hd), F32),
        grid=(db, g),
        in_specs=[small(hd),
                  pl.BlockSpec((None, wb, hd), lambda bi, gi: (ni * db + bi, 0, gi)),
                  pl.BlockSpec((None, wb, hd), lambda bi, gi: (ni * db + bi, 0, g + gi)),
                  pl.BlockSpec((None, None, 1, hd), lambda bi, gi: (bi, gi, 0, 0)),
                  pl.BlockSpec((None, None, 1, hd), lambda bi, gi: (bi, gi, 0, 0)),
                  pl.BlockSpec((None, r, wb), lambda bi, gi: (gi, 0, 0)),
                  pl.BlockSpec((None, r, LANE), lambda bi, gi: (gi, 0, 0)),
                  small(hd), small(hd), small(3)],
        out_specs=small(hd),
        compiler_params=_cp("arbitrary", "arbitrary"),
        name="win_attention_sample",
    )(q4, win3, win3, k_new, v_new, bias_w, bias0, o_cmp4, o_sel4, gate4)


def nsa_layer(h_all, bt, b, t, db, ni, cache_nsa, state_nsa_win, page_table, rel_bias,
              nsa_w_in, nsa_cmp_pe, nsa_cmp_w1, nsa_cmp_b1, nsa_cmp_w2, nsa_cmp_b2):
    nq = NSA_HEADS * HEAD_DIM
    nqkv = nq + 6 * NSA_KV
    z_p, z_s = dense_prompt_and_sample(h_all, h_all, nsa_w_in, (ni,), bt, ncols=nqkv)
    w_gate = jnp.zeros((h_all.shape[1], LANE), F32).at[:, :3 * NSA_HEADS].set(nsa_w_in[ni, :, nqkv:])
    zg_p, zg_s = dense_prompt_and_sample(h_all, h_all, w_gate, (), bt)
    wz = z_p.shape[1]
    z3 = z_p.reshape(b, t, wz)
    n_layers = cache_nsa.shape[1]
    past_len = page_table.shape[1] * PAGE_SIZE
    pos_p = jnp.arange(t)

    ncb, nch, nsb, nsbp = _cmp_geometry(t)
    page_rows = (jnp.arange(b)[:, None] * (t // PAGE_SIZE) + jnp.arange(t // PAGE_SIZE)[None, :])
    rows_p = z3[:, :, nq:nq + 4 * NSA_KV].reshape(b, t, 4, NSA_KV_HEADS, HEAD_DIM)
    src_p = rows_p.reshape(bt // PAGE_SIZE, 1, PAGE_SIZE, 4, NSA_KV_HEADS, HEAD_DIM)
    a_p = compress_chunks(src_p, page_rows, 0, nsa_cmp_w1, ni, b, nch)
    kvc_p = compress_blocks(a_p, nsa_cmp_pe, nsa_cmp_w1, nsa_cmp_b1, nsa_cmp_w2, nsa_cmp_b2, ni)
    bias_p, cover_p = cmp_tables(pos_p, t, nch, nsbp, rel_bias)
    o_cmp_p, score_p = cmp_attention(z3, kvc_p, bias_p, cover_p, QBLK)
    idx_p = select_blocks(score_p, pos_p, nsb)
    sel_mask = (idx_p[:, :, None, :, :] == jnp.arange(nsbp)[None, None, :, None, None]).any(axis=-1).astype(F32)
    ii = jnp.arange(QBLK)
    tb = _bias_of_dist(jnp.arange(3)[:, None, None] * QBLK + ii[None, None, :] - ii[None, :, None], rel_bias)
    tb = tb.reshape(NSA_KV_HEADS, NSA_GROUP, 3, QBLK, QBLK).transpose((2, 0, 3, 1, 4))
    tb = tb.reshape(3, NSA_KV_HEADS, QBLK, NSA_GROUP * QBLK)
    gate_lin = zg_p[:, :3 * NSA_HEADS].reshape(b, t, NSA_KV_HEADS, 3 * NSA_GROUP).transpose((0, 2, 1, 3))
    o_p = nsa_attention_prompt(z3, sel_mask, tb, o_cmp_p, gate_lin)
    keep = min(WINDOW, t)
    win_p = z3[:, t - keep:, nq + 4 * NSA_KV:nq + 6 * NSA_KV].reshape(b, keep, 2, NSA_KV_HEADS, HEAD_DIM)

    zs = z_s[:db]
    tk = past_len + DEC_SEQ
    ncb_s, nch_s, nsb_s, nsbp_s = _cmp_geometry(tk)
    assert (ncb_s + 1) * CMP_STRIDE <= past_len
    a_s = compress_chunks(cache_nsa, page_table, ni, nsa_cmp_w1, ni, db, nch_s)
    kvc_s = compress_blocks(a_s, nsa_cmp_pe, nsa_cmp_w1, nsa_cmp_b1, nsa_cmp_w2, nsa_cmp_b2, ni)
    pos_s = jnp.full((SUBLANE,), past_len, jnp.int32)
    bias_s, cover_s = cmp_tables(pos_s, tk, nch_s, nsbp_s, rel_bias)
    zs3 = jnp.zeros((db, SUBLANE, wz), F32).at[:, 0].set(zs)
    o_cmp_s, score_s = cmp_attention(zs3, kvc_s, bias_s, cover_s, SUBLANE)
    idx_s = select_blocks(score_s[:, :, 0:1], pos_s[0:1], nsb_s)[:, :, 0]
    jlast = past_len // SEL_BLK
    per_page = PAGE_SIZE // SEL_BLK
    page_of = jnp.take_along_axis(page_table, jnp.minimum(idx_s, jlast - 1).reshape(db, -1) // per_page, axis=1)
    row_blk = (page_of.reshape(idx_s.shape) * per_page + jnp.minimum(idx_s, jlast - 1) % per_page).astype(jnp.int32)
    q4 = zs[:, :nq].reshape(db, NSA_KV_HEADS, NSA_GROUP, HEAD_DIM)
    kv_new = zs[:, nq:nq + 6 * NSA_KV].reshape(db, 6, NSA_KV_HEADS, 1, HEAD_DIM)
    kpos = jnp.arange(nsb_s * SEL_BLK).reshape(nsb_s, SEL_BLK)
    bias_sel = _bias_of_dist(past_len - kpos, rel_bias).reshape(NSA_KV_HEADS, NSA_GROUP, nsb_s, SEL_BLK).transpose((0, 2, 1, 3))
    o_sel_s = sel_attention_sample(q4, cache_nsa, row_blk, idx_s.astype(jnp.int32), kv_new[:, 2], kv_new[:, 3],
                                   bias_sel, ni, jlast)
    wb = state_nsa_win.shape[2]
    win3 = state_nsa_win.reshape(state_nsa_win.shape[0] * db, wb, 2 * NSA_KV)
    bias_w = _bias_of_dist(wb - jnp.arange(wb), rel_bias).reshape(NSA_KV_HEADS, NSA_GROUP, wb)
    bias0 = jnp.broadcast_to(_bias_of_dist(jnp.zeros((1,), jnp.int32), rel_bias).reshape(NSA_KV_HEADS, NSA_GROUP, 1),
                             (NSA_KV_HEADS, NSA_GROUP, LANE))
    gate4 = zg_s[:db, :3 * NSA_HEADS].reshape(db, NSA_KV_HEADS, NSA_GROUP, 3)
    o_cmp4 = o_cmp_s[:, 0].reshape(db, NSA_KV_HEADS, NSA_GROUP, HEAD_DIM)
    o_s = win_attention_sample(q4, win3, kv_new[:, 4], kv_new[:, 5], bias_w, bias0, o_cmp4, o_sel_s, gate4, ni)
    rows_s = zs[:, nq:nq + 4 * NSA_KV].reshape(db, DEC_SEQ, 4, NSA_KV_HEADS, HEAD_DIM)
    new_win = zs[:, nq + 4 * NSA_KV:nq + 6 * NSA_KV].reshape(db, DEC_SEQ, 2, NSA_KV_HEADS, HEAD_DIM)
    win_all = jnp.concatenate([state_nsa_win[ni], new_win], axis=1)
    keep_s = min(WINDOW, wb + DEC_SEQ)
    win_s = win_all[:, wb + DEC_SEQ - keep_s:]
    return o_p.reshape(bt, nq), o_s.reshape(db, nq), rows_p, rows_s, win_p, win_s


def kernel(x_prompt, x_sample, c_prompt, c_sample, state_ret, cache_nsa, state_nsa_win, page_table, rel_bias,
           ada_w, ada_b, ln_g, ln_b, ret_w_in, ret_gn_w, ret_w_out, nsa_w_in, nsa_cmp_pe, nsa_cmp_w1, nsa_cmp_b1,
           nsa_cmp_w2, nsa_cmp_b2, nsa_w_out, moe_w_router, moe_b_router, moe_w_up, moe_b_up, moe_w_down, moe_b_down):
    b, t, d = x_prompt.shape
    db, ds, _ = x_sample.shape
    assert ds == DEC_SEQ == 1 and db <= SUBLANE and t % ROW_TILE == 0 and t % QBLK == 0
    bt = b * t
    npad = bt + ROW_TILE
    ntok = bt + db
    tiles_per_seq = t // ROW_TILE
    past_len = page_table.shape[1] * PAGE_SIZE

    x_all = jnp.zeros((npad, d), F32).at[:bt].set(x_prompt.reshape(bt, d)).at[bt:ntok].set(x_sample.reshape(db, d))
    rc = -(-(b + db) // SUBLANE) * SUBLANE
    c_all = jnp.zeros((rc, d), F32).at[:b].set(c_prompt).at[b:b + db].set(c_sample)
    mods = ada_all(c_all, ada_w, ada_b)

    def mod_slabs(li, sub):
        m = mods[li * 2 + sub]
        return tuple(_slabs(m[:, k * d:(k + 1) * d], b, db) for k in range(3))

    def tail_rows(a_s, width, dtype):
        return jnp.zeros((ROW_TILE, width), dtype).at[:db].set(a_s.astype(dtype))

    ret_sp, ret_ss, rows_p, rows_s, win_p, win_s = [], [], [], [], [], []
    for li in range(DEPTH):
        sh, sc, gt = mod_slabs(li, 0)
        h_all = modulate(x_all, sc, sh, tiles_per_seq, b)
        if li % N_MIXERS == 0:
            ri = li // N_MIXERS
            z_p, z_s = dense_prompt_and_sample(h_all, h_all, ret_w_in, (ri,), bt)
            a_p, sp = retention(z_p.reshape(b, t, RET_IN), jnp.arange(t), None, ret_gn_w, ri, min(RET_CHUNK, t) if t % RET_CHUNK == 0 else t)
            zs3 = jnp.zeros((db, SUBLANE, RET_IN), F32).at[:, 0].set(z_s[:db])
            a_s, ss = retention(zs3, jnp.full((SUBLANE,), past_len), state_ret, ret_gn_w, ri, DEC_SEQ)
            ret_sp.append(sp)
            ret_ss.append(ss)
            nv = RET_HEADS * RET_DV
            y_all = dense_prompt_and_sample(a_p.reshape(bt, nv), tail_rows(a_s[:, 0], nv, BF16), ret_w_out, (ri,), bt,
                                            into_rows=npad)
        else:
            ni = li // N_MIXERS
            o_p, o_s, rp, rs, wp, wsb = nsa_layer(h_all, bt, b, t, db, ni, cache_nsa, state_nsa_win, page_table, rel_bias,
                                                  nsa_w_in, nsa_cmp_pe, nsa_cmp_w1, nsa_cmp_b1, nsa_cmp_w2, nsa_cmp_b2)
            rows_p.append(rp)
            rows_s.append(rs)
            win_p.append(wp)
            win_s.append(wsb)
            nq = NSA_HEADS * HEAD_DIM
            y_all = dense_prompt_and_sample(o_p, tail_rows(o_s, nq, BF16), nsa_w_out, (ni,), bt, into_rows=npad)
        x_all = resid_ln(x_all, y_all, gt, ln_g, ln_b, li, 0, tiles_per_seq, b)

        sh, sc, gt = mod_slabs(li, 1)
        h_f32, logits = modulate_router(x_all, sc, sh, moe_w_router, moe_b_router, li, tiles_per_seq, b)
        y_all = moe(h_f32, logits, ntok, li, moe_w_up, moe_b_up, moe_w_down, moe_b_down)
        x_all = resid_ln(x_all, y_all, gt, ln_g, ln_b, li, 1, tiles_per_seq, b)

    y_prompt = x_all[:bt].reshape(b, t, d)
    y_sample = x_all[bt:ntok].reshape(db, ds, d)
    return (y_prompt, y_sample, jnp.stack(ret_sp), jnp.stack(ret_ss), jnp.stack(rows_p, axis=1),
            jnp.stack(rows_s, axis=1), jnp.stack(win_p), jnp.stack(win_s))
```

```python
import functools
import math

import jax
import jax.numpy as jnp
from jax import lax
from jax.experimental import pallas as pl
from jax.experimental.pallas import tpu as pltpu

D_MODEL = 4096
BATCH = 4
SEQ = 2048
DEPTH = 2
DEC_BATCH = 8
DEC_SEQ = 1
PAST_LEN = 16384
PAGE_SIZE = 128

N_MIXERS = 2
N_RET_LAYERS = (DEPTH + 1) // 2
N_NSA_LAYERS = DEPTH // 2

DN_ALPHA = (2.0 * DEPTH) ** 0.25
LN_EPS = 1e-5
NEG_INF = -1e30
NEG_SCORE = -1e9

RET_HEADS = 16
RET_DK = D_MODEL // RET_HEADS
RET_DV = 2 * RET_DK
RET_CHUNK = 128
ROPE_BASE = 10000.0

NSA_HEADS = 32
HEAD_DIM = D_MODEL // NSA_HEADS
NSA_KV_HEADS = 4
NSA_GROUP = NSA_HEADS // NSA_KV_HEADS
NSA_KV = NSA_KV_HEADS * HEAD_DIM
CMP_LEN = 32
CMP_STRIDE = 16
CMP_HIDDEN = 2 * HEAD_DIM
SEL_BLK = 64
SEL_TOPK = 16
WINDOW = 512
QBLK = 128
FORCE_BONUS = 1e6

REL_BUCKETS = 32
REL_MAX_DIST = 128

N_EXPERTS = 32
TOP_K = 4
D_EXPERT = D_MODEL // 2
SWIGLU_ALPHA = 1.702
SWIGLU_LIMIT = 7.0

RET_IN = 2 * RET_HEADS * RET_DK + 2 * RET_HEADS * RET_DV
NSA_IN = NSA_HEADS * HEAD_DIM + 6 * NSA_KV + 3 * NSA_HEADS

F32 = jnp.float32
BF16 = jnp.bfloat16
HI = lax.Precision.HIGHEST

LANE = 128
SUBLANE = 8
VMEM_LIMIT_BYTES = 56 * 1024 * 1024

ROW_TILE = 256
MOE_TM = 256
RET_HPS = 8
MOE_UP_TN = 1024
MOE_DOWN_TN = 2048
MOE_UP_CHUNK = 512
MOE_GATHER_ROWS = 256
MOE_COMBINE_ROWS = 128
CMP_PAGES = 8


def _cp(*sem, vmem=VMEM_LIMIT_BYTES):
    return pltpu.CompilerParams(dimension_semantics=sem, vmem_limit_bytes=vmem)


def _dot(a, b):
    return jnp.dot(a, b, preferred_element_type=F32)


def _dot_nt(a, b):
    return lax.dot_general(a, b, (((1,), (1,)), ((), ())), preferred_element_type=F32)


def _dot_tn(a, b):
    return lax.dot_general(a, b, (((0,), (0,)), ((), ())), preferred_element_type=F32)


def _pick(n, *cands):
    for c in cands:
        if n % c == 0:
            return c
    return n


def _ada_kernel(c_ref, w_ref, b_ref, o_ref):
    c = c_ref[...]
    s = (c / (1.0 + jnp.exp(-c))).astype(BF16)
    o_ref[...] = _dot(s, w_ref[...].astype(BF16)) + b_ref[...]


def ada_all(c_all, ada_w, ada_b):
    rc, d = c_all.shape
    nl = ada_w.shape[0] * ada_w.shape[1]
    w = ada_w.reshape(nl, d, 3 * d)
    b = ada_b.reshape(nl, 1, 3 * d)
    tn = _pick(3 * d, 512, 256, 128)
    return pl.pallas_call(
        _ada_kernel,
        out_shape=jax.ShapeDtypeStruct((nl, rc, 3 * d), F32),
        grid=(nl, 3 * d // tn),
        in_specs=[pl.BlockSpec((rc, d), lambda l, j: (0, 0)),
                  pl.BlockSpec((None, d, tn), lambda l, j: (l, 0, j)),
                  pl.BlockSpec((None, 1, tn), lambda l, j: (l, 0, j))],
        out_specs=pl.BlockSpec((None, rc, tn), lambda l, j: (l, 0, j)),
        compiler_params=_cp("arbitrary", "arbitrary"),
        name="ada_mod",
    )(c_all, w, b)


def _slabs(mod, nb, ndb):
    d = mod.shape[-1]
    p = jnp.broadcast_to(mod[:nb, None, :], (nb, SUBLANE, d))
    s = jnp.zeros((1, SUBLANE, d), F32).at[0, :ndb].set(mod[nb:nb + ndb])
    return jnp.concatenate([p, s], axis=0)


def _mod_kernel(x_ref, sc_ref, sh_ref, o_ref):
    tm, d = x_ref.shape
    x = x_ref[...].reshape(tm // SUBLANE, SUBLANE, d)
    h = x * (1.0 + sc_ref[...][None]) + sh_ref[...][None]
    o_ref[...] = h.reshape(tm, d).astype(o_ref.dtype)


def _mod_router_kernel(x_ref, sc_ref, sh_ref, wr_ref, br_ref, o_ref, lg_ref):
    tm, d = x_ref.shape
    x = x_ref[...].reshape(tm // SUBLANE, SUBLANE, d)
    h = (x * (1.0 + sc_ref[...][None]) + sh_ref[...][None]).reshape(tm, d)
    o_ref[...] = h
    lg_ref[...] = jnp.dot(h, wr_ref[...], precision=HI, preferred_element_type=F32) + br_ref[...]


def _slab_spec(d, tiles_per_seq, nb):
    return pl.BlockSpec((None, SUBLANE, d), lambda i: (jnp.minimum(i // tiles_per_seq, nb), 0, 0))


def modulate(x_all, sc, sh, tiles_per_seq, nb):
    n, d = x_all.shape
    return pl.pallas_call(
        _mod_kernel,
        out_shape=jax.ShapeDtypeStruct((n, d), BF16),
        grid=(n // ROW_TILE,),
        in_specs=[pl.BlockSpec((ROW_TILE, d), lambda i: (i, 0)),
                  _slab_spec(d, tiles_per_seq, nb), _slab_spec(d, tiles_per_seq, nb)],
        out_specs=pl.BlockSpec((ROW_TILE, d), lambda i: (i, 0)),
        compiler_params=_cp("arbitrary"),
        name="modulate",
    )(x_all, sc, sh)


def modulate_router(x_all, sc, sh, w_r, b_r, li, tiles_per_seq, nb):
    n, d = x_all.shape
    ne = w_r.shape[-1]
    b_r3 = b_r.reshape(b_r.shape[0], 1, ne)
    return pl.pallas_call(
        _mod_router_kernel,
        out_shape=(jax.ShapeDtypeStruct((n, d), F32), jax.ShapeDtypeStruct((n, ne), F32)),
        grid=(n // ROW_TILE,),
        in_specs=[pl.BlockSpec((ROW_TILE, d), lambda i: (i, 0)),
                  _slab_spec(d, tiles_per_seq, nb), _slab_spec(d, tiles_per_seq, nb),
                  pl.BlockSpec((None, d, ne), lambda i: (li, 0, 0)),
                  pl.BlockSpec((None, 1, ne), lambda i: (li, 0, 0))],
        out_specs=(pl.BlockSpec((ROW_TILE, d), lambda i: (i, 0)),
                   pl.BlockSpec((ROW_TILE, ne), lambda i: (i, 0))),
        compiler_params=_cp("arbitrary"),
        name="modulate_router",
    )(x_all, sc, sh, w_r, b_r3)


def _resid_ln_kernel(x_ref, y_ref, gt_ref, g_ref, b_ref, o_ref):
    tm, d = x_ref.shape
    x = x_ref[...].reshape(tm // SUBLANE, SUBLANE, d)
    y = y_ref[...].reshape(tm // SUBLANE, SUBLANE, d)
    v = (DN_ALPHA * x + gt_ref[...][None] * y).reshape(tm, d)
    mu = jnp.mean(v, axis=-1, keepdims=True)
    c = v - mu
    var = jnp.mean(c * c, axis=-1, keepdims=True)
    o_ref[...] = c * lax.rsqrt(var + LN_EPS) * g_ref[...] + b_ref[...]


def resid_ln(x_all, y_all, gt, ln_g, ln_b, li, sub, tiles_per_seq, nb):
    n, d = x_all.shape
    g3 = ln_g.reshape(ln_g.shape[0] * ln_g.shape[1], 1, d)
    b3 = ln_b.reshape(ln_b.shape[0] * ln_b.shape[1], 1, d)
    idx = li * 2 + sub
    return pl.pallas_call(
        _resid_ln_kernel,
        out_shape=jax.ShapeDtypeStruct((n, d), F32),
        grid=(n // ROW_TILE,),
        in_specs=[pl.BlockSpec((ROW_TILE, d), lambda i: (i, 0)),
                  pl.BlockSpec((ROW_TILE, d), lambda i: (i, 0)),
                  _slab_spec(d, tiles_per_seq, nb),
                  pl.BlockSpec((None, 1, d), lambda i: (idx, 0, 0)),
                  pl.BlockSpec((None, 1, d), lambda i: (idx, 0, 0))],
        out_specs=pl.BlockSpec((ROW_TILE, d), lambda i: (i, 0)),
        compiler_params=_cp("arbitrary"),
        name="resid_ln",
    )(x_all, y_all, gt, g3, b3)


def _mm_kernel(x_ref, w_ref, o_ref):
    @pl.when(pl.program_id(2) == 0)
    def _():
        o_ref[...] = jnp.zeros_like(o_ref)

    o_ref[...] += _dot(x_ref[...], w_ref[...].astype(BF16))


def _mm_into_kernel(prev_ref, x_ref, w_ref, o_ref):
    del prev_ref
    _mm_kernel(x_ref, w_ref, o_ref)


def matmul(x, w, wsel, *, row0, rows, tm, tn, tk, ncols=None, out_rows=None, out_row0=0, into=None):
    kdim = x.shape[1]
    n = w.shape[-1] if ncols is None else ncols
    assert n % tn == 0
    nj = n // tn
    assert rows % tm == 0 and row0 % tm == 0 and out_row0 % tm == 0 and kdim % tk == 0
    rb0, ob0 = row0 // tm, out_row0 // tm
    out_rows = rows if out_rows is None else out_rows
    nlead = len(wsel)
    in_specs = [pl.BlockSpec((tm, tk), lambda i, j, k: (rb0 + i, k)),
                pl.BlockSpec((None,) * nlead + (tk, tn), lambda i, j, k: tuple(wsel) + (k, j))]
    args = [x, w]
    kern, aliases = _mm_kernel, {}
    if into is not None:
        assert into.shape == (out_rows, nj * tn)
        in_specs = [pl.BlockSpec(memory_space=pl.ANY)] + in_specs
        args = [into] + args
        kern, aliases = _mm_into_kernel, {0: 0}
    return pl.pallas_call(
        kern,
        out_shape=jax.ShapeDtypeStruct((out_rows, nj * tn), F32),
        grid=(rows // tm, nj, kdim // tk),
        in_specs=in_specs,
        out_specs=pl.BlockSpec((tm, tn), lambda i, j, k: (ob0 + i, j)),
        input_output_aliases=aliases,
        compiler_params=_cp("arbitrary", "arbitrary", "arbitrary"),
        name="matmul",
    )(*args)


def dense_prompt_and_sample(x_p, x_s, w, wsel, n_prompt, *, ncols=None, into_rows=None):
    kdim = x_p.shape[1]
    n = w.shape[-1] if ncols is None else ncols
    tm = _pick(n_prompt, 2048, 1024, 512, 256)
    tn = _pick(n, 1024, 512, 256, 128)
    tk = _pick(kdim, 1024, 512, 256)
    s_row0 = x_s.shape[0] - ROW_TILE
    if into_rows is None:
        z_p = matmul(x_p, w, wsel, row0=0, rows=n_prompt, tm=tm, tn=tn, tk=tk, ncols=ncols)
        z_s = matmul(x_s, w, wsel, row0=s_row0, rows=ROW_TILE, tm=ROW_TILE, tn=tn, tk=tk, ncols=ncols)
        return z_p, z_s
    y = jnp.zeros((into_rows, n), F32)
    y = matmul(x_p, w, wsel, row0=0, rows=n_prompt, tm=tm, tn=tn, tk=tk, ncols=ncols, out_rows=into_rows, into=y)
    return matmul(x_s, w, wsel, row0=s_row0, rows=ROW_TILE, tm=ROW_TILE, tn=tn, tk=tk, ncols=ncols,
                  out_rows=into_rows, out_row0=n_prompt, into=y)


def _ret_kernel(*refs, has_s0, nchunks):
    (q_ref, k_ref, v_ref, g_ref, cos_ref, sin_ref, dec_ref, qd_ref, kd_ref, cd_ref, gn_ref), rest = refs[:11], refs[11:]
    if has_s0:
        s0_ref, a_ref, so_ref, s_scr = rest
    else:
        a_ref, so_ref, s_scr = rest
    c = pl.program_id(2)

    @pl.when(c == 0)
    def _():
        s_scr[...] = s0_ref[...] if has_s0 else jnp.zeros_like(s_scr)

    half = RET_DK // 2
    cos = cos_ref[...]
    sin = sin_ref[...]

    def rot(x):
        x1, x2 = x[:, :half], x[:, half:]
        return jnp.concatenate([x1 * cos - x2 * sin, x1 * sin + x2 * cos], axis=-1)

    for hp in range(s_scr.shape[0]):
        ks = slice(hp * RET_DK, (hp + 1) * RET_DK)
        vs = slice(hp * RET_DV, (hp + 1) * RET_DV)
        q = rot(q_ref[:, ks])
        k = rot(k_ref[:, ks]) * (RET_DK ** -0.5)
        qb = q.astype(BF16)
        vb = v_ref[:, vs].astype(BF16)
        s = s_scr[hp]
        att = _dot_nt(qb, k.astype(BF16)) * dec_ref[hp]
        o = _dot(att.astype(BF16), vb) + _dot(qb, s.astype(BF16)) * qd_ref[hp]
        s_new = s * cd_ref[hp] + _dot_tn((k * kd_ref[hp]).astype(BF16), vb)
        s_scr[hp] = s_new
        mu = jnp.mean(o, axis=-1, keepdims=True)
        oc = o - mu
        var = jnp.mean(oc * oc, axis=-1, keepdims=True)
        on = oc * lax.rsqrt(var + LN_EPS) * gn_ref[:, vs]
        g = g_ref[:, vs]
        a_ref[:, vs] = ((g / (1.0 + jnp.exp(-g))) * on).astype(BF16)

    @pl.when(c == nchunks - 1)
    def _():
        so_ref[...] = s_scr[...]


def retention(z3, pos, s0, gn_w, ri, true_chunk):
    b, tpad, _ = z3.shape
    cpad = min(RET_CHUNK, tpad)
    nchunks = tpad // cpad
    nk, nv = RET_HEADS * RET_DK, RET_HEADS * RET_DV
    half = RET_DK // 2
    inv = 1.0 / (ROPE_BASE ** (jnp.arange(half, dtype=F32) / half))
    ang = pos.astype(F32)[:, None] * inv[None, :]
    cos, sin = jnp.cos(ang), jnp.sin(ang)
    lg = jnp.log1p(-jnp.exp2(-5.0 - jnp.arange(RET_HEADS, dtype=F32)))
    i = jnp.arange(cpad, dtype=F32)
    diff = i[:, None] - i[None, :]
    decay = jnp.where(diff >= 0, jnp.exp(jnp.maximum(diff, 0.0)[None] * lg[:, None, None]), 0.0)
    q_dec = jnp.exp((i + 1.0)[None, :] * lg[:, None])[:, :, None]
    k_dec = jnp.exp((true_chunk - 1.0 - i)[None, :] * lg[:, None])[:, :, None]
    c_dec = jnp.exp(true_chunk * lg)[:, None, None]
    hps = math.gcd(RET_HPS, RET_HEADS)
    kq, kv = hps * RET_DK, hps * RET_DV
    in_specs = [
        pl.BlockSpec((None, cpad, kq), lambda bi, h, c: (bi, c, h)),
        pl.BlockSpec((None, cpad, kq), lambda bi, h, c: (bi, c, nk // kq + h)),
        pl.BlockSpec((None, cpad, kv), lambda bi, h, c: (bi, c, 2 * nk // kv + h)),
        pl.BlockSpec((None, cpad, kv), lambda bi, h, c: (bi, c, (2 * nk + nv) // kv + h)),
        pl.BlockSpec((cpad, half), lambda bi, h, c: (c, 0)),
        pl.BlockSpec((cpad, half), lambda bi, h, c: (c, 0)),
        pl.BlockSpec((hps, cpad, cpad), lambda bi, h, c: (h, 0, 0)),
        pl.BlockSpec((hps, cpad, 1), lambda bi, h, c: (h, 0, 0)),
        pl.BlockSpec((hps, cpad, 1), lambda bi, h, c: (h, 0, 0)),
        pl.BlockSpec((hps, 1, 1), lambda bi, h, c: (h, 0, 0)),
        pl.BlockSpec((None, 1, kv), lambda bi, h, c: (ri, 0, h)),
    ]
    args = [z3, z3, z3, z3, cos, sin, decay, q_dec, k_dec, c_dec, gn_w.reshape(gn_w.shape[0], 1, nv)]
    if s0 is not None:
        in_specs.append(pl.BlockSpec((None, None, hps, RET_DK, RET_DV), lambda bi, h, c: (ri, bi, h, 0, 0)))
        args.append(s0)
    return pl.pallas_call(
        functools.partial(_ret_kernel, has_s0=s0 is not None, nchunks=nchunks),
        out_shape=(jax.ShapeDtypeStruct((b, tpad, nv), BF16),
                   jax.ShapeDtypeStruct((b, RET_HEADS, RET_DK, RET_DV), F32)),
        grid=(b, RET_HEADS // hps, nchunks),
        in_specs=in_specs,
        out_specs=(pl.BlockSpec((None, cpad, kv), lambda bi, h, c: (bi, c, h)),
                   pl.BlockSpec((None, hps, RET_DK, RET_DV), lambda bi, h, c: (bi, h, 0, 0))),
        scratch_shapes=[pltpu.VMEM((hps, RET_DK, RET_DV), F32)],
        compiler_params=_cp("arbitrary", "arbitrary", "arbitrary"),
        name="retention",
    )(*args)


DMA_UNROLL = 8


def _moe_gather_kernel(tok_ref, nu_ref, h_hbm, o_ref, buf, sem, *, rb):
    i = pl.program_id(0)
    n_steps = nu_ref[0] * MOE_TM // rb

    def row_copy(step, r):
        slot = step % 2
        return pltpu.make_async_copy(h_hbm.at[pl.ds(tok_ref[step * rb + r], 1)], buf.at[slot, pl.ds(r, 1)], sem.at[slot])

    def start_block(step):
        def body(r8, carry):
            for u in range(DMA_UNROLL):
                row_copy(step, r8 * DMA_UNROLL + u).start(priority=u % 2)
            return carry

        lax.fori_loop(0, rb // DMA_UNROLL, body, 0)

    def wait_block(step):
        def body(r8, carry):
            for u in range(DMA_UNROLL):
                row_copy(step, r8 * DMA_UNROLL + u).wait()
            return carry

        lax.fori_loop(0, rb // DMA_UNROLL, body, 0)

    @pl.when(jnp.logical_and(i == 0, n_steps > 0))
    def _():
        start_block(0)

    @pl.when(i + 1 < n_steps)
    def _():
        start_block(i + 1)

    @pl.when(i < n_steps)
    def _():
        wait_block(i)
        o_ref[...] = buf[i % 2].astype(BF16)

    @pl.when(i >= n_steps)
    def _():
        o_ref[...] = jnp.zeros_like(o_ref)


def moe_gather(h_all, row_tok, n_used):
    rows = row_tok.shape[0]
    d = h_all.shape[1]
    rb = MOE_GATHER_ROWS
    return pl.pallas_call(
        functools.partial(_moe_gather_kernel, rb=rb),
        out_shape=jax.ShapeDtypeStruct((rows, d), BF16),
        grid_spec=pltpu.PrefetchScalarGridSpec(
            num_scalar_prefetch=2,
            grid=(rows // rb,),
            in_specs=[pl.BlockSpec(memory_space=pl.ANY)],
            out_specs=pl.BlockSpec((rb, d), lambda i, tok, nu: (i, 0)),
            scratch_shapes=[pltpu.VMEM((2, rb, d), F32), pltpu.SemaphoreType.DMA((2,))]),
        compiler_params=_cp("arbitrary"),
        name="moe_gather",
    )(row_tok, n_used, h_all)


def _moe_block_paths(meta, li, tn, w_hbm, o_ref, wland, wbf, sem, compute_rows):
    be_ref, first_ref, half_ref, nxe_ref, last_ref, nu_ref = meta
    j, blk = pl.program_id(0), pl.program_id(1)
    used = blk < nu_ref[0]

    def tile_copy(e, jj):
        return pltpu.make_async_copy(w_hbm.at[li, e, :, pl.ds(pl.multiple_of(jj * tn, tn), tn)], wland, sem)

    @pl.when(jnp.logical_and(j == 0, blk == 0))
    def _():
        tile_copy(be_ref[0], j).start()

    @pl.when(jnp.logical_and(used, first_ref[blk] == 1))
    def _():
        tile_copy(be_ref[blk], j).wait()
        wbf[...] = wland[...].astype(BF16)
        j_next = j + last_ref[blk]

        @pl.when(j_next < pl.num_programs(0))
        def _():
            tile_copy(nxe_ref[blk], j_next).start()

    @pl.when(jnp.logical_and(used, half_ref[blk] == 0))
    def _():
        compute_rows(MOE_TM)

    @pl.when(jnp.logical_and(used, half_ref[blk] == 1))
    def _():
        compute_rows(MOE_TM // 2)
        o_ref[MOE_TM // 2:, :] = jnp.zeros((MOE_TM // 2, o_ref.shape[1]), o_ref.dtype)

    @pl.when(jnp.logical_not(used))
    def _():
        o_ref[...] = jnp.zeros_like(o_ref)


def _moe_up_kernel(*refs, tn, li):
    meta, (x_ref, w_hbm, b_ref, o_ref, wland, wbf, sem) = refs[:6], refs[6:]
    cw = min(MOE_UP_CHUNK, tn)
    nchunk = tn // cw

    def compute_rows(nr):
        x = x_ref[:nr, :]
        pick_even = (lax.broadcasted_iota(jnp.int32, (2 * LANE, LANE), 0)
                     == 2 * lax.broadcasted_iota(jnp.int32, (2 * LANE, LANE), 1)).astype(BF16)

        def up(c):
            cols = slice(c * cw, (c + 1) * cw)
            return _dot(x, wbf[:, cols]) + b_ref[:, cols]

        def activate(hu):
            lin = pltpu.roll(hu, cw - 1, 1)
            glu = jnp.minimum(hu, SWIGLU_LIMIT)
            linc = jnp.clip(lin, -SWIGLU_LIMIT, SWIGLU_LIMIT)
            return ((glu / (1.0 + jnp.exp(-SWIGLU_ALPHA * glu))) * (linc + 1.0)).astype(BF16)

        hu = up(0)
        for c in range(nchunk):
            hu_next = up(c + 1) if c + 1 < nchunk else None
            act = activate(hu)
            for s in range(cw // (2 * LANE)):
                o0 = (c * cw // (2 * LANE) + s) * LANE
                o_ref[:nr, o0:o0 + LANE] = _dot(act[:, s * 2 * LANE:(s + 1) * 2 * LANE], pick_even).astype(BF16)
            hu = hu_next

    _moe_block_paths(meta, li, tn, w_hbm, o_ref, wland, wbf, sem, compute_rows)


def _moe_down_kernel(*refs, tn, li):
    meta, (x_ref, w_hbm, b_ref, o_ref, wland, wbf, sem) = refs[:6], refs[6:]

    def compute_rows(nr):
        o_ref[:nr, :] = _dot(x_ref[:nr, :], wbf[...]) + b_ref[...]

    _moe_block_paths(meta, li, tn, w_hbm, o_ref, wland, wbf, sem, compute_rows)


def _moe_grouped(kern, x, w, bias, li, meta, tn, out_cols, out_tn, out_dtype, name):
    rows, kdim = x.shape
    n = w.shape[-1]
    n_blk = rows // MOE_TM
    b4 = bias.reshape(bias.shape[0], bias.shape[1], 1, n)

    def bc(blk, nu):
        return jnp.minimum(blk, nu[0] - 1)

    return pl.pallas_call(
        functools.partial(kern, tn=tn, li=li),
        out_shape=jax.ShapeDtypeStruct((rows, out_cols), out_dtype),
        grid_spec=pltpu.PrefetchScalarGridSpec(
            num_scalar_prefetch=6,
            grid=(n // tn, n_blk),
            in_specs=[pl.BlockSpec((MOE_TM, kdim), lambda j, blk, be, fi, ha, nx, la, nu: (bc(blk, nu), 0)),
                      pl.BlockSpec(memory_space=pl.ANY),
                      pl.BlockSpec((None, None, 1, tn), lambda j, blk, be, fi, ha, nx, la, nu: (li, be[bc(blk, nu)], 0, j))],
            out_specs=pl.BlockSpec((MOE_TM, out_tn), lambda j, blk, be, fi, ha, nx, la, nu: (blk, j)),
            scratch_shapes=[pltpu.VMEM((kdim, tn), F32), pltpu.VMEM((kdim, tn), BF16), pltpu.SemaphoreType.DMA(())]),
        compiler_params=_cp("arbitrary", "arbitrary"),
        name=name,
    )(*meta, x, w, b4)


def _moe_combine_kernel(pos_ref, tv_ref, yb_hbm, o_ref, buf, sem, *, tt):
    i = pl.program_id(0)
    n_steps = pl.num_programs(0)
    rows_per_iter = DMA_UNROLL // TOP_K if DMA_UNROLL >= TOP_K else 1

    def row_copy(step, r, k):
        slot = step % 2
        return pltpu.make_async_copy(yb_hbm.at[pl.ds(pos_ref[(step * tt + r) * TOP_K + k], 1)],
                                     buf.at[slot, k, pl.ds(r, 1)], sem.at[slot])

    def start_tile(step):
        def body(rr, carry):
            for u in range(rows_per_iter):
                for k in range(TOP_K):
                    row_copy(step, rr * rows_per_iter + u, k).start(priority=k % 2)
            return carry

        lax.fori_loop(0, tt // rows_per_iter, body, 0)

    def wait_tile(step):
        def body(rr, carry):
            for u in range(rows_per_iter):
                for k in range(TOP_K):
                    row_copy(step, rr * rows_per_iter + u, k).wait()
            return carry

        lax.fori_loop(0, tt // rows_per_iter, body, 0)

    @pl.when(i == 0)
    def _():
        start_tile(0)

    @pl.when(i + 1 < n_steps)
    def _():
        start_tile(i + 1)

    wait_tile(i)
    slot = i % 2
    tv = tv_ref[...]
    e = jnp.exp(tv - jnp.max(tv, axis=-1, keepdims=True))
    gate = e / jnp.sum(e, axis=-1, keepdims=True)
    acc = gate[:, 0:1] * buf[slot, 0]
    for k in range(1, TOP_K):
        acc = acc + gate[:, k:k + 1] * buf[slot, k]
    o_ref[...] = acc


def moe_combine(yb, pos_flat, top_v):
    n = top_v.shape[0]
    d = yb.shape[1]
    tt = MOE_COMBINE_ROWS
    return pl.pallas_call(
        functools.partial(_moe_combine_kernel, tt=tt),
        out_shape=jax.ShapeDtypeStruct((n, d), F32),
        grid_spec=pltpu.PrefetchScalarGridSpec(
            num_scalar_prefetch=1,
            grid=(n // tt,),
            in_specs=[pl.BlockSpec((tt, TOP_K), lambda i, pos: (i, 0)),
                      pl.BlockSpec(memory_space=pl.ANY)],
            out_specs=pl.BlockSpec((tt, d), lambda i, pos: (i, 0)),
            scratch_shapes=[pltpu.VMEM((2, TOP_K, tt, d), F32), pltpu.SemaphoreType.DMA((2,))]),
        compiler_params=_cp("arbitrary"),
        name="moe_combine",
    )(pos_flat, top_v, yb)


def moe(h_all, logits, ntok, li, w_up, b_up, w_down, b_down):
    npad, d = h_all.shape
    tm = MOE_TM
    top_v, top_e = lax.top_k(logits[:ntok], TOP_K)
    n_asg = ntok * TOP_K
    n_blk = (n_asg + N_EXPERTS * (tm - 1) + tm - 1) // tm
    rows = n_blk * tm
    flat_e = top_e.reshape(n_asg)
    onehot = (jnp.arange(N_EXPERTS, dtype=jnp.int32)[:, None] == flat_e[None, :]).astype(jnp.int32)
    prefix = jnp.cumsum(onehot, axis=1)
    counts = prefix[:, -1]
    pcounts = (counts + tm - 1) // tm * tm
    pend = jnp.cumsum(pcounts)
    pstart = pend - pcounts
    dest = jnp.sum(onehot * (prefix - 1 + pstart[:, None]), axis=0).astype(jnp.int32)
    row_tok = jnp.zeros((rows,), jnp.int32).at[dest].set(jnp.arange(n_asg, dtype=jnp.int32) // TOP_K)
    pos = jnp.zeros((npad * TOP_K,), jnp.int32).at[:n_asg].set(dest)
    blk_e = jnp.minimum(jnp.sum((pend[None, :] <= (jnp.arange(n_blk) * tm)[:, None]).astype(jnp.int32), axis=1),
                        N_EXPERTS - 1).astype(jnp.int32)
    first = jnp.concatenate([jnp.ones((1,), jnp.int32), (blk_e[1:] != blk_e[:-1]).astype(jnp.int32)])
    n_used = (pend[-1] // tm).astype(jnp.int32).reshape(1)
    row_end = jnp.sum((jnp.arange(N_EXPERTS)[None, :] == blk_e[:, None]) * (pstart + counts)[None, :], axis=1)
    half = (row_end - jnp.arange(n_blk) * tm <= tm // 2).astype(jnp.int32)
    tv_pad = jnp.zeros((npad, TOP_K), F32).at[:ntok].set(top_v)
    bidx = jnp.arange(n_blk)
    run_start = jnp.where((first == 1) & (bidx < n_used[0]), bidx, n_blk)
    next_start = jnp.concatenate([lax.cummin(run_start[::-1])[::-1][1:], jnp.full((1,), n_blk, run_start.dtype)])
    is_last = (next_start >= n_blk).astype(jnp.int32)
    onehot_next = (bidx[None, :] == jnp.where(is_last == 1, 0, next_start)[:, None]).astype(jnp.int32)
    next_e = jnp.sum(onehot_next * blk_e[None, :], axis=1).astype(jnp.int32)
    meta = (blk_e, first, half, next_e, is_last, n_used)

    xs = moe_gather(h_all, row_tok, n_used)
    tn_up = _pick(2 * D_EXPERT, MOE_UP_TN, 512, 256)
    act = _moe_grouped(_moe_up_kernel, xs, w_up, b_up, li, meta, tn_up, D_EXPERT, tn_up // 2, BF16, "moe_up")
    tn_dn = _pick(d, MOE_DOWN_TN, 1024, 512, 256)
    yb = _moe_grouped(_moe_down_kernel, act, w_down, b_down, li, meta, tn_dn, d, tn_dn, F32, "moe_down")
    return moe_combine(yb, pos, tv_pad)


def _t5_bucket(dist):
    n = jnp.maximum(dist, 0)
    exact = REL_BUCKETS // 2
    nf = jnp.maximum(n, 1).astype(F32)
    large = exact + (jnp.log(nf / exact) / math.log(REL_MAX_DIST / exact) * (REL_BUCKETS - exact)).astype(jnp.int32)
    large = jnp.minimum(large, REL_BUCKETS - 1)
    return jnp.where(n < exact, n, large)


def _bias_of_dist(dist, table):
    onehot = (_t5_bucket(dist)[None] == jnp.arange(REL_BUCKETS).reshape((REL_BUCKETS,) + (1,) * dist.ndim)).astype(F32)
    return jnp.tensordot(table.astype(F32).T, onehot, axes=((1,), (0,)), precision=HI)


def _cmp_geometry(tk):
    ncb = (tk - CMP_LEN) // CMP_STRIDE + 1
    nch = -(-(ncb + CMP_LEN // CMP_STRIDE - 1) // (CMP_PAGES * SUBLANE)) * (CMP_PAGES * SUBLANE)
    nsb = -(-tk // SEL_BLK)
    nsbp = -(-nsb // LANE) * LANE
    return ncb, nch, nsb, nsbp


def _cmpa_kernel(*refs, npage):
    page_refs, w_ref, o_ref, pg = refs[1:npage + 1], refs[npage + 1], refs[npage + 2], refs[npage + 3]
    cpp = PAGE_SIZE // CMP_STRIDE
    m_rows = NSA_KV_HEADS * npage * cpp
    blk = 2 * HEAD_DIM
    per = CMP_LEN // CMP_STRIDE
    for p in range(npage):
        for slot in range(2):
            for g in range(NSA_KV_HEADS):
                pg[p, slot * NSA_KV_HEADS + g] = page_refs[p][:, slot, g, :]
    for slot in range(2):
        accs = [jnp.zeros((m_rows, CMP_HIDDEN), F32) for _ in range(per)]
        for s2 in range(CMP_STRIDE // 2):
            parts = []
            for g in range(NSA_KV_HEADS):
                for p in range(npage):
                    c = slot * NSA_KV_HEADS + g
                    x0 = pg[p, c, pl.ds(2 * s2, cpp, stride=CMP_STRIDE), :]
                    x1 = pg[p, c, pl.ds(2 * s2 + 1, cpp, stride=CMP_STRIDE), :]
                    parts.append(jnp.concatenate([x0, x1], axis=1))
            xm = jnp.concatenate(parts, axis=0).astype(BF16)
            for m in range(per):
                w = w_ref[slot, pl.ds(m * CMP_STRIDE * HEAD_DIM + s2 * blk, blk), :].astype(BF16)
                accs[m] = accs[m] + _dot(xm, w)
        o_ref[slot] = jnp.concatenate(accs, axis=1).reshape(NSA_KV_HEADS, npage * cpp, per * CMP_HIDDEN)


def compress_chunks(src6, page_rows, li, w1, ni, nb, nch):
    npg = page_rows.shape[1]
    cpp = PAGE_SIZE // CMP_STRIDE
    steps = nch // (CMP_PAGES * cpp)
    per = CMP_LEN // CMP_STRIDE
    idx = jnp.minimum(jnp.arange(steps * CMP_PAGES), npg - 1)
    pr = page_rows[:, idx].reshape(-1).astype(jnp.int32)

    def page_spec(p):
        return pl.BlockSpec((None, None, PAGE_SIZE, 2, NSA_KV_HEADS, HEAD_DIM),
                            lambda b, s, pr_ref: (pr_ref[(b * steps + s) * CMP_PAGES + p], li, 0, 0, 0, 0))

    return pl.pallas_call(
        functools.partial(_cmpa_kernel, npage=CMP_PAGES),
        out_shape=jax.ShapeDtypeStruct((2, nb, NSA_KV_HEADS, nch, per * CMP_HIDDEN), F32),
        grid_spec=pltpu.PrefetchScalarGridSpec(
            num_scalar_prefetch=1,
            grid=(nb, steps),
            in_specs=[page_spec(p) for p in range(CMP_PAGES)]
            + [pl.BlockSpec((None, 2, CMP_LEN * HEAD_DIM, CMP_HIDDEN), lambda b, s, pr_ref: (ni, 0, 0, 0))],
            out_specs=pl.BlockSpec((2, None, NSA_KV_HEADS, CMP_PAGES * cpp, per * CMP_HIDDEN),
                                   lambda b, s, pr_ref: (0, b, 0, s, 0)),
            scratch_shapes=[pltpu.VMEM((CMP_PAGES, 2 * NSA_KV_HEADS, PAGE_SIZE, HEAD_DIM), F32)]),
        compiler_params=_cp("arbitrary", "arbitrary"),
        name="compress_chunks",
    )(pr, *([src6] * CMP_PAGES), w1)


def _cmpb_kernel(a_ref, pe_ref, w1_ref, b1_ref, w2_ref, b2_ref, o_ref):
    nch = a_ref.shape[0]
    pew = _dot(pe_ref[...].astype(BF16), w1_ref[...].astype(BF16))[0:1]
    a = a_ref[...]
    hid = b1_ref[...] + pew
    hid = hid + a[:, :CMP_HIDDEN]
    hid = hid + pltpu.roll(a[:, CMP_HIDDEN:], nch - 1, 0)
    act = jax.nn.gelu(hid, approximate=True)
    o_ref[...] = _dot(act.astype(BF16), w2_ref[...].astype(BF16)) + b2_ref[...]


def compress_blocks(a, pe, w1, b1, w2, b2, ni):
    _, nb, g, nch, _ = a.shape
    assert CMP_LEN // CMP_STRIDE == 2
    pe8 = jnp.broadcast_to(pe.reshape(pe.shape[0], 2, 1, CMP_LEN * HEAD_DIM), (pe.shape[0], 2, SUBLANE, CMP_LEN * HEAD_DIM))
    return pl.pallas_call(
        _cmpb_kernel,
        out_shape=jax.ShapeDtypeStruct((2, nb, nch, g * HEAD_DIM), F32),
        grid=(2, nb, g),
        in_specs=[pl.BlockSpec((None, None, None, nch, 2 * CMP_HIDDEN), lambda s, b, gi: (s, b, gi, 0, 0)),
                  pl.BlockSpec((None, None, SUBLANE, CMP_LEN * HEAD_DIM), lambda s, b, gi: (ni, s, 0, 0)),
                  pl.BlockSpec((None, None, CMP_LEN * HEAD_DIM, CMP_HIDDEN), lambda s, b, gi: (ni, s, 0, 0)),
                  pl.BlockSpec((None, None, 1, CMP_HIDDEN), lambda s, b, gi: (ni, s, 0, 0)),
                  pl.BlockSpec((None, None, CMP_HIDDEN, HEAD_DIM), lambda s, b, gi: (ni, s, 0, 0)),
                  pl.BlockSpec((None, None, 1, HEAD_DIM), lambda s, b, gi: (ni, s, 0, 0))],
        out_specs=pl.BlockSpec((None, None, nch, HEAD_DIM), lambda s, b, gi: (s, b, 0, gi)),
        compiler_params=_cp("arbitrary", "arbitrary", "arbitrary"),
        name="compress_blocks",
    )(a, pe8, w1, b1.reshape(b1.shape[0], 2, 1, CMP_HIDDEN), w2, b2.reshape(b2.shape[0], 2, 1, HEAD_DIM))


def _cmp_attn_kernel(q_ref, kc_ref, vc_ref, bias_ref, cov_ref, o_ref, sc_ref):
    kc = kc_ref[...].astype(BF16)
    vc = vc_ref[...].astype(BF16)
    scale = HEAD_DIM ** -0.5
    tq = q_ref.shape[0]
    nch = kc.shape[0]
    qall = jnp.concatenate([q_ref[:, r * HEAD_DIM:(r + 1) * HEAD_DIM] for r in range(NSA_GROUP)], axis=0).astype(BF16)
    lg = _dot_nt(qall, kc) * scale + bias_ref[...].reshape(NSA_GROUP * tq, nch)
    mx = jnp.max(lg, axis=-1, keepdims=True)
    e = jnp.exp(lg - mx)
    p = e / jnp.sum(e, axis=-1, keepdims=True) * (mx > 0.1 * NEG_INF).astype(F32)
    o = _dot(p.astype(BF16), vc)
    for r in range(NSA_GROUP):
        o_ref[:, r * HEAD_DIM:(r + 1) * HEAD_DIM] = o[r * tq:(r + 1) * tq]
    p_heads = jnp.sum(p.reshape(NSA_GROUP, tq, nch), axis=0)
    sc_ref[...] = jnp.dot(p_heads, cov_ref[...], precision=HI, preferred_element_type=F32)


def cmp_attention(z3, kvc, bias, cover, tq):
    b, t, _ = z3.shape
    nch = kvc.shape[2]
    nsbp = cover.shape[1]
    gw = NSA_GROUP * HEAD_DIM
    return pl.pallas_call(
        _cmp_attn_kernel,
        out_shape=(jax.ShapeDtypeStruct((b, t, NSA_HEADS * HEAD_DIM), F32),
                   jax.ShapeDtypeStruct((b, NSA_KV_HEADS, t, nsbp), F32)),
        grid=(b, NSA_KV_HEADS, t // tq),
        in_specs=[pl.BlockSpec((None, tq, gw), lambda bi, g, i: (bi, i, g)),
                  pl.BlockSpec((None, None, nch, HEAD_DIM), lambda bi, g, i: (0, bi, 0, g)),
                  pl.BlockSpec((None, None, nch, HEAD_DIM), lambda bi, g, i: (1, bi, 0, g)),
                  pl.BlockSpec((NSA_GROUP, tq, nch), lambda bi, g, i: (g, i, 0)),
                  pl.BlockSpec((nch, nsbp), lambda bi, g, i: (0, 0))],
        out_specs=(pl.BlockSpec((None, tq, gw), lambda bi, g, i: (bi, i, g)),
                   pl.BlockSpec((None, None, tq, nsbp), lambda bi, g, i: (bi, g, i, 0))),
        compiler_params=_cp("arbitrary", "arbitrary", "arbitrary"),
        name="cmp_attention",
    )(z3, kvc, kvc, bias, cover)


def cmp_tables(q_pos, tk, nch, nsbp, table):
    ncb = (tk - CMP_LEN) // CMP_STRIDE + 1
    nsb = -(-tk // SEL_BLK)
    n = jnp.arange(nch)
    dist = q_pos[:, None] - (n * CMP_STRIDE + CMP_LEN - 1)[None, :]
    vis = (dist >= 0) & (n < ncb)[None, :]
    bias = jnp.where(vis[None], _bias_of_dist(dist, table), NEG_INF)
    j = jnp.arange(nsbp)[None, :]
    i = n[:, None]
    cover = ((i * CMP_STRIDE < (j + 1) * SEL_BLK) & (i * CMP_STRIDE + CMP_LEN > j * SEL_BLK)
             & (i < ncb) & (j < nsb)).astype(F32)
    return bias, cover


def select_blocks(score, q_pos, nsb):
    j = jnp.arange(nsb)[None, :]
    cur = (q_pos // SEL_BLK)[:, None]
    forced = (j == 0) | (j == cur) | (j == cur - 1)
    s = score[..., :nsb]
    s = jnp.where((j > cur)[None, None], NEG_SCORE, s + jnp.where(forced, FORCE_BONUS, 0.0)[None, None])
    _, idx = lax.top_k(s, min(SEL_TOPK, nsb))
    return idx


def _nsa_attn_kernel(q_ref, ks_ref, vs_ref, kw_ref, vw_ref, msk_ref, tb_ref, oc_ref, gt_ref, o_ref):
    i = pl.program_id(2)
    tq = QBLK
    rows = NSA_GROUP * tq
    scale = HEAD_DIM ** -0.5
    qall = jnp.concatenate([q_ref[:, r * HEAD_DIM:(r + 1) * HEAD_DIM] for r in range(NSA_GROUP)], axis=0).astype(BF16)
    kpos0 = lax.broadcasted_iota(jnp.int32, (tq, tq), 0)
    qpos = i * tq + lax.broadcasted_iota(jnp.int32, (tq, tq), 1)
    selm_t = msk_ref[...]
    nsbp = selm_t.shape[0]
    per_blk = tq // SEL_BLK
    srow = lax.broadcasted_iota(jnp.int32, (tq, nsbp), 0) // SEL_BLK
    jcol = lax.broadcasted_iota(jnp.int32, (tq, nsbp), 1)
    pass_cols = rows

    def sel_valid(m):
        expand_t = (jcol == per_blk * m + srow).astype(F32)
        picked = _dot(expand_t, selm_t)
        return (picked > 0.5) & (m * tq + kpos0 <= qpos)

    def win_valid(m):
        dist = qpos - (m * tq + kpos0)
        return (dist >= 0) & (dist < WINDOW)

    def branch(k_ref, v_ref, lo, valid_fn):
        def body(m, carry):
            mx, l, acc = carry
            r0 = pl.multiple_of(m * tq, tq)
            k = k_ref[pl.ds(r0, tq), :].astype(BF16)
            v_t = v_ref[pl.ds(r0, tq), :].T.astype(BF16)
            valid = valid_fn(m)
            valid = jnp.concatenate([valid] * (pass_cols // tq), axis=1)
            bias = tb_ref[jnp.minimum(i - m, 2)]
            mx_o, l_o, acc_o = [], [], []
            for c in range(rows // pass_cols):
                cs = slice(c * pass_cols, (c + 1) * pass_cols)
                s = _dot_nt(k, qall[cs]) * scale + bias[:, cs]
                s = jnp.where(valid, s, NEG_INF)
                mx_new = jnp.maximum(mx[:, cs], jnp.max(s, axis=0, keepdims=True))
                alpha = jnp.exp(mx[:, cs] - mx_new)
                p = jnp.exp(s - mx_new)
                mx_o.append(mx_new)
                l_o.append(alpha * l[:, cs] + jnp.sum(p, axis=0, keepdims=True))
                acc_o.append(alpha * acc[:, cs] + _dot(v_t, p.astype(BF16)))
            return jnp.concatenate(mx_o, axis=1), jnp.concatenate(l_o, axis=1), jnp.concatenate(acc_o, axis=1)

        init = (jnp.full((1, rows), NEG_INF, F32), jnp.zeros((1, rows), F32), jnp.zeros((HEAD_DIM, rows), F32))
        _, l, acc = lax.fori_loop(lo, i + 1, body, init)
        return acc / l

    o_sel = branch(ks_ref, vs_ref, 0, sel_valid)
    o_win = branch(kw_ref, vw_ref, jnp.maximum(i - WINDOW // tq, 0), win_valid)
    gt = gt_ref[...]
    gates = 1.0 / (1.0 + jnp.exp(-gt))
    for r in range(NSA_GROUP):
        sl = slice(r * HEAD_DIM, (r + 1) * HEAD_DIM)
        o = (gates[:, 3 * r:3 * r + 1] * oc_ref[:, sl]
             + gates[:, 3 * r + 1:3 * r + 2] * o_sel[:, r * tq:(r + 1) * tq].T
             + gates[:, 3 * r + 2:3 * r + 3] * o_win[:, r * tq:(r + 1) * tq].T)
        o_ref[:, sl] = o.astype(BF16)


def nsa_attention_prompt(z3, sel_mask, tb, o_cmp, gate_lin):
    b, t, _ = z3.shape
    gw = NSA_GROUP * HEAD_DIM
    nsbp = sel_mask.shape[-2]
    kv0 = NSA_HEADS * HEAD_DIM // HEAD_DIM
    g4 = NSA_KV_HEADS

    def kv_spec(slot):
        return pl.BlockSpec((None, t, HEAD_DIM), lambda bi, g, i: (bi, 0, kv0 + slot * g4 + g))

    return pl.pallas_call(
        _nsa_attn_kernel,
        out_shape=jax.ShapeDtypeStruct((b, t, NSA_HEADS * HEAD_DIM), BF16),
        grid=(b, NSA_KV_HEADS, t // QBLK),
        in_specs=[pl.BlockSpec((None, QBLK, gw), lambda bi, g, i: (bi, i, g)),
                  kv_spec(2), kv_spec(3), kv_spec(4), kv_spec(5),
                  pl.BlockSpec((None, None, nsbp, QBLK), lambda bi, g, i: (bi, g, 0, i)),
                  pl.BlockSpec((3, None, QBLK, NSA_GROUP * QBLK), lambda bi, g, i: (0, g, 0, 0)),
                  pl.BlockSpec((None, QBLK, gw), lambda bi, g, i: (bi, i, g)),
                  pl.BlockSpec((None, None, QBLK, 3 * NSA_GROUP), lambda bi, g, i: (bi, g, i, 0))],
        out_specs=pl.BlockSpec((None, QBLK, gw), lambda bi, g, i: (bi, i, g)),
        compiler_params=_cp("arbitrary", "arbitrary", "arbitrary"),
        name="nsa_attention_prompt",
    )(z3, z3, z3, z3, z3, sel_mask, tb, o_cmp, gate_lin)


def _sel_sample_kernel(rb_ref, js_ref, q_ref, *refs, jlast, ksel):
    ng = NSA_KV_HEADS
    k_refs, v_refs, b_refs = refs[:ng], refs[ng:2 * ng], refs[2 * ng + 2:3 * ng + 2]
    kn_ref, vn_ref = refs[2 * ng], refs[2 * ng + 1]
    o_ref, m_s, l_s, a_s = refs[3 * ng + 2:]
    bi, kk = pl.program_id(0), pl.program_id(1)
    scale = HEAD_DIM ** -0.5

    @pl.when(kk == 0)
    def _():
        m_s[...] = jnp.full_like(m_s, NEG_INF)
        l_s[...] = jnp.zeros_like(l_s)
        a_s[...] = jnp.zeros_like(a_s)

    first_row = lax.broadcasted_iota(jnp.int32, (SEL_BLK, HEAD_DIM), 0) == 0
    for g in range(ng):
        is_new = js_ref[(bi * ng + g) * ksel + kk] == jlast
        k = jnp.where(is_new, jnp.where(first_row, kn_ref[g], 0.0), k_refs[g][:, g, :])
        v = jnp.where(is_new, jnp.where(first_row, vn_ref[g], 0.0), v_refs[g][:, g, :])
        s = _dot_nt(q_ref[g].astype(BF16), k.astype(BF16)) * scale + b_refs[g][...]
        valid = jnp.logical_or(jnp.logical_not(is_new), lax.broadcasted_iota(jnp.int32, s.shape, 1) == 0)
        s = jnp.where(valid, s, NEG_INF)
        mx = m_s[g]
        mx_new = jnp.maximum(mx, jnp.max(s, axis=-1, keepdims=True))
        alpha = jnp.exp(mx - mx_new)
        p = jnp.exp(s - mx_new)
        m_s[g] = mx_new
        l_s[g] = alpha * l_s[g] + jnp.sum(p, axis=-1, keepdims=True)
        a_s[g] = alpha * a_s[g] + _dot(p.astype(BF16), v.astype(BF16))

    @pl.when(kk == ksel - 1)
    def _():
        o_ref[...] = a_s[...] / l_s[...]


def sel_attention_sample(q4, cache, row_blk, jsel, k_new, v_new, bias, ni, jlast):
    db, g, r, hd = q4.shape
    ksel = jsel.shape[-1]
    per_page = PAGE_SIZE // SEL_BLK

    def blk_spec(slot, gi):
        def index(bi, kk, rb, js):
            blk = rb[(bi * g + gi) * ksel + kk]
            return (blk // per_page, ni, blk % per_page, slot, 0, 0)

        return pl.BlockSpec((None, None, SEL_BLK, None, g, HEAD_DIM), index)

    def bias_spec(gi):
        return pl.BlockSpec((None, None, r, SEL_BLK), lambda bi, kk, rb, js: (gi, js[(bi * g + gi) * ksel + kk], 0, 0))

    per_seq = pl.BlockSpec((None, g, r, hd), lambda bi, kk, rb, js: (bi, 0, 0, 0))
    new_row = pl.BlockSpec((None, g, 1, hd), lambda bi, kk, rb, js: (bi, 0, 0, 0))
    return pl.pallas_call(
        functools.partial(_sel_sample_kernel, jlast=jlast, ksel=ksel),
        out_shape=jax.ShapeDtypeStruct((db, g, r, hd), F32),
        grid_spec=pltpu.PrefetchScalarGridSpec(
            num_scalar_prefetch=2,
            grid=(db, ksel),
            in_specs=[per_seq] + [blk_spec(2, gi) for gi in range(g)] + [blk_spec(3, gi) for gi in range(g)]
            + [new_row, new_row] + [bias_spec(gi) for gi in range(g)],
            out_specs=per_seq,
            scratch_shapes=[pltpu.VMEM((g, r, 1), F32), pltpu.VMEM((g, r, 1), F32), pltpu.VMEM((g, r, hd), F32)]),
        compiler_params=_cp("arbitrary", "arbitrary"),
        name="sel_attention_sample",
    )(row_blk.reshape(-1), jsel.reshape(-1), q4, *([cache] * (2 * g)), k_new, v_new, *([bias] * g))


def _win_sample_kernel(q_ref, kb_ref, vb_ref, kn_ref, vn_ref, bw_ref, b0_ref, oc_ref, os_ref, gt_ref, o_ref):
    scale = HEAD_DIM ** -0.5
    q = q_ref[...]
    wb = kb_ref.shape[0]
    s = _dot_nt(q.astype(BF16), kb_ref[...].astype(BF16)) * scale + bw_ref[...]
    dist = wb - lax.broadcasted_iota(jnp.int32, s.shape, 1)
    s = jnp.where(dist < WINDOW, s, NEG_INF)
    s_new = jnp.sum(q * kn_ref[...], axis=-1, keepdims=True) * scale + b0_ref[...][:, 0:1]
    mx = jnp.maximum(jnp.max(s, axis=-1, keepdims=True), s_new)
    e = jnp.exp(s - mx)
    e_new = jnp.exp(s_new - mx)
    l = jnp.sum(e, axis=-1, keepdims=True) + e_new
    o_win = (_dot(e.astype(BF16), vb_ref[...].astype(BF16)) + e_new * vn_ref[...]) / l
    gates = 1.0 / (1.0 + jnp.exp(-gt_ref[...]))
    o_ref[...] = gates[:, 0:1] * oc_ref[...] + gates[:, 1:2] * os_ref[...] + gates[:, 2:3] * o_win


def win_attention_sample(q4, win3, k_new, v_new, bias_w, bias0, o_cmp4, o_sel4, gate4, ni):
    db, g, r, hd = q4.shape
    wb = win3.shape[1]

    def small(shape_last):
        return pl.BlockSpec((None, None, r, shape_last), lambda bi, gi: (bi, gi, 0, 0))

    return pl.pallas_call(
        _win_sample_kernel,
        out_shape=jax.ShapeDtypeStruct((db, g, r, hd), F32),
        grid=(db, g),
        in_specs=[small(hd),
                  pl.BlockSpec((None, wb, hd), lambda bi, gi: (ni * db + bi, 0, gi)),
                  pl.BlockSpec((None, wb, hd), lambda bi, gi: (ni * db + bi, 0, g + gi)),
                  pl.BlockSpec((None, None, 1, hd), lambda bi, gi: (bi, gi, 0, 0)),
                  pl.BlockSpec((None, None, 1, hd), lambda bi, gi: (bi, gi, 0, 0)),
                  pl.BlockSpec((None, r, wb), lambda bi, gi: (gi, 0, 0)),
                  pl.BlockSpec((None, r, LANE), lambda bi, gi: (gi, 0, 0)),
                  small(hd), small(hd), small(3)],
        out_specs=small(hd),
        compiler_params=_cp("arbitrary", "arbitrary"),
        name="win_attention_sample",
    )(q4, win3, win3, k_new, v_new, bias_w, bias0, o_cmp4, o_sel4, gate4)


def nsa_layer(h_all, bt, b, t, db, ni, cache_nsa, state_nsa_win, page_table, rel_bias,
              nsa_w_in, nsa_cmp_pe, nsa_cmp_w1, nsa_cmp_b1, nsa_cmp_w2, nsa_cmp_b2):
    nq = NSA_HEADS * HEAD_DIM
    nqkv = nq + 6 * NSA_KV
    z_p, z_s = dense_prompt_and_sample(h_all, h_all, nsa_w_in, (ni,), bt, ncols=nqkv)
    w_gate = jnp.zeros((h_all.shape[1], LANE), F32).at[:, :3 * NSA_HEADS].set(nsa_w_in[ni, :, nqkv:])
    zg_p, zg_s = dense_prompt_and_sample(h_all, h_all, w_gate, (), bt)
    wz = z_p.shape[1]
    z3 = z_p.reshape(b, t, wz)
    n_layers = cache_nsa.shape[1]
    past_len = page_table.shape[1] * PAGE_SIZE
    pos_p = jnp.arange(t)

    ncb, nch, nsb, nsbp = _cmp_geometry(t)
    page_rows = (jnp.arange(b)[:, None] * (t // PAGE_SIZE) + jnp.arange(t // PAGE_SIZE)[None, :])
    rows_p = z3[:, :, nq:nq + 4 * NSA_KV].reshape(b, t, 4, NSA_KV_HEADS, HEAD_DIM)
    src_p = rows_p.reshape(bt // PAGE_SIZE, 1, PAGE_SIZE, 4, NSA_KV_HEADS, HEAD_DIM)
    a_p = compress_chunks(src_p, page_rows, 0, nsa_cmp_w1, ni, b, nch)
    kvc_p = compress_blocks(a_p, nsa_cmp_pe, nsa_cmp_w1, nsa_cmp_b1, nsa_cmp_w2, nsa_cmp_b2, ni)
    bias_p, cover_p = cmp_tables(pos_p, t, nch, nsbp, rel_bias)
    o_cmp_p, score_p = cmp_attention(z3, kvc_p, bias_p, cover_p, QBLK)
    idx_p = select_blocks(score_p, pos_p, nsb)
    sel_mask = (idx_p[:, :, None, :, :] == jnp.arange(nsbp)[None, None, :, None, None]).any(axis=-1).astype(F32)
    ii = jnp.arange(QBLK)
    tb = _bias_of_dist(jnp.arange(3)[:, None, None] * QBLK + ii[None, None, :] - ii[None, :, None], rel_bias)
    tb = tb.reshape(NSA_KV_HEADS, NSA_GROUP, 3, QBLK, QBLK).transpose((2, 0, 3, 1, 4))
    tb = tb.reshape(3, NSA_KV_HEADS, QBLK, NSA_GROUP * QBLK)
    gate_lin = zg_p[:, :3 * NSA_HEADS].reshape(b, t, NSA_KV_HEADS, 3 * NSA_GROUP).transpose((0, 2, 1, 3))
    o_p = nsa_attention_prompt(z3, sel_mask, tb, o_cmp_p, gate_lin)
    keep = min(WINDOW, t)
    win_p = z3[:, t - keep:, nq + 4 * NSA_KV:nq + 6 * NSA_KV].reshape(b, keep, 2, NSA_KV_HEADS, HEAD_DIM)

    zs = z_s[:db]
    tk = past_len + DEC_SEQ
    ncb_s, nch_s, nsb_s, nsbp_s = _cmp_geometry(tk)
    assert (ncb_s + 1) * CMP_STRIDE <= past_len
    a_s = compress_chunks(cache_nsa, page_table, ni, nsa_cmp_w1, ni, db, nch_s)
    kvc_s = compress_blocks(a_s, nsa_cmp_pe, nsa_cmp_w1, nsa_cmp_b1, nsa_cmp_w2, nsa_cmp_b2, ni)
    pos_s = jnp.full((SUBLANE,), past_len, jnp.int32)
    bias_s, cover_s = cmp_tables(pos_s, tk, nch_s, nsbp_s, rel_bias)
    zs3 = jnp.zeros((db, SUBLANE, wz), F32).at[:, 0].set(zs)
    o_cmp_s, score_s = cmp_attention(zs3, kvc_s, bias_s, cover_s, SUBLANE)
    idx_s = select_blocks(score_s[:, :, 0:1], pos_s[0:1], nsb_s)[:, :, 0]
    jlast = past_len // SEL_BLK
    per_page = PAGE_SIZE // SEL_BLK
    page_of = jnp.take_along_axis(page_table, jnp.minimum(idx_s, jlast - 1).reshape(db, -1) // per_page, axis=1)
    row_blk = (page_of.reshape(idx_s.shape) * per_page + jnp.minimum(idx_s, jlast - 1) % per_page).astype(jnp.int32)
    q4 = zs[:, :nq].reshape(db, NSA_KV_HEADS, NSA_GROUP, HEAD_DIM)
    kv_new = zs[:, nq:nq + 6 * NSA_KV].reshape(db, 6, NSA_KV_HEADS, 1, HEAD_DIM)
    kpos = jnp.arange(nsb_s * SEL_BLK).reshape(nsb_s, SEL_BLK)
    bias_sel = _bias_of_dist(past_len - kpos, rel_bias).reshape(NSA_KV_HEADS, NSA_GROUP, nsb_s, SEL_BLK).transpose((0, 2, 1, 3))
    o_sel_s = sel_attention_sample(q4, cache_nsa, row_blk, idx_s.astype(jnp.int32), kv_new[:, 2], kv_new[:, 3],
                                   bias_sel, ni, jlast)
    wb = state_nsa_win.shape[2]
    win3 = state_nsa_win.reshape(state_nsa_win.shape[0] * db, wb, 2 * NSA_KV)
    bias_w = _bias_of_dist(wb - jnp.arange(wb), rel_bias).reshape(NSA_KV_HEADS, NSA_GROUP, wb)
    bias0 = jnp.broadcast_to(_bias_of_dist(jnp.zeros((1,), jnp.int32), rel_bias).reshape(NSA_KV_HEADS, NSA_GROUP, 1),
                             (NSA_KV_HEADS, NSA_GROUP, LANE))
    gate4 = zg_s[:db, :3 * NSA_HEADS].reshape(db, NSA_KV_HEADS, NSA_GROUP, 3)
    o_cmp4 = o_cmp_s[:, 0].reshape(db, NSA_KV_HEADS, NSA_GROUP, HEAD_DIM)
    o_s = win_attention_sample(q4, win3, kv_new[:, 4], kv_new[:, 5], bias_w, bias0, o_cmp4, o_sel_s, gate4, ni)
    rows_s = zs[:, nq:nq + 4 * NSA_KV].reshape(db, DEC_SEQ, 4, NSA_KV_HEADS, HEAD_DIM)
    new_win = zs[:, nq + 4 * NSA_KV:nq + 6 * NSA_KV].reshape(db, DEC_SEQ, 2, NSA_KV_HEADS, HEAD_DIM)
    win_all = jnp.concatenate([state_nsa_win[ni], new_win], axis=1)
    keep_s = min(WINDOW, wb + DEC_SEQ)
    win_s = win_all[:, wb + DEC_SEQ - keep_s:]
    return o_p.reshape(bt, nq), o_s.reshape(db, nq), rows_p, rows_s, win_p, win_s


def kernel(x_prompt, x_sample, c_prompt, c_sample, state_ret, cache_nsa, state_nsa_win, page_table, rel_bias,
           ada_w, ada_b, ln_g, ln_b, ret_w_in, ret_gn_w, ret_w_out, nsa_w_in, nsa_cmp_pe, nsa_cmp_w1, nsa_cmp_b1,
           nsa_cmp_w2, nsa_cmp_b2, nsa_w_out, moe_w_router, moe_b_router, moe_w_up, moe_b_up, moe_w_down, moe_b_down):
    b, t, d = x_prompt.shape
    db, ds, _ = x_sample.shape
    assert ds == DEC_SEQ == 1 and db <= SUBLANE and t % ROW_TILE == 0 and t % QBLK == 0
    bt = b * t
    npad = bt + ROW_TILE
    ntok = bt + db
    tiles_per_seq = t // ROW_TILE
    past_len = page_table.shape[1] * PAGE_SIZE

    x_all = jnp.zeros((npad, d), F32).at[:bt].set(x_prompt.reshape(bt, d)).at[bt:ntok].set(x_sample.reshape(db, d))
    rc = -(-(b + db) // SUBLANE) * SUBLANE
    c_all = jnp.zeros((rc, d), F32).at[:b].set(c_prompt).at[b:b + db].set(c_sample)
    mods = ada_all(c_all, ada_w, ada_b)

    def mod_slabs(li, sub):
        m = mods[li * 2 + sub]
        return tuple(_slabs(m[:, k * d:(k + 1) * d], b, db) for k in range(3))

    def tail_rows(a_s, width, dtype):
        return jnp.zeros((ROW_TILE, width), dtype).at[:db].set(a_s.astype(dtype))

    ret_sp, ret_ss, rows_p, rows_s, win_p, win_s = [], [], [], [], [], []
    for li in range(DEPTH):
        sh, sc, gt = mod_slabs(li, 0)
        h_all = modulate(x_all, sc, sh, tiles_per_seq, b)
        if li % N_MIXERS == 0:
            ri = li // N_MIXERS
            z_p, z_s = dense_prompt_and_sample(h_all, h_all, ret_w_in, (ri,), bt)
            a_p, sp = retention(z_p.reshape(b, t, RET_IN), jnp.arange(t), None, ret_gn_w, ri, min(RET_CHUNK, t) if t % RET_CHUNK == 0 else t)
            zs3 = jnp.zeros((db, SUBLANE, RET_IN), F32).at[:, 0].set(z_s[:db])
            a_s, ss = retention(zs3, jnp.full((SUBLANE,), past_len), state_ret, ret_gn_w, ri, DEC_SEQ)
            ret_sp.append(sp)
            ret_ss.append(ss)
            nv = RET_HEADS * RET_DV
            y_all = dense_prompt_and_sample(a_p.reshape(bt, nv), tail_rows(a_s[:, 0], nv, BF16), ret_w_out, (ri,), bt,
                                            into_rows=npad)
        else:
            ni = li // N_MIXERS
            o_p, o_s, rp, rs, wp, wsb = nsa_layer(h_all, bt, b, t, db, ni, cache_nsa, state_nsa_win, page_table, rel_bias,
                                                  nsa_w_in, nsa_cmp_pe, nsa_cmp_w1, nsa_cmp_b1, nsa_cmp_w2, nsa_cmp_b2)
            rows_p.append(rp)
            rows_s.append(rs)
            win_p.append(wp)
            win_s.append(wsb)
            nq = NSA_HEADS * HEAD_DIM
            y_all = dense_prompt_and_sample(o_p, tail_rows(o_s, nq, BF16), nsa_w_out, (ni,), bt, into_rows=npad)
        x_all = resid_ln(x_all, y_all, gt, ln_g, ln_b, li, 0, tiles_per_seq, b)

        sh, sc, gt = mod_slabs(li, 1)
        h_f32, logits = modulate_router(x_all, sc, sh, moe_w_router, moe_b_router, li, tiles_per_seq, b)
        y_all = moe(h_f32, logits, ntok, li, moe_w_up, moe_b_up, moe_w_down, moe_b_down)
        x_all = resid_ln(x_all, y_all, gt, ln_g, ln_b, li, 1, tiles_per_seq, b)

    y_prompt = x_all[:bt].reshape(b, t, d)
    y_sample = x_all[bt:ntok].reshape(db, ds, d)
    return (y_prompt, y_sample, jnp.stack(ret_sp), jnp.stack(ret_ss), jnp.stack(rows_p, axis=1),
            jnp.stack(rows_s, axis=1), jnp.stack(win_p), jnp.stack(win_s))
```
